```python
import math
import jax
import jax.numpy as jnp
from jax import lax
import numpy as np

D_MODEL = 1024
BATCH = 8
SEQ = 2048
DEPTH = 2
DEC_BATCH = 128
DEC_SEQ = 8
PAST_LEN = 2048
PAGE_SIZE = 128

N_META = 16
SSD_EXPAND = 2
SSD_INNER = SSD_EXPAND * D_MODEL
SSD_HEAD_DIM = 64
SSD_HEADS = SSD_INNER // SSD_HEAD_DIM
SSD_STATE = 128
SSD_GROUPS = 4
SSD_CONV = 4
SSD_CHUNK = 128
SSD_XBC = SSD_INNER + 2 * SSD_GROUPS * SSD_STATE
DT_MIN = 0.001
DT_MAX = 0.1
DA_HEAD_DIM = 64
DA_HEADS = D_MODEL // (2 * DA_HEAD_DIM)
DA_WIDTH = DA_HEADS * 2 * DA_HEAD_DIM
ROT_DIM = DA_HEAD_DIM // 4
ROPE_THETA = 500000.0
Q_BLOCK = 128
CC_CH = D_MODEL
CC_KERNEL = 31
FF_HIDDEN = 2816
N_BRANCH = 3
IN_SIZES = (SSD_INNER, SSD_XBC, SSD_HEADS, DA_WIDTH, DA_WIDTH, DA_WIDTH, 2 * CC_CH, N_BRANCH * D_MODEL)
IN_COLS = SSD_INNER + SSD_XBC + SSD_HEADS + 3 * DA_WIDTH + 2 * CC_CH + N_BRANCH * D_MODEL
ALPHA = (2.0 * DEPTH) ** 0.25
BETA = (8.0 * DEPTH) ** -0.25
LN_EPS = 1e-5
F32 = jnp.float32

kernel_name = 'hybrid_ssd_diffattn_conformer_decoder_step'


def _split_points():
    pts, acc = [], 0
    for s in IN_SIZES[:-1]:
        acc += s
        pts.append(acc)
    return pts


def layer_norm(x, g, b):
    xf = x.astype(F32)
    xc = xf - jnp.mean(xf, axis=-1, keepdims=True)
    var = jnp.mean(xc * xc, axis=-1, keepdims=True)
    return (xc * lax.rsqrt(var + LN_EPS) * g.astype(F32) + b.astype(F32)).astype(x.dtype)


def rms_norm_f32(x, w):
    xf = x.astype(F32)
    return xf * lax.rsqrt(jnp.mean(xf * xf, axis=-1, keepdims=True) + LN_EPS) * w.astype(F32)


def swiglu(x, w_gu, w_down):
    g, u = jnp.split(x @ w_gu, 2, axis=-1)
    return (jax.nn.silu(g) * u) @ w_down


def causal_dwconv(x_ext, w):
    return lax.conv_general_dilated(x_ext, w.astype(x_ext.dtype)[:, None, :], window_strides=(1,), padding='VALID',
                                    dimension_numbers=('NWC', 'WIO', 'NWC'), feature_group_count=x_ext.shape[-1])


def rope_partial(x, pos):
    half = ROT_DIM // 2
    inv_freq = ROPE_THETA ** (-jnp.arange(half, dtype=F32) * 2.0 / ROT_DIM)
    ang = pos.astype(F32)[:, None] * inv_freq[None, :]
    cos = jnp.cos(ang)[None, :, None, None, :]
    sin = jnp.sin(ang)[None, :, None, None, :]
    xf = x.astype(F32)
    x1, x2 = xf[..., :half], xf[..., half:ROT_DIM]
    return jnp.concatenate([x1 * cos - x2 * sin, x2 * cos + x1 * sin, xf[..., ROT_DIM:]], axis=-1).astype(x.dtype)


def ssd_chunk(h0, xs, dt, bm, cm, a):
    b, L, H, P = xs.shape
    G, N = bm.shape[2], bm.shape[3]
    hg = H // G
    acum = jnp.cumsum(dt * a, axis=1)
    causal = jnp.tril(jnp.ones((L, L), dtype=bool))[None, :, :, None]
    decay = jnp.exp(jnp.where(causal, acum[:, :, None, :] - acum[:, None, :, :], -jnp.inf)).reshape(b, L, L, G, hg)
    xdt = (xs * dt[..., None]).reshape(b, L, G, hg, P)
    cb = jnp.einsum('btgn,bsgn->btsg', cm, bm)
    y = jnp.einsum('btsgh,bsghp->btghp', cb[..., None] * decay, xdt)
    h0g = h0.reshape(b, G, hg, P, N)
    y = y + jnp.einsum('btgn,bghpn->btghp', cm, h0g) * jnp.exp(acum).reshape(b, L, G, hg)[..., None]
    to_end = jnp.exp(acum[:, -1:, :] - acum).reshape(b, L, G, hg, 1)
    h1 = h0g * jnp.exp(acum[:, -1]).reshape(b, G, hg, 1, 1) + jnp.einsum('bsgn,bsghp->bghpn', bm, xdt * to_end)
    return h1.reshape(b, H, P, N), y.reshape(b, L, H, P)


def ssd_scan(h0, xs, dt, bm, cm, a, prompt):
    if not prompt:
        return ssd_chunk(h0, xs, dt, bm, cm, a)
    h, y_meta = ssd_chunk(h0, xs[:, :N_META], dt[:, :N_META], bm[:, :N_META], cm[:, :N_META], a)
    b = xs.shape[0]
    n_c = (xs.shape[1] - N_META) // SSD_CHUNK

    def to_chunks(t):
        return jnp.moveaxis(t[:, N_META:].reshape((b, n_c, SSD_CHUNK) + t.shape[2:]), 1, 0)

    def step(hc, inp):
        return ssd_chunk(hc, *inp, a)

    h, y_rest = lax.scan(step, h, (to_chunks(xs), to_chunks(dt), to_chunks(bm), to_chunks(cm)))
    y_rest = jnp.moveaxis(y_rest, 0, 1).reshape((b, n_c * SSD_CHUNK) + xs.shape[2:])
    return h, jnp.concatenate([y_meta, y_rest], axis=1)


def diff_attend(q, k, v, q_pos, k_pos, lam):
    s = jnp.einsum('bqhcd,bkhcd->bhcqk', q, k, preferred_element_type=F32) * (DA_HEAD_DIM ** -0.5)
    s = jnp.where((k_pos[None, :] <= q_pos[:, None])[None, None, None], s, -jnp.inf)
    p = jax.nn.softmax(s, axis=-1)
    att = p[:, :, 0] - lam * p[:, :, 1]
    return jnp.einsum('bhqk,bkhe->bqhe', att.astype(v.dtype), v)


def prompt_diff_attention(q, k, v, lam):
    b, T = q.shape[0], q.shape[1]
    pos = jnp.arange(T)
    o_meta = diff_attend(q[:, :N_META], k[:, :N_META], v[:, :N_META], pos[:N_META], pos[:N_META], lam)
    n_blk = (T - N_META) // Q_BLOCK

    def blk(i):
        start = N_META + i * Q_BLOCK
        qb = lax.dynamic_slice_in_dim(q, start, Q_BLOCK, axis=1)
        return diff_attend(qb, k, v, start + jnp.arange(Q_BLOCK), pos, lam)

    o_rest = lax.map(blk, jnp.arange(n_blk))
    o_rest = jnp.moveaxis(o_rest, 0, 1).reshape((b, n_blk * Q_BLOCK) + o_rest.shape[3:])
    return jnp.concatenate([o_meta, o_rest], axis=1)


def token_mixer(h, pos, lp, lam_init, conv_prev, ssm_h0, cc_prev, k_past, v_past):
    prompt = k_past is None
    bt, L, _ = h.shape
    dtype = h.dtype
    z, xbc, dt, q, k, v, glu, gate = jnp.split(h @ lp['w_in'], _split_points(), axis=-1)
    if prompt:
        conv_prev = jnp.zeros((bt, SSD_CONV - 1, SSD_XBC), dtype)
        ssm_h0 = jnp.zeros((bt, SSD_HEADS, SSD_HEAD_DIM, SSD_STATE), F32)
        cc_prev = jnp.zeros((bt, CC_KERNEL - 1, CC_CH), dtype)

    xbc_ext = jnp.concatenate([conv_prev.astype(dtype), xbc], axis=1)
    new_conv = xbc_ext[:, -(SSD_CONV - 1):]
    xbc_c = jax.nn.silu(causal_dwconv(xbc_ext, lp['ssd_conv_w']) + lp['ssd_conv_b'].astype(dtype)).astype(F32)
    xs, bm, cm = jnp.split(xbc_c, [SSD_INNER, SSD_INNER + SSD_GROUPS * SSD_STATE], axis=-1)
    xs = xs.reshape(bt, L, SSD_HEADS, SSD_HEAD_DIM)
    bm = bm.reshape(bt, L, SSD_GROUPS, SSD_STATE)
    cm = cm.reshape(bt, L, SSD_GROUPS, SSD_STATE)
    dtp = jax.nn.softplus(dt.astype(F32) + lp['ssd_dt_bias'].astype(F32))
    a = -jnp.exp(lp['ssd_a_log'].astype(F32))
    h_fin, y = ssd_scan(ssm_h0.astype(F32), xs, dtp, bm, cm, a, prompt)
    y = (y + lp['ssd_d'].astype(F32)[:, None] * xs).reshape(bt, L, SSD_INNER)
    yg = (y * jax.nn.silu(z.astype(F32))).reshape(bt, L, SSD_GROUPS, SSD_INNER // SSD_GROUPS)
    yg = yg * lax.rsqrt(jnp.mean(yg * yg, axis=-1, keepdims=True) + LN_EPS)
    ssd_branch = (yg.reshape(bt, L, SSD_INNER) * lp['ssd_norm_w'].astype(F32)).astype(dtype) @ lp['ssd_out']

    q = rope_partial(q.reshape(bt, L, DA_HEADS, 2, DA_HEAD_DIM), pos)
    k = rope_partial(k.reshape(bt, L, DA_HEADS, 2, DA_HEAD_DIM), pos)
    v = v.reshape(bt, L, DA_HEADS, 2 * DA_HEAD_DIM)
    lam = (jnp.exp(jnp.sum(lp['da_lambda_q1'].astype(F32) * lp['da_lambda_k1'].astype(F32)))
           - jnp.exp(jnp.sum(lp['da_lambda_q2'].astype(F32) * lp['da_lambda_k2'].astype(F32))) + lam_init)
    if prompt:
        o = prompt_diff_attention(q, k, v, lam)
    else:
        past = k_past.shape[1]
        k_all = jnp.concatenate([k_past.astype(k.dtype), k], axis=1)
        v_all = jnp.concatenate([v_past.astype(v.dtype), v], axis=1)
        o = diff_attend(q, k_all, v_all, pos, jnp.arange(past + L), lam)
    o = rms_norm_f32(o, lp['da_subln_w']) * (1.0 - lam_init)
    da_branch = o.reshape(bt, L, DA_WIDTH).astype(dtype) @ lp['da_out']

    ga, gb = jnp.split(glu, 2, axis=-1)
    u = ga * jax.nn.sigmoid(gb)
    u_ext = jnp.concatenate([cc_prev.astype(dtype), u], axis=1)
    new_cc = u_ext[:, -(CC_KERNEL - 1):]
    c = causal_dwconv(u_ext, lp['cc_conv_w']) + lp['cc_conv_b'].astype(dtype)
    c = jax.nn.silu(layer_norm(c, lp['cc_ln_g'], lp['cc_ln_b']))
    cc_branch = c @ lp['cc_out']

    gates = jax.nn.sigmoid((gate + lp['b_gate']).astype(F32)).reshape(bt, L, N_BRANCH, D_MODEL).astype(dtype)
    merged = gates[:, :, 0] * ssd_branch + gates[:, :, 1] * da_branch + gates[:, :, 2] * cc_branch
    mix = merged @ lp['w_out']
    k_rows = k.reshape(bt, L, DA_HEADS, 2 * DA_HEAD_DIM)
    return mix, (k_rows, v, h_fin.astype(dtype), new_conv, new_cc)


def decoder_layer(h, pos, lp, lam_init, conv_prev, ssm_h0, cc_prev, k_past, v_past):
    g, b = lp['ln_g'], lp['ln_b']
    h = layer_norm(ALPHA * h + 0.5 * swiglu(h, lp['w_ff1_gu'], lp['w_ff1_down']), g[0], b[0])
    mix, new_state = token_mixer(h, pos, lp, lam_init, conv_prev, ssm_h0, cc_prev, k_past, v_past)
    h = layer_norm(ALPHA * h + mix, g[1], b[1])
    h = layer_norm(ALPHA * h + 0.5 * swiglu(h, lp['w_ff2_gu'], lp['w_ff2_down']), g[2], b[2])
    return h, new_state


def setup_inputs(seed: int = 0) -> dict:
    key = jax.random.key(seed)
    keys = iter(jax.random.split(key, 48))

    def nrm(shape, scale):
        return jax.random.normal(next(keys), shape, F32) * scale

    n_pages = PAST_LEN // PAGE_SIZE
    n_pool = (DEC_BATCH * n_pages * 5) // 4
    page_table = jax.random.permutation(next(keys), n_pool)[: DEC_BATCH * n_pages].reshape(DEC_BATCH, n_pages).astype(jnp.int32)
    dt0 = jnp.exp(jax.random.uniform(next(keys), (DEPTH, SSD_HEADS), F32, math.log(DT_MIN), math.log(DT_MAX)))
    dt_bias = dt0 + jnp.log(-jnp.expm1(-dt0))
    a_log = jnp.log(jax.random.uniform(next(keys), (DEPTH, SSD_HEADS), F32, 1.0, 16.0))
    d_in = D_MODEL ** -0.5
    return {
        'x_prompt': nrm((BATCH, SEQ, D_MODEL), 1.0),
        'x_sample': nrm((DEC_BATCH, DEC_SEQ, D_MODEL), 1.0),
        'cache_k': nrm((DEPTH, n_pool, PAGE_SIZE, DA_HEADS, 2 * DA_HEAD_DIM), 1.0),
        'cache_v': nrm((DEPTH, n_pool, PAGE_SIZE, DA_HEADS, 2 * DA_HEAD_DIM), 1.0),
        'state_ssm': nrm((DEPTH, DEC_BATCH, SSD_HEADS, SSD_HEAD_DIM, SSD_STATE), 0.3),
        'state_ssd_conv': nrm((DEPTH, DEC_BATCH, SSD_CONV - 1, SSD_XBC), 1.0),
        'state_conf_conv': nrm((DEPTH, DEC_BATCH, CC_KERNEL - 1, CC_CH), 0.5),
        'page_table': page_table,
        'meta_tokens': nrm((N_META, D_MODEL), 1.0),
        'ln_g': 1.0 + nrm((DEPTH, 3, D_MODEL), 0.02),
        'ln_b': nrm((DEPTH, 3, D_MODEL), 0.02),
        'w_ff1_gu': nrm((DEPTH, D_MODEL, 2 * FF_HIDDEN), d_in),
        'w_ff1_down': nrm((DEPTH, FF_HIDDEN, D_MODEL), FF_HIDDEN ** -0.5 * BETA),
        'w_ff2_gu': nrm((DEPTH, D_MODEL, 2 * FF_HIDDEN), d_in),
        'w_ff2_down': nrm((DEPTH, FF_HIDDEN, D_MODEL), FF_HIDDEN ** -0.5 * BETA),
        'w_in': nrm((DEPTH, D_MODEL, IN_COLS), d_in),
        'b_gate': nrm((DEPTH, N_BRANCH * D_MODEL), 0.02),
        'ssd_conv_w': nrm((DEPTH, SSD_CONV, SSD_XBC), SSD_CONV ** -0.5),
        'ssd_conv_b': nrm((DEPTH, SSD_XBC), 0.02),
        'ssd_dt_bias': dt_bias,
        'ssd_a_log': a_log,
        'ssd_d': 1.0 + nrm((DEPTH, SSD_HEADS), 0.02),
        'ssd_norm_w': 1.0 + nrm((DEPTH, SSD_INNER), 0.02),
        'ssd_out': nrm((DEPTH, SSD_INNER, D_MODEL), SSD_INNER ** -0.5 * BETA),
        'da_lambda_q1': nrm((DEPTH, DA_HEAD_DIM), 0.1),
        'da_lambda_k1': nrm((DEPTH, DA_HEAD_DIM), 0.1),
        'da_lambda_q2': nrm((DEPTH, DA_HEAD_DIM), 0.1),
        'da_lambda_k2': nrm((DEPTH, DA_HEAD_DIM), 0.1),
        'da_subln_w': 1.0 + nrm((DEPTH, 2 * DA_HEAD_DIM), 0.02),
        'da_out': nrm((DEPTH, DA_WIDTH, D_MODEL), DA_WIDTH ** -0.5 * BETA),
        'cc_conv_w': nrm((DEPTH, CC_KERNEL, CC_CH), CC_KERNEL ** -0.5),
        'cc_conv_b': nrm((DEPTH, CC_CH), 0.02),
        'cc_ln_g': 1.0 + nrm((DEPTH, CC_CH), 0.02),
        'cc_ln_b': nrm((DEPTH, CC_CH), 0.02),
        'cc_out': nrm((DEPTH, CC_CH, D_MODEL), CC_CH ** -0.5 * BETA),
        'w_out': nrm((DEPTH, D_MODEL, D_MODEL), d_in * BETA),
    }


def reference(x_prompt, x_sample, cache_k, cache_v, state_ssm, state_ssd_conv, state_conf_conv, page_table,
              meta_tokens, ln_g, ln_b, w_ff1_gu, w_ff1_down, w_ff2_gu, w_ff2_down, w_in, b_gate,
              ssd_conv_w, ssd_conv_b, ssd_dt_bias, ssd_a_log, ssd_d, ssd_norm_w, ssd_out,
              da_lambda_q1, da_lambda_k1, da_lambda_q2, da_lambda_k2, da_subln_w, da_out,
              cc_conv_w, cc_conv_b, cc_ln_g, cc_ln_b, cc_out, w_out):
    bp, seq = x_prompt.shape[0], x_prompt.shape[1]
    ds = x_sample.shape[0]
    dtype = x_prompt.dtype
    meta = jnp.broadcast_to(meta_tokens.astype(dtype)[None], (bp, N_META, D_MODEL))
    hp = jnp.concatenate([meta, x_prompt], axis=1)
    pos_p = jnp.arange(N_META + seq)
    hs = x_sample
    past = page_table.shape[1] * cache_k.shape[2]
    pos_s = past + jnp.arange(x_sample.shape[1])
    st_p, st_s = [], []
    for l in range(DEPTH):
        lp = dict(ln_g=ln_g[l], ln_b=ln_b[l], w_ff1_gu=w_ff1_gu[l], w_ff1_down=w_ff1_down[l],
                  w_ff2_gu=w_ff2_gu[l], w_ff2_down=w_ff2_down[l], w_in=w_in[l], b_gate=b_gate[l],
                  ssd_conv_w=ssd_conv_w[l], ssd_conv_b=ssd_conv_b[l], ssd_dt_bias=ssd_dt_bias[l],
                  ssd_a_log=ssd_a_log[l], ssd_d=ssd_d[l], ssd_norm_w=ssd_norm_w[l], ssd_out=ssd_out[l],
                  da_lambda_q1=da_lambda_q1[l], da_lambda_k1=da_lambda_k1[l], da_lambda_q2=da_lambda_q2[l],
                  da_lambda_k2=da_lambda_k2[l], da_subln_w=da_subln_w[l], da_out=da_out[l],
                  cc_conv_w=cc_conv_w[l], cc_conv_b=cc_conv_b[l], cc_ln_g=cc_ln_g[l], cc_ln_b=cc_ln_b[l],
                  cc_out=cc_out[l], w_out=w_out[l])
        lam_init = 0.8 - 0.6 * math.exp(-0.3 * l)
        hp, sp = decoder_layer(hp, pos_p, lp, lam_init, None, None, None, None, None)
        k_past = cache_k[l, page_table].reshape(ds, past, DA_HEADS, 2, DA_HEAD_DIM)
        v_past = cache_v[l, page_table].reshape(ds, past, DA_HEADS, 2 * DA_HEAD_DIM)
        hs, ss = decoder_layer(hs, pos_s, lp, lam_init, state_ssd_conv[l], state_ssm[l], state_conf_conv[l], k_past, v_past)
        st_p.append(sp)
        st_s.append(ss)
    y_prompt = hp[:, N_META:]
    y_sample = hs
    k_prompt = jnp.stack([s[0] for s in st_p])
    v_prompt = jnp.stack([s[1] for s in st_p])
    k_sample = jnp.stack([s[0] for s in st_s])
    v_sample = jnp.stack([s[1] for s in st_s])
    ssm_prompt = jnp.stack([s[2] for s in st_p])
    ssm_sample = jnp.stack([s[2] for s in st_s])
    ssd_conv_prompt = jnp.stack([s[3] for s in st_p])
    ssd_conv_sample = jnp.stack([s[3] for s in st_s])
    conf_conv_prompt = jnp.stack([s[4] for s in st_p])
    conf_conv_sample = jnp.stack([s[4] for s in st_s])
    return (y_prompt, y_sample, k_prompt, v_prompt, k_sample, v_sample, ssm_prompt, ssm_sample,
            ssd_conv_prompt, ssd_conv_sample, conf_conv_prompt, conf_conv_sample)
```

```python
import functools
import math

import jax
import jax.numpy as jnp
from jax import lax
from jax.experimental import pallas as pl
from jax.experimental.pallas import tpu as pltpu

F32 = jnp.float32
BF16 = jnp.bfloat16

D_MODEL = 1024
N_META = 16
SSD_INNER = 2048
SSD_HEAD_DIM = 64
SSD_HEADS = 32
SSD_STATE = 128
SSD_GROUPS = 4
SSD_CONV = 4
SSD_CHUNK = 128
SSD_XBC = SSD_INNER + 2 * SSD_GROUPS * SSD_STATE
GROUP_CH = SSD_INNER // SSD_GROUPS
HEADS_PER_GROUP = SSD_HEADS // SSD_GROUPS
DA_HEAD_DIM = 64
DA_HEADS = 8
DA_WIDTH = 1024
HEAD_W = 2 * DA_HEAD_DIM
ROT_DIM = DA_HEAD_DIM // 4
ROPE_THETA = 500000.0
Q_BLOCK = 128
PAGE_SIZE = 128
CC_CH = 1024
CC_KERNEL = 31
FF_HIDDEN = 2816
N_BRANCH = 3
IN_SIZES = (SSD_INNER, SSD_XBC, SSD_HEADS, DA_WIDTH, DA_WIDTH, DA_WIDTH, 2 * CC_CH, N_BRANCH * D_MODEL)
LN_EPS = 1e-5

VMEM_LIMIT_V7X = 52 * 1024 * 1024
LANES = 128
SAMPLE_BLOCK = 16
SAMPLE_ROWS = SAMPLE_BLOCK * 8


def _cparams(n_axes):
    return pltpu.CompilerParams(dimension_semantics=("arbitrary",) * n_axes,
                                vmem_limit_bytes=VMEM_LIMIT_V7X)


def _resident(shape):
    nd = len(shape)
    return pl.BlockSpec(shape, lambda *_: (0,) * nd, pipeline_mode=pl.Buffered(1))


def _dot(a, b):
    return jnp.dot(a, b, preferred_element_type=F32)


def _dot_nt(a, b):
    return lax.dot_general(a, b, (((1,), (1,)), ((), ())), preferred_element_type=F32)


def _dot_exact(a, b):
    return jnp.dot(a, b, preferred_element_type=F32, precision=lax.Precision.HIGHEST)


def _silu(x):
    return x * jax.nn.sigmoid(x)


def _layer_norm(x, g, b):
    xc = x - jnp.mean(x, axis=-1, keepdims=True)
    var = jnp.mean(xc * xc, axis=-1, keepdims=True)
    return xc * lax.rsqrt(var + LN_EPS) * g + b


FF_CHUNK = 256


def _ffn_kernel(x_ref, wgu_ref, wd_ref, g_ref, b_ref, o_ref, acc_ref, *, alpha):
    x = x_ref[...]
    xb = x.astype(BF16)
    for j in range(FF_HIDDEN // FF_CHUNK):
        lo = j * FF_CHUNK
        gate = _dot(xb, wgu_ref[:, lo:lo + FF_CHUNK])
        up = _dot(xb, wgu_ref[:, FF_HIDDEN + lo:FF_HIDDEN + lo + FF_CHUNK])
        act = (_silu(gate) * up).astype(BF16)
        down = _dot(act, wd_ref[lo:lo + FF_CHUNK, :])
        if j == 0:
            acc_ref[...] = down
        else:
            acc_ref[...] += down
    o_ref[...] = _layer_norm(alpha * x + 0.5 * acc_ref[...], g_ref[...], b_ref[...])


def _ffn_ln(x, wgu, wd, g, b, *, tm, alpha):
    m = x.shape[0]
    row = pl.BlockSpec((tm, D_MODEL), lambda i: (i, 0))
    return pl.pallas_call(
        functools.partial(_ffn_kernel, alpha=alpha),
        grid=(m // tm,),
        in_specs=[row, _resident(wgu.shape), _resident(wd.shape), _resident(g.shape), _resident(b.shape)],
        out_specs=row,
        out_shape=jax.ShapeDtypeStruct((m, D_MODEL), F32),
        scratch_shapes=[pltpu.VMEM((tm, D_MODEL), F32)],
        compiler_params=_cparams(1),
    )(x, wgu, wd, g, b)


def _softplus(x):
    return jnp.maximum(x, 0.0) + jnp.log1p(jnp.exp(-jnp.abs(x)))


def _rope_slice(x, cos_t, sin_up, sin_dn):
    return (x * cos_t + pltpu.roll(x, LANES - ROT_DIM // 2, axis=1) * sin_up
            + pltpu.roll(x, ROT_DIM // 2, axis=1) * sin_dn)


def _proj_ssd_kernel(x_ref, wz_ref, wxbc_ref, wdt_ref, dtb_ref, z_ref, xbc_ref, dt_ref):
    xb = x_ref[...].astype(BF16)
    cw = 512
    for c in range(SSD_INNER // cw):
        z_ref[:, c * cw:(c + 1) * cw] = _dot(xb, wz_ref[:, c * cw:(c + 1) * cw])
    for c in range(SSD_XBC // cw):
        xbc_ref[:, c * cw:(c + 1) * cw] = _dot(xb, wxbc_ref[:, c * cw:(c + 1) * cw])
    dt_ref[...] = _softplus(_dot(xb, wdt_ref[...]) + dtb_ref[...])


def _proj_ssd(x, wz, wxbc, wdt, dtb, *, tm):
    m = x.shape[0]

    def row(n):
        return pl.BlockSpec((tm, n), lambda i: (i, 0))

    return pl.pallas_call(
        _proj_ssd_kernel,
        grid=(m // tm,),
        in_specs=[row(D_MODEL), _resident(wz.shape), _resident(wxbc.shape), _resident(wdt.shape),
                  _resident(dtb.shape)],
        out_specs=[row(SSD_INNER), row(SSD_XBC), row(SSD_GROUPS * LANES)],
        out_shape=[jax.ShapeDtypeStruct((m, SSD_INNER), F32), jax.ShapeDtypeStruct((m, SSD_XBC), F32),
                   jax.ShapeDtypeStruct((m, SSD_GROUPS * LANES), F32)],
        compiler_params=_cparams(1),
    )(x, wz, wxbc, wdt, dtb)


def _proj_mix_kernel(x_ref, cos_ref, sup_ref, sdn_ref, wq_ref, wk_ref, wv_ref, wga_ref, wgb_ref,
                     q_ref, k_ref, v_ref, u_ref):
    xb = x_ref[...].astype(BF16)
    cos_t, sin_up, sin_dn = cos_ref[...], sup_ref[...], sdn_ref[...]
    cw = 256
    for c in range(DA_WIDTH // cw):
        lo = c * cw
        qc = _dot(xb, wq_ref[:, lo:lo + cw])
        kc = _dot(xb, wk_ref[:, lo:lo + cw])
        for s in range(cw // HEAD_W):
            sl = slice(s * HEAD_W, (s + 1) * HEAD_W)
            dst = slice(lo + s * HEAD_W, lo + (s + 1) * HEAD_W)
            q_ref[:, dst] = (_rope_slice(qc[:, sl], cos_t, sin_up, sin_dn)
                             * (DA_HEAD_DIM ** -0.5)).astype(q_ref.dtype)
            k_ref[:, dst] = _rope_slice(kc[:, sl], cos_t, sin_up, sin_dn)
        v_ref[:, lo:lo + cw] = _dot(xb, wv_ref[:, lo:lo + cw])
        ga = _dot(xb, wga_ref[:, lo:lo + cw])
        gb = _dot(xb, wgb_ref[:, lo:lo + cw])
        u_ref[:, lo:lo + cw] = ga * jax.nn.sigmoid(gb)


def _proj_mix(x, tables, wq, wk, wv, wga, wgb, *, tm, q_dtype):
    m = x.shape[0]
    n_tab = tables[0].shape[0] // tm
    row = pl.BlockSpec((tm, D_MODEL), lambda i: (i, 0))
    tab = pl.BlockSpec((tm, LANES), lambda i: (i % n_tab, 0))
    w = _resident(wq.shape)
    return pl.pallas_call(
        _proj_mix_kernel,
        grid=(m // tm,),
        in_specs=[row, tab, tab, tab, w, w, w, w, w],
        out_specs=[row, row, row, row],
        out_shape=[jax.ShapeDtypeStruct((m, D_MODEL), q_dtype)] + [jax.ShapeDtypeStruct((m, D_MODEL), F32)] * 3,
        compiler_params=_cparams(1),
    )(x, *tables, wq, wk, wv, wga, wgb)


def _merge_kernel(h_ref, ys_ref, yo_ref, yc_ref, wgate_ref, bgate_ref, wssd_ref, wda_ref, wcc_ref, wout_ref,
                  g_ref, b_ref, o_ref, *, alpha):
    h = h_ref[...]
    hb = h.astype(BF16)
    merged = None
    for i, (y_ref, w_ref) in enumerate(((ys_ref, wssd_ref), (yo_ref, wda_ref), (yc_ref, wcc_ref))):
        sl = slice(i * D_MODEL, (i + 1) * D_MODEL)
        gate = jax.nn.sigmoid(_dot(hb, wgate_ref[:, sl]) + bgate_ref[:, sl])
        term = gate * _dot(y_ref[...].astype(BF16), w_ref[...])
        merged = term if merged is None else merged + term
    mix = _dot(merged.astype(BF16), wout_ref[...])
    o_ref[...] = _layer_norm(alpha * h + mix, g_ref[...], b_ref[...])


def _merge(h, ys, yo, yc, wgate, bgate, wssd, wda, wcc, wout, g, b, *, tm, alpha):
    m = h.shape[0]

    def row(n):
        return pl.BlockSpec((tm, n), lambda i: (i, 0))

    consts = (wgate, bgate, wssd, wda, wcc, wout, g, b)
    return pl.pallas_call(
        functools.partial(_merge_kernel, alpha=alpha),
        grid=(m // tm,),
        in_specs=[row(D_MODEL), row(SSD_INNER), row(DA_WIDTH), row(CC_CH)] + [_resident(c.shape) for c in consts],
        out_specs=row(D_MODEL),
        out_shape=jax.ShapeDtypeStruct((m, D_MODEL), F32),
        compiler_params=_cparams(1),
    )(h, ys, yo, yc, *consts)


def _pair_cols(x, h0, h1, lane_lo):
    shape = (x.shape[0], LANES)
    return jnp.where(lane_lo[:x.shape[0]], jnp.broadcast_to(x[:, h0:h0 + 1], shape),
                     jnp.broadcast_to(x[:, h1:h1 + 1], shape))


def _ssd_intra(xs, bc, cc, dt, acol, mask, lane_lo):
    arow = acol.T
    bb = bc.astype(BF16)
    cb = _dot_nt(cc.astype(BF16), bb)
    y_in, xdts, aces = [], [], []
    for j in range(HEADS_PER_GROUP // 2):
        h0, h1 = 2 * j, 2 * j + 1
        gs = []
        for h in (h0, h1):
            delta = acol[:, h:h + 1] - arow[h:h + 1, :]
            gs.append((cb * jnp.exp(jnp.where(mask, delta, -jnp.inf))).astype(BF16))
        gp = jnp.concatenate(gs, axis=1)
        xdt = xs[:, j * LANES:(j + 1) * LANES] * _pair_cols(dt, h0, h1, lane_lo)
        zero = jnp.zeros_like(xdt)
        xbd = jnp.concatenate([jnp.where(lane_lo, xdt, zero), jnp.where(lane_lo, zero, xdt)], axis=0).astype(BF16)
        y_in.append(_dot(gp, xbd))
        xdts.append(xdt)
        aces.append(_pair_cols(acol, h0, h1, lane_lo))
    return y_in, xdts, aces, bb


def _ssd_finish(ys, xs, z, d_row, nw_row):
    gated, ss = [], None
    for j, y in enumerate(ys):
        sl = slice(j * LANES, (j + 1) * LANES)
        yz = (y + d_row[:, sl] * xs[:, sl]) * _silu(z[:, sl])
        gated.append(yz)
        part = jnp.sum(yz * yz, axis=1, keepdims=True)
        ss = part if ss is None else ss + part
    rs = lax.rsqrt(ss * (1.0 / GROUP_CH) + LN_EPS)
    return [(yz * rs * nw_row[:, j * LANES:(j + 1) * LANES]).astype(BF16) for j, yz in enumerate(gated)]


def _conv_silu(win_ref, lo, hi, w_ref, b_ref, *, rows, first):
    acc = b_ref[...]
    for k in range(SSD_CONV):
        acc = acc + w_ref[k:k + 1, :] * win_ref[first + k:first + k + rows, lo:hi]
    return _silu(acc)


def _ssd_prompt_kernel(xs_ref, bm_ref, cm_ref, z_ref, dt_ref, wx_ref, wb_ref, wc_ref, bx_ref, bb_ref, bc_ref,
                       a_ref, d_ref, nw_ref, y_ref, hfin_ref, win_ref, state_ref):
    L = SSD_CHUNK
    rows = lax.broadcasted_iota(jnp.int32, (L, LANES), 0)
    cols = lax.broadcasted_iota(jnp.int32, (L, LANES), 1)
    causal = rows >= cols
    tri = causal.astype(F32)
    lane_lo = cols < SSD_HEAD_DIM
    row_lo = rows < SSD_HEAD_DIM
    state_ref[...] = jnp.zeros_like(state_ref)
    srcs = ((xs_ref, 0, GROUP_CH), (bm_ref, GROUP_CH, GROUP_CH + SSD_STATE),
            (cm_ref, GROUP_CH + SSD_STATE, GROUP_CH + 2 * SSD_STATE))

    def chunk(r0, meta):
        for src, lo, hi in srcs:
            win_ref[8:8 + L, lo:hi] = src[pl.ds(r0, L), :]
            if meta:
                win_ref[0:8, lo:hi] = jnp.zeros((8, hi - lo), F32)
            else:
                win_ref[0:8, lo:hi] = src[pl.ds(r0 - 8, 8), :]
        first = 8 - (SSD_CONV - 1)
        xs = _conv_silu(win_ref, *srcs[0][1:], wx_ref, bx_ref, rows=L, first=first)
        bc = _conv_silu(win_ref, *srcs[1][1:], wb_ref, bb_ref, rows=L, first=first)
        cc = _conv_silu(win_ref, *srcs[2][1:], wc_ref, bc_ref, rows=L, first=first)
        dt = dt_ref[pl.ds(r0, L), :]
        if meta:
            dt = jnp.where(rows < N_META, dt, 0.0)
        acol = _dot_exact(tri, dt * a_ref[...])
        alast = acol[L - 1:L, :]
        y_in, xdts, aces, bb = _ssd_intra(xs, bc, cc, dt, acol, causal, lane_lo)
        cb16 = cc.astype(BF16)
        ys = []
        for j in range(HEADS_PER_GROUP // 2):
            h0, h1 = 2 * j, 2 * j + 1
            st = state_ref[j * LANES:(j + 1) * LANES, :]
            ace = aces[j]
            ys.append(y_in[j] + _dot_nt(cb16, st.astype(BF16)) * jnp.exp(ace))
            ale = jnp.where(lane_lo[0:1], jnp.broadcast_to(alast[:, h0:h0 + 1], (1, LANES)),
                            jnp.broadcast_to(alast[:, h1:h1 + 1], (1, LANES)))
            xw = xdts[j] * jnp.exp(ale - ace)
            upd = _dot(xw.T.astype(BF16), bb)
            dec = jnp.where(row_lo, jnp.broadcast_to(jnp.exp(alast[:, h0:h0 + 1]), (L, LANES)),
                            jnp.broadcast_to(jnp.exp(alast[:, h1:h1 + 1]), (L, LANES)))
            state_ref[j * LANES:(j + 1) * LANES, :] = st * dec + upd
        outs = _ssd_finish(ys, xs, z_ref[pl.ds(r0, L), :], d_ref[...], nw_ref[...])
        for j, o in enumerate(outs):
            y_ref[pl.ds(r0, L), j * LANES:(j + 1) * LANES] = o

    chunk(0, True)

    def body(c, carry):
        chunk(pl.multiple_of(N_META + c * L, 16), False)
        return carry

    lax.fori_loop(0, (y_ref.shape[0] - N_META) // L, body, 0)
    hfin_ref[...] = state_ref[...]


def _ssd_prompt(xbc, z, dt, conv_w, conv_b, a4, d_e, nw, *, batch, seq):
    g_b = SSD_INNER // SSD_STATE

    def seq_block(width, col0=0):
        return pl.BlockSpec((seq, width), lambda b, g: (b, g + col0))

    def par_block(rows, width, col0=0):
        return pl.BlockSpec((rows, width), lambda b, g: (0, g + col0))

    return pl.pallas_call(
        _ssd_prompt_kernel,
        grid=(batch, SSD_GROUPS),
        in_specs=[seq_block(GROUP_CH), seq_block(SSD_STATE, g_b), seq_block(SSD_STATE, g_b + SSD_GROUPS),
                  seq_block(GROUP_CH), seq_block(LANES),
                  par_block(SSD_CONV, GROUP_CH), par_block(SSD_CONV, SSD_STATE, g_b),
                  par_block(SSD_CONV, SSD_STATE, g_b + SSD_GROUPS),
                  par_block(1, GROUP_CH), par_block(1, SSD_STATE, g_b), par_block(1, SSD_STATE, g_b + SSD_GROUPS),
                  par_block(1, LANES), par_block(1, GROUP_CH), par_block(1, GROUP_CH)],
        out_specs=[seq_block(GROUP_CH), pl.BlockSpec((None, GROUP_CH, SSD_STATE), lambda b, g: (b, g, 0))],
        out_shape=[jax.ShapeDtypeStruct((batch * seq, SSD_INNER), BF16),
                   jax.ShapeDtypeStruct((batch, SSD_INNER, SSD_STATE), F32)],
        scratch_shapes=[pltpu.VMEM((8 + SSD_CHUNK, GROUP_CH + 2 * SSD_STATE), F32),
                        pltpu.VMEM((GROUP_CH, SSD_STATE), F32)],
        compiler_params=_cparams(2),
    )(xbc, xbc, xbc, z, dt, conv_w, conv_w, conv_w, conv_b, conv_b, conv_b, a4, d_e, nw)


def _ssd_sample_kernel(xs_ref, bm_ref, cm_ref, px_ref, pb_ref, pc_ref, z_ref, dt_ref, h0_ref,
                       wx_ref, wb_ref, wc_ref, bx_ref, bb_ref, bc_ref, a_ref, d_ref, nw_ref,
                       y_ref, h1_ref, ext_ref, flat_ref, xwt_ref, yst_ref):
    R = SAMPLE_ROWS
    rows = lax.broadcasted_iota(jnp.int32, (R, LANES), 0)
    cols = lax.broadcasted_iota(jnp.int32, (R, LANES), 1)
    same = (rows >> 3) == (cols >> 3)
    causal = same & (rows >= cols)
    lane_lo = cols < SSD_HEAD_DIM
    srcs = ((xs_ref, px_ref, 0, GROUP_CH), (bm_ref, pb_ref, GROUP_CH, GROUP_CH + SSD_STATE),
            (cm_ref, pc_ref, GROUP_CH + SSD_STATE, GROUP_CH + 2 * SSD_STATE))
    for cur, prev, lo, hi in srcs:
        ext_ref[:, 0:8, lo:hi] = prev[...]
        ext_ref[:, 8:16, lo:hi] = cur[...]
    first = 8 - (SSD_CONV - 1)
    for k in range(SSD_CONV):
        flat_ref[k] = ext_ref[:, first + k:first + k + 8, :].reshape(R, GROUP_CH + 2 * SSD_STATE)

    def conv(lo, hi, w_ref, b_ref):
        acc = b_ref[...]
        for k in range(SSD_CONV):
            acc = acc + w_ref[k:k + 1, :] * flat_ref[k, :, lo:hi]
        return _silu(acc)

    xs = conv(0, GROUP_CH, wx_ref, bx_ref)
    bc = conv(GROUP_CH, GROUP_CH + SSD_STATE, wb_ref, bb_ref)
    cc = conv(GROUP_CH + SSD_STATE, GROUP_CH + 2 * SSD_STATE, wc_ref, bc_ref)
    dt = dt_ref[...]
    dta = dt * a_ref[...]
    acol = _dot_exact(causal.astype(F32), dta)
    total = _dot_exact(same.astype(F32), dta)
    y_in, xdts, aces, _ = _ssd_intra(xs, bc, cc, dt, acol, causal, lane_lo)
    for j in range(HEADS_PER_GROUP // 2):
        xw = xdts[j] * jnp.exp(_pair_cols(total, 2 * j, 2 * j + 1, lane_lo) - aces[j])
        xwt_ref[j * LANES:(j + 1) * LANES, :] = xw.T.astype(BF16)
    xwt = xwt_ref[...]
    for i in range(SAMPLE_BLOCK):
        r = slice(8 * i, 8 * i + 8)
        st = h0_ref[i]
        c16 = jnp.concatenate([cc[r], jnp.zeros((8, SSD_STATE), F32)], axis=0).astype(BF16)
        yst_ref[r, :] = _dot_nt(c16, st.astype(BF16))[0:8]
        b_i = jnp.where((rows >> 3) == i, bc, 0.0).astype(BF16)
        upd = _dot(xwt, b_i)
        tot_i = total[8 * i:8 * i + 1, :]
        for h in range(HEADS_PER_GROUP):
            hs = slice(h * SSD_HEAD_DIM, (h + 1) * SSD_HEAD_DIM)
            h1_ref[i, hs, :] = st[hs] * jnp.exp(tot_i[:, h:h + 1]) + upd[hs]
    ys = [y_in[j] + yst_ref[:, j * LANES:(j + 1) * LANES] * jnp.exp(aces[j])
          for j in range(HEADS_PER_GROUP // 2)]
    outs = _ssd_finish(ys, xs, z_ref[...], d_ref[...], nw_ref[...])
    for j, o in enumerate(outs):
        y_ref[:, j * LANES:(j + 1) * LANES] = o


def _ssd_sample(xbc3, prev3, z, dt, h0, conv_w, conv_b, a4, d_e, nw):
    n_req = xbc3.shape[0]
    g_b = SSD_INNER // SSD_STATE
    R = SAMPLE_ROWS
    width = GROUP_CH + 2 * SSD_STATE

    def tok_block(w, col0=0):
        return pl.BlockSpec((SAMPLE_BLOCK, 8, w), lambda i, g: (i, 0, g + col0))

    def row_block(w):
        return pl.BlockSpec((R, w), lambda i, g: (i, g))

    def par_block(rows, w, col0=0):
        return pl.BlockSpec((rows, w), lambda i, g: (0, g + col0))

    state = pl.BlockSpec((SAMPLE_BLOCK, GROUP_CH, SSD_STATE), lambda i, g: (i, g, 0))
    toks = [tok_block(GROUP_CH), tok_block(SSD_STATE, g_b), tok_block(SSD_STATE, g_b + SSD_GROUPS)]
    return pl.pallas_call(
        _ssd_sample_kernel,
        grid=(n_req // SAMPLE_BLOCK, SSD_GROUPS),
        in_specs=toks + toks + [row_block(GROUP_CH), row_block(LANES), state,
                                par_block(SSD_CONV, GROUP_CH), par_block(SSD_CONV, SSD_STATE, g_b),
                                par_block(SSD_CONV, SSD_STATE, g_b + SSD_GROUPS),
                                par_block(1, GROUP_CH), par_block(1, SSD_STATE, g_b),
                                par_block(1, SSD_STATE, g_b + SSD_GROUPS),
                                par_block(1, LANES), par_block(1, GROUP_CH), par_block(1, GROUP_CH)],
        out_specs=[row_block(GROUP_CH), state],
        out_shape=[jax.ShapeDtypeStruct((n_req * 8, SSD_INNER), BF16),
                   jax.ShapeDtypeStruct((n_req, SSD_INNER, SSD_STATE), F32)],
        scratch_shapes=[pltpu.VMEM((SAMPLE_BLOCK, 16, width), F32), pltpu.VMEM((SSD_CONV, R, width), F32),
                        pltpu.VMEM((GROUP_CH, R), BF16), pltpu.VMEM((R, GROUP_CH), F32)],
        compiler_params=_cparams(2),
    )(xbc3, xbc3, xbc3, prev3, prev3, prev3, z, dt, h0, conv_w, conv_w, conv_w, conv_b, conv_b, conv_b,
      a4, d_e, nw)


def _sub_ln(o, w_row, post_scale):
    return o * lax.rsqrt(jnp.mean(o * o, axis=-1, keepdims=True) + LN_EPS) * w_row * post_scale


def _split_components(q):
    lane_lo = lax.broadcasted_iota(jnp.int32, q.shape, 1) < DA_HEAD_DIM
    zero = jnp.zeros_like(q)
    return jnp.concatenate([jnp.where(lane_lo, q, zero), jnp.where(lane_lo, zero, q)], axis=0)


def _attn_prompt_kernel(lam_ref, q_ref, k_ref, v_ref, w_ref, o_ref, kb_ref, vb_ref, *, post_scale):
    lam = lam_ref[0]
    kb_ref[...] = k_ref[...].astype(BF16)
    vb_ref[...] = v_ref[...].astype(BF16)
    k_meta = kb_ref[0:Q_BLOCK]
    v_meta = vb_ref[0:Q_BLOCK]
    w_row = w_ref[...]

    def attend(r0, n_q, parts):
        m = None
        for s, _ in parts:
            pm = jnp.max(s, axis=1, keepdims=True)
            m = pm if m is None else jnp.maximum(m, pm)
        ps = [jnp.exp(s - m) for s, _ in parts]
        den = None
        for p in ps:
            pd = jnp.sum(p, axis=1, keepdims=True)
            den = pd if den is None else den + pd
        inv = 1.0 / den
        o = None
        for p, (_, vals) in zip(ps, parts):
            pn = p * inv
            att = (pn[0:n_q] - lam * pn[n_q:2 * n_q]).astype(BF16)
            term = _dot(att, vals)
            o = term if o is None else o + term
        o_ref[r0:r0 + n_q, :] = _sub_ln(o, w_row, post_scale).astype(o_ref.dtype)

    q2 = _split_components(q_ref[0:N_META])
    s = _dot_nt(q2, k_meta)
    qpos = lax.broadcasted_iota(jnp.int32, s.shape, 0) & (N_META - 1)
    kpos = lax.broadcasted_iota(jnp.int32, s.shape, 1)
    attend(0, N_META, [(jnp.where(kpos <= qpos, s, -jnp.inf), v_meta)])

    n_blk = (q_ref.shape[0] - N_META) // Q_BLOCK
    for i in range(n_blk):
        r0 = N_META + i * Q_BLOCK
        tk = (i + 1) * Q_BLOCK
        q2 = _split_components(q_ref[r0:r0 + Q_BLOCK])
        s_meta = _dot_nt(q2, k_meta)
        s_meta = jnp.where(lax.broadcasted_iota(jnp.int32, s_meta.shape, 1) < N_META, s_meta, -jnp.inf)
        s_real = _dot_nt(q2, kb_ref[N_META:N_META + tk])
        qpos = (lax.broadcasted_iota(jnp.int32, s_real.shape, 0) & (Q_BLOCK - 1)) + i * Q_BLOCK
        kpos = lax.broadcasted_iota(jnp.int32, s_real.shape, 1)
        s_real = jnp.where(kpos <= qpos, s_real, -jnp.inf)
        attend(r0, Q_BLOCK, [(s_meta, v_meta), (s_real, vb_ref[N_META:N_META + tk])])


def _attn_prompt(lam, q, k, v, w_row, *, batch, seq, post_scale):
    blk = pl.BlockSpec((seq, HEAD_W), lambda b, h: (b, h))
    return pl.pallas_call(
        functools.partial(_attn_prompt_kernel, post_scale=post_scale),
        grid=(batch, DA_HEADS),
        in_specs=[pl.BlockSpec(memory_space=pltpu.SMEM), blk, blk, blk, _resident(w_row.shape)],
        out_specs=blk,
        out_shape=jax.ShapeDtypeStruct((batch * seq, DA_WIDTH), BF16),
        scratch_shapes=[pltpu.VMEM((seq, HEAD_W), BF16), pltpu.VMEM((seq, HEAD_W), BF16)],
        compiler_params=_cparams(2),
    )(lam, q, k, v, w_row)


def _attn_sample_kernel(pt_ref, lam_ref, q_ref, kn_ref, vn_ref, w_ref, *refs, n_pages, post_scale):
    del pt_ref
    k_pages = refs[:n_pages]
    v_pages = refs[n_pages:2 * n_pages]
    o_ref, s_ref = refs[2 * n_pages], refs[2 * n_pages + 1]
    lam = lam_ref[0]
    page_rows = PAGE_SIZE * DA_HEADS
    new_rows = 8 * DA_HEADS
    q = q_ref[...]
    lane = lax.broadcasted_iota(jnp.int32, (8, HEAD_W), 1)
    pieces = []
    for c in range(2):
        for h in range(DA_HEADS):
            qh = q[:, h * HEAD_W:(h + 1) * HEAD_W]
            keep = lane < DA_HEAD_DIM if c == 0 else lane >= DA_HEAD_DIM
            pieces.append(jnp.where(keep, qh, 0.0))
    qmat = jnp.concatenate(pieces, axis=0).astype(BF16)
    row_head = (lax.broadcasted_iota(jnp.int32, (LANES, page_rows), 0) >> 3) & (DA_HEADS - 1)
    col_head = lax.broadcasted_iota(jnp.int32, (LANES, page_rows), 1) & (DA_HEADS - 1)
    head_ok = row_head == col_head
    s_new = _dot_nt(qmat, kn_ref[...].astype(BF16))
    r_i = lax.broadcasted_iota(jnp.int32, s_new.shape, 0)
    c_i = lax.broadcasted_iota(jnp.int32, s_new.shape, 1)
    ok_new = (((r_i >> 3) & (DA_HEADS - 1)) == (c_i & (DA_HEADS - 1))) & ((c_i >> 3) <= (r_i & 7))
    s_new = jnp.where(ok_new, s_new, -jnp.inf)
    m = jnp.max(s_new, axis=1, keepdims=True)
    for p in range(n_pages):
        s = _dot_nt(qmat, k_pages[p][...].astype(BF16))
        s = jnp.where(head_ok, s, -jnp.inf)
        s_ref[:, p * page_rows:(p + 1) * page_rows] = s
        m = jnp.maximum(m, jnp.max(s, axis=1, keepdims=True))
    p_new = jnp.exp(s_new - m)
    den = jnp.sum(p_new, axis=1, keepdims=True)
    for p in range(n_pages):
        sl = slice(p * page_rows, (p + 1) * page_rows)
        e = jnp.exp(s_ref[:, sl] - m)
        s_ref[:, sl] = e
        den = den + jnp.sum(e, axis=1, keepdims=True)
    inv = 1.0 / den
    half = LANES // 2

    def combine(pr):
        pn = pr * inv
        return (pn[0:half] - lam * pn[half:LANES]).astype(BF16)

    acc = _dot(combine(p_new), vn_ref[...].astype(BF16))
    for p in range(n_pages):
        acc = acc + _dot(combine(s_ref[:, p * page_rows:(p + 1) * page_rows]), v_pages[p][...].astype(BF16))
    w_row = w_ref[...]
    for h in range(DA_HEADS):
        o_ref[:, h * HEAD_W:(h + 1) * HEAD_W] = _sub_ln(acc[h * 8:(h + 1) * 8], w_row, post_scale)


def _attn_sample(page_table_flat, lam, q, k_new, v_new, w_row, cache_k, cache_v, *, layer, post_scale):
    n_req = q.shape[0] // 8
    n_pages = page_table_flat.shape[0] // n_req
    page_rows = PAGE_SIZE * DA_HEADS

    def page_spec(j):
        return pl.BlockSpec((None, None, page_rows, HEAD_W), lambda b, pt: (layer, pt[b * n_pages + j], 0, 0))

    tok = pl.BlockSpec((8, DA_WIDTH), lambda b, pt: (b, 0))
    new = pl.BlockSpec((None, 8 * DA_HEADS, HEAD_W), lambda b, pt: (b, 0, 0))
    pages = [page_spec(j) for j in range(n_pages)]
    grid_spec = pltpu.PrefetchScalarGridSpec(
        num_scalar_prefetch=1,
        grid=(n_req,),
        in_specs=[pl.BlockSpec(memory_space=pltpu.SMEM), tok, new, new,
                  pl.BlockSpec(w_row.shape, lambda b, pt: (0, 0))] + pages + pages,
        out_specs=tok,
        scratch_shapes=[pltpu.VMEM((LANES, n_pages * page_rows), F32)],
    )
    return pl.pallas_call(
        functools.partial(_attn_sample_kernel, n_pages=n_pages, post_scale=post_scale),
        grid_spec=grid_spec,
        out_shape=jax.ShapeDtypeStruct((n_req * 8, DA_WIDTH), F32),
        compiler_params=_cparams(1),
    )(page_table_flat, lam, q, k_new, v_new, w_row, *([cache_k] * n_pages), *([cache_v] * n_pages))


CC_HIST = 32
CC_ROWS = 48


def _cc_prompt_kernel(u_ref, w_ref, b_ref, g_ref, beta_ref, o_ref, win_ref):
    first = CC_HIST - (CC_KERNEL - 1)

    def body(c, carry):
        r0 = pl.multiple_of(c * CC_ROWS, 16)
        win_ref[CC_HIST:CC_HIST + CC_ROWS, :] = u_ref[pl.ds(r0, CC_ROWS), :]

        @pl.when(c == 0)
        def _():
            win_ref[0:CC_HIST, :] = jnp.zeros((CC_HIST, CC_CH), F32)

        @pl.when(c > 0)
        def _():
            win_ref[0:CC_HIST, :] = u_ref[pl.ds(pl.multiple_of(jnp.maximum(r0 - CC_HIST, 0), 16), CC_HIST), :]

        acc = b_ref[...]
        for k in range(CC_KERNEL):
            acc = acc + w_ref[k:k + 1, :] * win_ref[first + k:first + k + CC_ROWS, :]
        o_ref[pl.ds(r0, CC_ROWS), :] = _silu(_layer_norm(acc, g_ref[...], beta_ref[...])).astype(o_ref.dtype)
        return carry

    lax.fori_loop(0, u_ref.shape[0] // CC_ROWS, body, 0)


def _cc_prompt(u, w, b, g, beta, *, batch, seq):
    blk = pl.BlockSpec((seq, CC_CH), lambda i: (i, 0))
    return pl.pallas_call(
        _cc_prompt_kernel,
        grid=(batch,),
        in_specs=[blk] + [_resident(c.shape) for c in (w, b, g, beta)],
        out_specs=blk,
        out_shape=jax.ShapeDtypeStruct((batch * seq, CC_CH), BF16),
        scratch_shapes=[pltpu.VMEM((CC_HIST + CC_ROWS, CC_CH), F32)],
        compiler_params=_cparams(1),
    )(u, w, b, g, beta)


def _cc_sample_kernel(u_ref, prev_ref, w_ref, b_ref, g_ref, beta_ref, o_ref, ext_ref):
    ext_ref[:, 0:CC_HIST, :] = prev_ref[...]
    ext_ref[:, CC_HIST:CC_HIST + 8, :] = u_ref[...]
    first = CC_HIST - (CC_KERNEL - 1)
    acc = b_ref[...]
    for k in range(CC_KERNEL):
        acc = acc + w_ref[k:k + 1, :] * ext_ref[:, first + k:first + k + 8, :].reshape(SAMPLE_ROWS, CC_CH)
    o_ref[...] = _silu(_layer_norm(acc, g_ref[...], beta_ref[...])).astype(o_ref.dtype)


def _cc_sample(u3, prev3, w, b, g, beta):
    n_req = u3.shape[0]
    return pl.pallas_call(
        _cc_sample_kernel,
        grid=(n_req // SAMPLE_BLOCK,),
        in_specs=[pl.BlockSpec((SAMPLE_BLOCK, 8, CC_CH), lambda i: (i, 0, 0)),
                  pl.BlockSpec((SAMPLE_BLOCK, CC_HIST, CC_CH), lambda i: (i, 0, 0))]
        + [_resident(c.shape) for c in (w, b, g, beta)],
        out_specs=pl.BlockSpec((SAMPLE_ROWS, CC_CH), lambda i: (i, 0)),
        out_shape=jax.ShapeDtypeStruct((n_req * 8, CC_CH), BF16),
        scratch_shapes=[pltpu.VMEM((SAMPLE_BLOCK, CC_HIST + 8, CC_CH), F32)],
        compiler_params=_cparams(1),
    )(u3, prev3, w, b, g, beta)


def _rope_tables(pos):
    half = ROT_DIM // 2
    inv_freq = ROPE_THETA ** (-jnp.arange(half, dtype=F32) * 2.0 / ROT_DIM)
    ang = pos.astype(F32)[:, None] * inv_freq[None, :]
    cos, sin = jnp.cos(ang), jnp.sin(ang)
    n = pos.shape[0]
    pad = jnp.zeros((n, DA_HEAD_DIM - ROT_DIM), F32)
    zeros = jnp.zeros((n, half), F32)
    cos_c = jnp.concatenate([cos, cos, pad + 1.0], axis=1)
    up_c = jnp.concatenate([-sin, zeros, pad], axis=1)
    dn_c = jnp.concatenate([zeros, sin, pad], axis=1)
    return tuple(jnp.concatenate([t, t], axis=1) for t in (cos_c, up_c, dn_c))


def _head_lanes(vec):
    v = vec.astype(F32).reshape(SSD_GROUPS, HEADS_PER_GROUP)
    return jnp.pad(v, ((0, 0), (0, LANES - HEADS_PER_GROUP))).reshape(1, SSD_GROUPS * LANES)


def _prepare_layer(l, p):
    pts = [0]
    for s in IN_SIZES:
        pts.append(pts[-1] + s)
    w_in = p['w_in'][l]
    cols = [w_in[:, pts[i]:pts[i + 1]] for i in range(len(IN_SIZES))]
    w_dt = cols[2].reshape(D_MODEL, SSD_GROUPS, HEADS_PER_GROUP)
    w_dt = jnp.pad(w_dt, ((0, 0), (0, 0), (0, LANES - HEADS_PER_GROUP))).reshape(D_MODEL, SSD_GROUPS * LANES)
    row = lambda v: v.astype(F32).reshape(1, -1)
    lam_init = 0.8 - 0.6 * math.exp(-0.3 * l)
    lam = (jnp.exp(jnp.sum(p['da_lambda_q1'][l].astype(F32) * p['da_lambda_k1'][l].astype(F32)))
           - jnp.exp(jnp.sum(p['da_lambda_q2'][l].astype(F32) * p['da_lambda_k2'][l].astype(F32))) + lam_init)
    return dict(
        lam_init=lam_init, lam=lam.reshape(1).astype(F32),
        ln_g=[row(p['ln_g'][l, i]) for i in range(3)], ln_b=[row(p['ln_b'][l, i]) for i in range(3)],
        ff1_gu=p['w_ff1_gu'][l].astype(BF16), ff1_d=p['w_ff1_down'][l].astype(BF16),
        ff2_gu=p['w_ff2_gu'][l].astype(BF16), ff2_d=p['w_ff2_down'][l].astype(BF16),
        w_z=cols[0].astype(BF16), w_xbc=cols[1].astype(BF16), w_dt=w_dt.astype(BF16),
        w_q=cols[3].astype(BF16), w_k=cols[4].astype(BF16), w_v=cols[5].astype(BF16),
        w_ga=cols[6][:, :CC_CH].astype(BF16), w_gb=cols[6][:, CC_CH:].astype(BF16),
        w_gate=cols[7].astype(BF16), b_gate=row(p['b_gate'][l]),
        dt_bias=_head_lanes(p['ssd_dt_bias'][l]), a4=_head_lanes(-jnp.exp(p['ssd_a_log'][l].astype(F32))),
        conv_w=p['ssd_conv_w'][l].astype(F32), conv_b=row(p['ssd_conv_b'][l]),
        d_e=jnp.repeat(p['ssd_d'][l].astype(F32), SSD_HEAD_DIM).reshape(1, SSD_INNER),
        norm_w=row(p['ssd_norm_w'][l]), ssd_out=p['ssd_out'][l].astype(BF16),
        subln_w=row(p['da_subln_w'][l]), da_out=p['da_out'][l].astype(BF16),
        cc_w=jnp.pad(p['cc_conv_w'][l].astype(F32), ((0, 1), (0, 0))), cc_b=row(p['cc_conv_b'][l]),
        cc_g=row(p['cc_ln_g'][l]), cc_beta=row(p['cc_ln_b'][l]), cc_out=p['cc_out'][l].astype(BF16),
        w_out=p['w_out'][l].astype(BF16),
    )


def _mixer_tail(h1, ys, yo, yc, lp, alpha, tm_merge, tm_ffn):
    h2 = _merge(h1, ys, yo, yc, lp['w_gate'], lp['b_gate'], lp['ssd_out'], lp['da_out'], lp['cc_out'], lp['w_out'],
                lp['ln_g'][1], lp['ln_b'][1], tm=tm_merge, alpha=alpha)
    return _ffn_ln(h2, lp['ff2_gu'], lp['ff2_d'], lp['ln_g'][2], lp['ln_b'][2], tm=tm_ffn, alpha=alpha)


def kernel(x_prompt, x_sample, cache_k, cache_v, state_ssm, state_ssd_conv, state_conf_conv, page_table,
           meta_tokens, ln_g, ln_b, w_ff1_gu, w_ff1_down, w_ff2_gu, w_ff2_down, w_in, b_gate,
           ssd_conv_w, ssd_conv_b, ssd_dt_bias, ssd_a_log, ssd_d, ssd_norm_w, ssd_out,
           da_lambda_q1, da_lambda_k1, da_lambda_q2, da_lambda_k2, da_subln_w, da_out,
           cc_conv_w, cc_conv_b, cc_ln_g, cc_ln_b, cc_out, w_out):
    params = dict(ln_g=ln_g, ln_b=ln_b, w_ff1_gu=w_ff1_gu, w_ff1_down=w_ff1_down, w_ff2_gu=w_ff2_gu,
                  w_ff2_down=w_ff2_down, w_in=w_in, b_gate=b_gate, ssd_conv_w=ssd_conv_w, ssd_conv_b=ssd_conv_b,
                  ssd_dt_bias=ssd_dt_bias, ssd_a_log=ssd_a_log, ssd_d=ssd_d, ssd_norm_w=ssd_norm_w, ssd_out=ssd_out,
                  da_lambda_q1=da_lambda_q1, da_lambda_k1=da_lambda_k1, da_lambda_q2=da_lambda_q2,
                  da_lambda_k2=da_lambda_k2, da_subln_w=da_subln_w, da_out=da_out, cc_conv_w=cc_conv_w,
                  cc_conv_b=cc_conv_b, cc_ln_g=cc_ln_g, cc_ln_b=cc_ln_b, cc_out=cc_out, w_out=w_out)
    depth = w_in.shape[0]
    alpha = (2.0 * depth) ** 0.25
    bp, seq = x_prompt.shape[0], x_prompt.shape[1]
    lp_len = seq + N_META
    ds, dec_len = x_sample.shape[0], x_sample.shape[1]
    assert dec_len == 8 and lp_len % 48 == 0 and ds % SAMPLE_BLOCK == 0
    n_pages = page_table.shape[1]
    past = n_pages * cache_k.shape[2]
    n_pool = cache_k.shape[1]

    tm_p = lp_len // 3
    tm_pm = 384
    tm_s = 512
    assert (bp * lp_len) % tm_pm == 0 and (ds * dec_len) % tm_s == 0

    hp = jnp.concatenate([jnp.broadcast_to(meta_tokens.astype(F32)[None], (bp, N_META, D_MODEL)),
                          x_prompt.astype(F32)], axis=1).reshape(bp * lp_len, D_MODEL)
    hs = x_sample.astype(F32).reshape(ds * dec_len, D_MODEL)
    tab_p = _rope_tables(jnp.arange(lp_len))
    tab_s = tuple(jnp.tile(t, (tm_s // dec_len, 1)) for t in _rope_tables(past + jnp.arange(dec_len)))
    ck = cache_k.reshape(depth, n_pool, PAGE_SIZE * DA_HEADS, HEAD_W)
    cv = cache_v.reshape(depth, n_pool, PAGE_SIZE * DA_HEADS, HEAD_W)
    pt_flat = page_table.reshape(-1).astype(jnp.int32)

    outs = {k: [] for k in ('kp', 'vp', 'ks', 'vs', 'hp', 'hs', 'cp', 'cs', 'up', 'us')}
    for l in range(depth):
        lp = _prepare_layer(l, params)
        post = 1.0 - lp['lam_init']

        h1 = _ffn_ln(hp, lp['ff1_gu'], lp['ff1_d'], lp['ln_g'][0], lp['ln_b'][0], tm=tm_p, alpha=alpha)
        z, xbc, dt = _proj_ssd(h1, lp['w_z'], lp['w_xbc'], lp['w_dt'], lp['dt_bias'], tm=tm_p)
        q, k, v, u = _proj_mix(h1, tab_p, lp['w_q'], lp['w_k'], lp['w_v'], lp['w_ga'], lp['w_gb'],
                               tm=tm_p, q_dtype=BF16)
        ys, hfin = _ssd_prompt(xbc, z, dt, lp['conv_w'], lp['conv_b'], lp['a4'], lp['d_e'], lp['norm_w'],
                               batch=bp, seq=lp_len)
        yo = _attn_prompt(lp['lam'], q, k, v, lp['subln_w'], batch=bp, seq=lp_len, post_scale=post)
        yc = _cc_prompt(u, lp['cc_w'], lp['cc_b'], lp['cc_g'], lp['cc_beta'], batch=bp, seq=lp_len)
        hp = _mixer_tail(h1, ys, yo, yc, lp, alpha, tm_pm, tm_p)
        outs['kp'].append(k.reshape(bp, lp_len, DA_HEADS, HEAD_W))
        outs['vp'].append(v.reshape(bp, lp_len, DA_HEADS, HEAD_W))
        outs['hp'].append(hfin.reshape(bp, SSD_HEADS, SSD_HEAD_DIM, SSD_STATE))
        outs['cp'].append(xbc.reshape(bp, lp_len, SSD_XBC)[:, lp_len - (SSD_CONV - 1):])
        outs['up'].append(u.reshape(bp, lp_len, CC_CH)[:, lp_len - (CC_KERNEL - 1):])

        h1 = _ffn_ln(hs, lp['ff1_gu'], lp['ff1_d'], lp['ln_g'][0], lp['ln_b'][0], tm=tm_s, alpha=alpha)
        z, xbc, dt = _proj_ssd(h1, lp['w_z'], lp['w_xbc'], lp['w_dt'], lp['dt_bias'], tm=tm_s)
        q, k, v, u = _proj_mix(h1, tab_s, lp['w_q'], lp['w_k'], lp['w_v'], lp['w_ga'], lp['w_gb'],
                               tm=tm_s, q_dtype=F32)
        conv_prev = state_ssd_conv[l].astype(F32)
        prev3 = jnp.pad(conv_prev, ((0, 0), (8 - (SSD_CONV - 1), 0), (0, 0)))
        ys, h1s = _ssd_sample(xbc.reshape(ds, dec_len, SSD_XBC), prev3, z, dt,
                              state_ssm[l].astype(F32).reshape(ds, SSD_INNER, SSD_STATE),
                              lp['conv_w'], lp['conv_b'], lp['a4'], lp['d_e'], lp['norm_w'])
        yo = _attn_sample(pt_flat, lp['lam'], q, k.reshape(ds, dec_len * DA_HEADS, HEAD_W),
                          v.reshape(ds, dec_len * DA_HEADS, HEAD_W), lp['subln_w'], ck, cv,
                          layer=l, post_scale=post)
        cc_prev = state_conf_conv[l].astype(F32)
        u3 = u.reshape(ds, dec_len, CC_CH)
        yc = _cc_sample(u3, jnp.pad(cc_prev, ((0, 0), (CC_HIST - (CC_KERNEL - 1), 0), (0, 0))),
                        lp['cc_w'], lp['cc_b'], lp['cc_g'], lp['cc_beta'])
        hs = _mixer_tail(h1, ys, yo, yc, lp, alpha, tm_s, tm_s)
        outs['ks'].append(k.reshape(ds, dec_len, DA_HEADS, HEAD_W))
        outs['vs'].append(v.reshape(ds, dec_len, DA_HEADS, HEAD_W))
        outs['hs'].append(h1s.reshape(ds, SSD_HEADS, SSD_HEAD_DIM, SSD_STATE))
        xbc3 = xbc.reshape(ds, dec_len, SSD_XBC)
        outs['cs'].append(jnp.concatenate([conv_prev, xbc3], axis=1)[:, -(SSD_CONV - 1):])
        outs['us'].append(jnp.concatenate([cc_prev, u3], axis=1)[:, -(CC_KERNEL - 1):])

    y_prompt = hp.reshape(bp, lp_len, D_MODEL)[:, N_META:]
    y_sample = hs.reshape(ds, dec_len, D_MODEL)
    st = {k: jnp.stack(v) for k, v in outs.items()}
    return (y_prompt, y_sample, st['kp'], st['vp'], st['ks'], st['vs'], st['hp'], st['hs'],
            st['cp'], st['cs'], st['up'], st['us'])
```

```python
import functools
import math

import jax
import jax.numpy as jnp
from jax import lax
from jax.experimental import pallas as pl
from jax.experimental.pallas import tpu as pltpu

F32 = jnp.float32
BF16 = jnp.bfloat16

D_MODEL = 1024
N_META = 16
SSD_INNER = 2048
SSD_HEAD_DIM = 64
SSD_HEADS = 32
SSD_STATE = 128
SSD_GROUPS = 4
SSD_CONV = 4
SSD_CHUNK = 128
SSD_XBC = SSD_INNER + 2 * SSD_GROUPS * SSD_STATE
GROUP_CH = SSD_INNER // SSD_GROUPS
HEADS_PER_GROUP = SSD_HEADS // SSD_GROUPS
DA_HEAD_DIM = 64
DA_HEADS = 8
DA_WIDTH = 1024
HEAD_W = 2 * DA_HEAD_DIM
ROT_DIM = DA_HEAD_DIM // 4
ROPE_THETA = 500000.0
Q_BLOCK = 256
PAGE_SIZE = 128
CC_CH = 1024
CC_KERNEL = 31
FF_HIDDEN = 2816
N_BRANCH = 3
IN_SIZES = (SSD_INNER, SSD_XBC, SSD_HEADS, DA_WIDTH, DA_WIDTH, DA_WIDTH, 2 * CC_CH, N_BRANCH * D_MODEL)
LN_EPS = 1e-5

VMEM_LIMIT_V7X = 52 * 1024 * 1024
LANES = 128
SAMPLE_BLOCK = 16
SAMPLE_ROWS = SAMPLE_BLOCK * 8


def _cparams(n_axes):
    return pltpu.CompilerParams(dimension_semantics=("arbitrary",) * n_axes,
                                vmem_limit_bytes=VMEM_LIMIT_V7X)


def _resident(shape):
    nd = len(shape)
    return pl.BlockSpec(shape, lambda *_: (0,) * nd, pipeline_mode=pl.Buffered(1))


def _dot(a, b):
    return jnp.dot(a, b, preferred_element_type=F32)


def _dot_nt(a, b):
    return lax.dot_general(a, b, (((1,), (1,)), ((), ())), preferred_element_type=F32)


def _dot_exact(a, b):
    return jnp.dot(a, b, preferred_element_type=F32, precision=lax.Precision.HIGHEST)


def _sigmoid(x):
    return 0.5 * jnp.tanh(0.5 * x) + 0.5


def _silu(x):
    return x * _sigmoid(x)


def _layer_norm(x, g, b):
    xc = x - jnp.mean(x, axis=-1, keepdims=True)
    var = jnp.mean(xc * xc, axis=-1, keepdims=True)
    return xc * lax.rsqrt(var + LN_EPS) * g + b


FF_CHUNK = 256


def _ffn_kernel(x_ref, wgu_ref, wd_ref, g_ref, b_ref, o_ref, acc_ref, *, alpha):
    x = x_ref[...]
    xb = x.astype(BF16)
    for j in range(FF_HIDDEN // FF_CHUNK):
        lo = j * FF_CHUNK
        gate = _dot(xb, wgu_ref[:, lo:lo + FF_CHUNK])
        up = _dot(xb, wgu_ref[:, FF_HIDDEN + lo:FF_HIDDEN + lo + FF_CHUNK])
        act = (_silu(gate) * up).astype(BF16)
        down = _dot(act, wd_ref[lo:lo + FF_CHUNK, :])
        if j == 0:
            acc_ref[...] = down
        else:
            acc_ref[...] += down
    o_ref[...] = _layer_norm(alpha * x + 0.5 * acc_ref[...], g_ref[...], b_ref[...])


def _ffn_ln(x, wgu, wd, g, b, *, tm, alpha):
    m = x.shape[0]
    row = pl.BlockSpec((tm, D_MODEL), lambda i: (i, 0))
    return pl.pallas_call(
        functools.partial(_ffn_kernel, alpha=alpha),
        grid=(m // tm,),
        in_specs=[row, _resident(wgu.shape), _resident(wd.shape), _resident(g.shape), _resident(b.shape)],
        out_specs=row,
        out_shape=jax.ShapeDtypeStruct((m, D_MODEL), F32),
        scratch_shapes=[pltpu.VMEM((tm, D_MODEL), F32)],
        compiler_params=_cparams(1),
    )(x, wgu, wd, g, b)


def _softplus(x):
    return jnp.maximum(x, 0.0) + jnp.log1p(jnp.exp(-jnp.abs(x)))


def _rope_slice(x, cos_t, sin_up, sin_dn):
    return (x * cos_t + pltpu.roll(x, LANES - ROT_DIM // 2, axis=1) * sin_up
            + pltpu.roll(x, ROT_DIM // 2, axis=1) * sin_dn)


def _proj_ssd_kernel(x_ref, wz_ref, wxbc_ref, wdt_ref, dtb_ref, z_ref, xbc_ref, dt_ref):
    xb = x_ref[...].astype(BF16)
    cw = 512
    for c in range(SSD_INNER // cw):
        z_ref[:, c * cw:(c + 1) * cw] = _dot(xb, wz_ref[:, c * cw:(c + 1) * cw])
    for c in range(SSD_XBC // cw):
        xbc_ref[:, c * cw:(c + 1) * cw] = _dot(xb, wxbc_ref[:, c * cw:(c + 1) * cw])
    dt_ref[...] = _softplus(_dot(xb, wdt_ref[...]) + dtb_ref[...])


def _proj_ssd(x, wz, wxbc, wdt, dtb, *, tm):
    m = x.shape[0]

    def row(n):
        return pl.BlockSpec((tm, n), lambda i: (i, 0))

    return pl.pallas_call(
        _proj_ssd_kernel,
        grid=(m // tm,),
        in_specs=[row(D_MODEL), _resident(wz.shape), _resident(wxbc.shape), _resident(wdt.shape),
                  _resident(dtb.shape)],
        out_specs=[row(SSD_INNER), row(SSD_XBC), row(SSD_GROUPS * LANES)],
        out_shape=[jax.ShapeDtypeStruct((m, SSD_INNER), F32), jax.ShapeDtypeStruct((m, SSD_XBC), F32),
                   jax.ShapeDtypeStruct((m, SSD_GROUPS * LANES), F32)],
        compiler_params=_cparams(1),
    )(x, wz, wxbc, wdt, dtb)


def _proj_mix_kernel(x_ref, cos_ref, sup_ref, sdn_ref, wq_ref, wk_ref, wv_ref, wga_ref, wgb_ref,
                     q_ref, k_ref, v_ref, u_ref):
    xb = x_ref[...].astype(BF16)
    cos_t, sin_up, sin_dn = cos_ref[...], sup_ref[...], sdn_ref[...]
    cw = 256
    for c in range(DA_WIDTH // cw):
        lo = c * cw
        qc = _dot(xb, wq_ref[:, lo:lo + cw])
        kc = _dot(xb, wk_ref[:, lo:lo + cw])
        for s in range(cw // HEAD_W):
            sl = slice(s * HEAD_W, (s + 1) * HEAD_W)
            dst = slice(lo + s * HEAD_W, lo + (s + 1) * HEAD_W)
            q_ref[:, dst] = (_rope_slice(qc[:, sl], cos_t, sin_up, sin_dn)
                             * (DA_HEAD_DIM ** -0.5)).astype(q_ref.dtype)
            k_ref[:, dst] = _rope_slice(kc[:, sl], cos_t, sin_up, sin_dn)
        v_ref[:, lo:lo + cw] = _dot(xb, wv_ref[:, lo:lo + cw])
        ga = _dot(xb, wga_ref[:, lo:lo + cw])
        gb = _dot(xb, wgb_ref[:, lo:lo + cw])
        u_ref[:, lo:lo + cw] = ga * _sigmoid(gb)


def _proj_mix(x, tables, wq, wk, wv, wga, wgb, *, tm, q_dtype):
    m = x.shape[0]
    n_tab = tables[0].shape[0] // tm
    row = pl.BlockSpec((tm, D_MODEL), lambda i: (i, 0))
    tab = pl.BlockSpec((tm, LANES), lambda i: (i % n_tab, 0))
    w = _resident(wq.shape)
    return pl.pallas_call(
        _proj_mix_kernel,
        grid=(m // tm,),
        in_specs=[row, tab, tab, tab, w, w, w, w, w],
        out_specs=[row, row, row, row],
        out_shape=[jax.ShapeDtypeStruct((m, D_MODEL), q_dtype)] + [jax.ShapeDtypeStruct((m, D_MODEL), F32)] * 3,
        compiler_params=_cparams(1),
    )(x, *tables, wq, wk, wv, wga, wgb)


def _merge_kernel(h_ref, ys_ref, yo_ref, yc_ref, wgate_ref, bgate_ref, wssd_ref, wda_ref, wcc_ref, wout_ref,
                  g_ref, b_ref, o_ref, *, alpha):
    h = h_ref[...]
    hb = h.astype(BF16)
    merged = None
    for i, (y_ref, w_ref) in enumerate(((ys_ref, wssd_ref), (yo_ref, wda_ref), (yc_ref, wcc_ref))):
        sl = slice(i * D_MODEL, (i + 1) * D_MODEL)
        gate = _sigmoid(_dot(hb, wgate_ref[:, sl]) + bgate_ref[:, sl])
        term = gate * _dot(y_ref[...].astype(BF16), w_ref[...])
        merged = term if merged is None else merged + term
    mix = _dot(merged.astype(BF16), wout_ref[...])
    o_ref[...] = _layer_norm(alpha * h + mix, g_ref[...], b_ref[...])


def _merge(h, ys, yo, yc, wgate, bgate, wssd, wda, wcc, wout, g, b, *, tm, alpha):
    m = h.shape[0]

    def row(n):
        return pl.BlockSpec((tm, n), lambda i: (i, 0))

    consts = (wgate, bgate, wssd, wda, wcc, wout, g, b)
    return pl.pallas_call(
        functools.partial(_merge_kernel, alpha=alpha),
        grid=(m // tm,),
        in_specs=[row(D_MODEL), row(SSD_INNER), row(DA_WIDTH), row(CC_CH)] + [_resident(c.shape) for c in consts],
        out_specs=row(D_MODEL),
        out_shape=jax.ShapeDtypeStruct((m, D_MODEL), F32),
        compiler_params=_cparams(1),
    )(h, ys, yo, yc, *consts)


def _pair_cols(x, h0, h1, lane_lo):
    shape = (x.shape[0], LANES)
    return jnp.where(lane_lo[:x.shape[0]], jnp.broadcast_to(x[:, h0:h0 + 1], shape),
                     jnp.broadcast_to(x[:, h1:h1 + 1], shape))


def _ssd_intra(xs, bc, cc, dt, acol, mask, lane_lo):
    arow = acol.T
    bb = bc.astype(BF16)
    cb = _dot_nt(cc.astype(BF16), bb)
    y_in, xdts, aces = [], [], []
    for j in range(HEADS_PER_GROUP // 2):
        h0, h1 = 2 * j, 2 * j + 1
        gs = []
        for h in (h0, h1):
            delta = acol[:, h:h + 1] - arow[h:h + 1, :]
            gs.append((cb * jnp.exp(jnp.where(mask, delta, -jnp.inf))).astype(BF16))
        gp = jnp.concatenate(gs, axis=1)
        xdt = xs[:, j * LANES:(j + 1) * LANES] * _pair_cols(dt, h0, h1, lane_lo)
        zero = jnp.zeros_like(xdt)
        xbd = jnp.concatenate([jnp.where(lane_lo, xdt, zero), jnp.where(lane_lo, zero, xdt)], axis=0).astype(BF16)
        y_in.append(_dot(gp, xbd))
        xdts.append(xdt)
        aces.append(_pair_cols(acol, h0, h1, lane_lo))
    return y_in, xdts, aces, bb


def _ssd_finish(ys, xs, z, d_row, nw_row):
    gated, ss = [], None
    for j, y in enumerate(ys):
        sl = slice(j * LANES, (j + 1) * LANES)
        yz = (y + d_row[:, sl] * xs[:, sl]) * _silu(z[:, sl])
        gated.append(yz)
        part = jnp.sum(yz * yz, axis=1, keepdims=True)
        ss = part if ss is None else ss + part
    rs = lax.rsqrt(ss * (1.0 / GROUP_CH) + LN_EPS)
    return [(yz * rs * nw_row[:, j * LANES:(j + 1) * LANES]).astype(BF16) for j, yz in enumerate(gated)]


def _conv_silu(taps, w_ref, b_ref):
    rows, ch = taps[0].shape
    acc = jnp.broadcast_to(b_ref[...][None], (rows // 8, 8, ch))
    for k in range(SSD_CONV):
        acc = acc + w_ref[k][None] * taps[k].reshape(rows // 8, 8, ch)
    return _silu(acc).reshape(rows, ch)


def _ssd_prompt_kernel(xs_ref, bm_ref, cm_ref, z_ref, dt_ref, wx_ref, wb_ref, wc_ref, bx_ref, bb_ref, bc_ref,
                       a_ref, d_ref, nw_ref, y_ref, hfin_ref, win_ref, state_ref):
    L = SSD_CHUNK
    rows = lax.broadcasted_iota(jnp.int32, (L, LANES), 0)
    cols = lax.broadcasted_iota(jnp.int32, (L, LANES), 1)
    causal = rows >= cols
    tri = causal.astype(F32)
    lane_lo = cols < SSD_HEAD_DIM
    row_lo = rows < SSD_HEAD_DIM
    state_ref[...] = jnp.zeros_like(state_ref)
    srcs = ((xs_ref, 0, GROUP_CH), (bm_ref, GROUP_CH, GROUP_CH + SSD_STATE),
            (cm_ref, GROUP_CH + SSD_STATE, GROUP_CH + 2 * SSD_STATE))

    def chunk(r0, meta):
        for src, lo, hi in srcs:
            win_ref[8:8 + L, lo:hi] = src[pl.ds(r0, L), :]
            if meta:
                win_ref[0:8, lo:hi] = jnp.zeros((8, hi - lo), F32)
            else:
                win_ref[0:8, lo:hi] = src[pl.ds(r0 - 8, 8), :]
        first = 8 - (SSD_CONV - 1)

        def conv(lo, hi, w_ref, b_ref):
            return _conv_silu([win_ref[first + k:first + k + L, lo:hi] for k in range(SSD_CONV)], w_ref, b_ref)

        xs = conv(*srcs[0][1:], wx_ref, bx_ref)
        bc = conv(*srcs[1][1:], wb_ref, bb_ref)
        cc = conv(*srcs[2][1:], wc_ref, bc_ref)
        dt = dt_ref[pl.ds(r0, L), :]
        if meta:
            dt = jnp.where(rows < N_META, dt, 0.0)
        acol = _dot_exact(tri, dt * a_ref[...])
        alast = acol[L - 1:L, :]
        y_in, xdts, aces, bb = _ssd_intra(xs, bc, cc, dt, acol, causal, lane_lo)
        cb16 = cc.astype(BF16)
        ys = []
        for j in range(HEADS_PER_GROUP // 2):
            h0, h1 = 2 * j, 2 * j + 1
            st = state_ref[j * LANES:(j + 1) * LANES, :]
            ace = aces[j]
            ys.append(y_in[j] + _dot_nt(cb16, st.astype(BF16)) * jnp.exp(ace))
            ale = jnp.where(lane_lo[0:1], jnp.broadcast_to(alast[:, h0:h0 + 1], (1, LANES)),
                            jnp.broadcast_to(alast[:, h1:h1 + 1], (1, LANES)))
            xw = xdts[j] * jnp.exp(ale - ace)
            upd = _dot(xw.T.astype(BF16), bb)
            dec = jnp.where(row_lo, jnp.broadcast_to(jnp.exp(alast[:, h0:h0 + 1]), (L, LANES)),
                            jnp.broadcast_to(jnp.exp(alast[:, h1:h1 + 1]), (L, LANES)))
            state_ref[j * LANES:(j + 1) * LANES, :] = st * dec + upd
        outs = _ssd_finish(ys, xs, z_ref[pl.ds(r0, L), :], d_ref[...], nw_ref[...])
        for j, o in enumerate(outs):
            y_ref[pl.ds(r0, L), j * LANES:(j + 1) * LANES] = o

    chunk(0, True)

    def body(c, carry):
        chunk(pl.multiple_of(N_META + c * L, 16), False)
        return carry

    lax.fori_loop(0, (y_ref.shape[0] - N_META) // L, body, 0)
    hfin_ref[...] = state_ref[...]


def _ssd_prompt(xbc, z, dt, conv_w, conv_b, a4, d_e, nw, *, batch, seq):
    g_b = SSD_INNER // SSD_STATE

    def seq_block(width, col0=0):
        return pl.BlockSpec((seq, width), lambda b, g: (b, g + col0))

    def par_block(rows, width, col0=0):
        return pl.BlockSpec((rows, width), lambda b, g: (0, g + col0))

    def tap_block(width, col0=0):
        return pl.BlockSpec((SSD_CONV, 8, width), lambda b, g: (0, 0, g + col0))

    return pl.pallas_call(
        _ssd_prompt_kernel,
        grid=(batch, SSD_GROUPS),
        in_specs=[seq_block(GROUP_CH), seq_block(SSD_STATE, g_b), seq_block(SSD_STATE, g_b + SSD_GROUPS),
                  seq_block(GROUP_CH), seq_block(LANES),
                  tap_block(GROUP_CH), tap_block(SSD_STATE, g_b), tap_block(SSD_STATE, g_b + SSD_GROUPS),
                  par_block(8, GROUP_CH), par_block(8, SSD_STATE, g_b), par_block(8, SSD_STATE, g_b + SSD_GROUPS),
                  par_block(1, LANES), par_block(1, GROUP_CH), par_block(1, GROUP_CH)],
        out_specs=[seq_block(GROUP_CH), pl.BlockSpec((None, GROUP_CH, SSD_STATE), lambda b, g: (b, g, 0))],
        out_shape=[jax.ShapeDtypeStruct((batch * seq, SSD_INNER), BF16),
                   jax.ShapeDtypeStruct((batch, SSD_INNER, SSD_STATE), F32)],
        scratch_shapes=[pltpu.VMEM((8 + SSD_CHUNK, GROUP_CH + 2 * SSD_STATE), F32),
                        pltpu.VMEM((GROUP_CH, SSD_STATE), F32)],
        compiler_params=_cparams(2),
    )(xbc, xbc, xbc, z, dt, conv_w, conv_w, conv_w, conv_b, conv_b, conv_b, a4, d_e, nw)


def _ssd_sample_kernel(xs_ref, bm_ref, cm_ref, px_ref, pb_ref, pc_ref, z_ref, dt_ref, h0_ref,
                       wx_ref, wb_ref, wc_ref, bx_ref, bb_ref, bc_ref, a_ref, d_ref, nw_ref, *rest, n_prev):
    y_ref, h1_ref, ext_ref, flat_ref, xwt_ref, yst_ref = rest[-6:]
    if n_prev:
        h1_ref[0:n_prev] = rest[0][...]
    R = SAMPLE_ROWS
    rows = lax.broadcasted_iota(jnp.int32, (R, LANES), 0)
    cols = lax.broadcasted_iota(jnp.int32, (R, LANES), 1)
    same = (rows >> 3) == (cols >> 3)
    causal = same & (rows >= cols)
    lane_lo = cols < SSD_HEAD_DIM
    srcs = ((xs_ref, px_ref, 0, GROUP_CH), (bm_ref, pb_ref, GROUP_CH, GROUP_CH + SSD_STATE),
            (cm_ref, pc_ref, GROUP_CH + SSD_STATE, GROUP_CH + 2 * SSD_STATE))
    for cur, prev, lo, hi in srcs:
        ext_ref[:, 0:8, lo:hi] = prev[...]
        ext_ref[:, 8:16, lo:hi] = cur[...]
    first = 8 - (SSD_CONV - 1)
    for k in range(SSD_CONV):
        flat_ref[k] = ext_ref[:, first + k:first + k + 8, :].reshape(R, GROUP_CH + 2 * SSD_STATE)

    def conv(lo, hi, w_ref, b_ref):
        return _conv_silu([flat_ref[k, :, lo:hi] for k in range(SSD_CONV)], w_ref, b_ref)

    xs = conv(0, GROUP_CH, wx_ref, bx_ref)
    bc = conv(GROUP_CH, GROUP_CH + SSD_STATE, wb_ref, bb_ref)
    cc = conv(GROUP_CH + SSD_STATE, GROUP_CH + 2 * SSD_STATE, wc_ref, bc_ref)
    dt = dt_ref[...]
    dta = dt * a_ref[...]
    acol = _dot_exact(causal.astype(F32), dta)
    total = _dot_exact(same.astype(F32), dta)
    y_in, xdts, aces, _ = _ssd_intra(xs, bc, cc, dt, acol, causal, lane_lo)
    for j in range(HEADS_PER_GROUP // 2):
        xw = xdts[j] * jnp.exp(_pair_cols(total, 2 * j, 2 * j + 1, lane_lo) - aces[j])
        xwt_ref[j * LANES:(j + 1) * LANES, :] = xw.T.astype(BF16)
    xwt = xwt_ref[...]
    for i in range(SAMPLE_BLOCK):
        r = slice(8 * i, 8 * i + 8)
        st = h0_ref[i]
        c16 = jnp.concatenate([cc[r], jnp.zeros((8, SSD_STATE), F32)], axis=0).astype(BF16)
        yst_ref[r, :] = _dot_nt(c16, st.astype(BF16))[0:8]
        b_i = jnp.where((rows >> 3) == i, bc, 0.0).astype(BF16)
        upd = _dot(xwt, b_i)
        tot_i = total[8 * i:8 * i + 1, :]
        for h in range(HEADS_PER_GROUP):
            hs = slice(h * SSD_HEAD_DIM, (h + 1) * SSD_HEAD_DIM)
            h1_ref[n_prev, i, hs, :] = st[hs] * jnp.exp(tot_i[:, h:h + 1]) + upd[hs]
    ys = [y_in[j] + yst_ref[:, j * LANES:(j + 1) * LANES] * jnp.exp(aces[j])
          for j in range(HEADS_PER_GROUP // 2)]
    outs = _ssd_finish(ys, xs, z_ref[...], d_ref[...], nw_ref[...])
    for j, o in enumerate(outs):
        y_ref[:, j * LANES:(j + 1) * LANES] = o


def _ssd_sample(xbc3, prev4, z, dt, h0_all, conv_w, conv_b, a4, d_e, nw, *, layer, prev_h1=None):
    n_req = xbc3.shape[0]
    n_prev = 0 if prev_h1 is None else prev_h1.shape[0]
    g_b = SSD_INNER // SSD_STATE
    R = SAMPLE_ROWS
    width = GROUP_CH + 2 * SSD_STATE

    def tok_block(w, col0=0):
        return pl.BlockSpec((SAMPLE_BLOCK, 8, w), lambda i, g: (i, 0, g + col0))

    def prev_block(w, col0=0):
        return pl.BlockSpec((None, SAMPLE_BLOCK, 8, w), lambda i, g: (layer, i, 0, g + col0))

    def row_block(w):
        return pl.BlockSpec((R, w), lambda i, g: (i, g))

    def par_block(rows, w, col0=0):
        return pl.BlockSpec((rows, w), lambda i, g: (0, g + col0))

    def tap_block(w, col0=0):
        return pl.BlockSpec((SSD_CONV, 8, w), lambda i, g: (0, 0, g + col0))

    state = pl.BlockSpec((None, SAMPLE_BLOCK, GROUP_CH, SSD_STATE), lambda i, g: (layer, i, g, 0))
    toks = [tok_block(GROUP_CH), tok_block(SSD_STATE, g_b), tok_block(SSD_STATE, g_b + SSD_GROUPS)]
    prevs = [prev_block(GROUP_CH), prev_block(SSD_STATE, g_b), prev_block(SSD_STATE, g_b + SSD_GROUPS)]
    in_specs = toks + prevs + [row_block(GROUP_CH), row_block(LANES), state,
                               tap_block(GROUP_CH), tap_block(SSD_STATE, g_b),
                               tap_block(SSD_STATE, g_b + SSD_GROUPS),
                               par_block(8, GROUP_CH), par_block(8, SSD_STATE, g_b),
                               par_block(8, SSD_STATE, g_b + SSD_GROUPS),
                               par_block(1, LANES), par_block(1, GROUP_CH), par_block(1, GROUP_CH)]
    args = [xbc3, xbc3, xbc3, prev4, prev4, prev4, z, dt, h0_all, conv_w, conv_w, conv_w, conv_b, conv_b, conv_b,
            a4, d_e, nw]
    def stack_block(n):
        return pl.BlockSpec((n, SAMPLE_BLOCK, GROUP_CH, SSD_STATE), lambda i, g: (0, i, g, 0))

    if n_prev:
        in_specs = in_specs + [stack_block(n_prev)]
        args = args + [prev_h1]
    return pl.pallas_call(
        functools.partial(_ssd_sample_kernel, n_prev=n_prev),
        grid=(n_req // SAMPLE_BLOCK, SSD_GROUPS),
        in_specs=in_specs,
        out_specs=[row_block(GROUP_CH), stack_block(n_prev + 1)],
        out_shape=[jax.ShapeDtypeStruct((n_req * 8, SSD_INNER), BF16),
                   jax.ShapeDtypeStruct((n_prev + 1, n_req, SSD_INNER, SSD_STATE), F32)],
        scratch_shapes=[pltpu.VMEM((SAMPLE_BLOCK, 16, width), F32), pltpu.VMEM((SSD_CONV, R, width), F32),
                        pltpu.VMEM((GROUP_CH, R), BF16), pltpu.VMEM((R, GROUP_CH), F32)],
        compiler_params=_cparams(2),
    )(*args)


def _sub_ln(o, w_row, post_scale):
    return o * lax.rsqrt(jnp.mean(o * o, axis=-1, keepdims=True) + LN_EPS) * w_row * post_scale


def _split_components(q):
    lane_lo = lax.broadcasted_iota(jnp.int32, q.shape, 1) < DA_HEAD_DIM
    zero = jnp.zeros_like(q)
    return jnp.concatenate([jnp.where(lane_lo, q, zero), jnp.where(lane_lo, zero, q)], axis=0)


def _attn_prompt_kernel(lam_ref, q_ref, k_ref, v_ref, w_ref, *rest, n_prev, post_scale):
    kb_ref, vb_ref = rest[-2:]
    if n_prev:
        kp_ref, vp_ref, o_ref, kall_ref, vall_ref = rest[:5]
        kall_ref[0:n_prev] = kp_ref[...]
        vall_ref[0:n_prev] = vp_ref[...]
        kall_ref[n_prev] = k_ref[...]
        vall_ref[n_prev] = v_ref[...]
    else:
        o_ref = rest[0]
    lam = lam_ref[0]
    kb_ref[...] = k_ref[...].astype(BF16)
    vb_ref[:, 0:HEAD_W] = v_ref[...].astype(BF16)
    vb_ref[:, HEAD_W:2 * HEAD_W] = jnp.ones((vb_ref.shape[0], HEAD_W), BF16)
    k_meta = kb_ref[0:LANES]
    v_meta = vb_ref[0:LANES]
    w_row = w_ref[...]

    def attend(r0, n_q, parts):
        m = None
        for s, _ in parts:
            pm = jnp.max(s, axis=1, keepdims=True)
            m = pm if m is None else jnp.maximum(m, pm)
        acc = None
        for s, vals in parts:
            p = jnp.exp((s - m).astype(BF16))
            term = _dot(p, vals)
            acc = term if acc is None else acc + term
        on = acc[:, 0:HEAD_W] * (1.0 / acc[:, HEAD_W:HEAD_W + 1])
        o = on[0:n_q] - lam * on[n_q:2 * n_q]
        o_ref[r0:r0 + n_q, :] = _sub_ln(o, w_row, post_scale).astype(o_ref.dtype)

    q2 = _split_components(q_ref[0:N_META])
    s = _dot_nt(q2, k_meta)
    qpos = lax.broadcasted_iota(jnp.int32, s.shape, 0) & (N_META - 1)
    kpos = lax.broadcasted_iota(jnp.int32, s.shape, 1)
    attend(0, N_META, [(jnp.where(kpos <= qpos, s, -jnp.inf), v_meta)])

    n_blk = (q_ref.shape[0] - N_META) // Q_BLOCK
    blk_shape = (2 * Q_BLOCK, Q_BLOCK)
    kcol = lax.broadcasted_iota(jnp.int32, blk_shape, 1)
    meta_ok = lax.broadcasted_iota(jnp.int32, (2 * Q_BLOCK, LANES), 1) < N_META
    diag_ok = kcol <= (lax.broadcasted_iota(jnp.int32, blk_shape, 0) & (Q_BLOCK - 1))

    def scores(i):
        r0 = N_META + i * Q_BLOCK
        q2 = _split_components(q_ref[r0:r0 + Q_BLOCK])
        parts = [(jnp.where(meta_ok, _dot_nt(q2, k_meta), -jnp.inf), v_meta)]
        if i > 0:
            parts.append((_dot_nt(q2, kb_ref[N_META:r0]), vb_ref[N_META:r0]))
        parts.append((jnp.where(diag_ok, _dot_nt(q2, kb_ref[r0:r0 + Q_BLOCK]), -jnp.inf),
                      vb_ref[r0:r0 + Q_BLOCK]))
        return parts

    nxt = scores(0)
    for i in range(n_blk):
        cur, nxt = nxt, (scores(i + 1) if i + 1 < n_blk else None)
        attend(N_META + i * Q_BLOCK, Q_BLOCK, cur)


def _attn_prompt(lam, q, k, v, w_row, *, batch, seq, post_scale, prev_kv=None):
    blk = pl.BlockSpec((seq, HEAD_W), lambda b, h: (b, h))
    in_specs = [pl.BlockSpec(memory_space=pltpu.SMEM), blk, blk, blk, _resident(w_row.shape)]
    args = [lam, q, k, v, w_row]
    out_specs = [blk]
    out_shape = [jax.ShapeDtypeStruct((batch * seq, DA_WIDTH), BF16)]
    n_prev = 0
    if prev_kv is not None:
        n_prev = prev_kv[0].shape[0]
        in_specs += [pl.BlockSpec((n_prev, seq, HEAD_W), lambda b, h: (0, b, h))] * 2
        args += list(prev_kv)
        out_specs += [pl.BlockSpec((n_prev + 1, seq, HEAD_W), lambda b, h: (0, b, h))] * 2
        out_shape += [jax.ShapeDtypeStruct((n_prev + 1, batch * seq, DA_WIDTH), F32)] * 2
    return pl.pallas_call(
        functools.partial(_attn_prompt_kernel, n_prev=n_prev, post_scale=post_scale),
        grid=(batch, DA_HEADS),
        in_specs=in_specs,
        out_specs=out_specs,
        out_shape=out_shape,
        scratch_shapes=[pltpu.VMEM((seq, HEAD_W), BF16), pltpu.VMEM((seq, 2 * HEAD_W), BF16)],
        compiler_params=_cparams(2),
    )(*args)


def _attn_sample_kernel(pt_ref, lam_ref, q_ref, kn_ref, vn_ref, w_ref, *refs, n_pages, post_scale):
    del pt_ref
    k_pages = refs[:n_pages]
    v_pages = refs[n_pages:2 * n_pages]
    o_ref, s_ref = refs[2 * n_pages], refs[2 * n_pages + 1]
    lam = lam_ref[0]
    page_rows = PAGE_SIZE * DA_HEADS
    new_rows = 8 * DA_HEADS
    q = q_ref[...]
    lane = lax.broadcasted_iota(jnp.int32, (8, HEAD_W), 1)
    pieces = []
    for c in range(2):
        for h in range(DA_HEADS):
            qh = q[:, h * HEAD_W:(h + 1) * HEAD_W]
            keep = lane < DA_HEAD_DIM if c == 0 else lane >= DA_HEAD_DIM
            pieces.append(jnp.where(keep, qh, 0.0))
    qmat = jnp.concatenate(pieces, axis=0).astype(BF16)
    row_head = (lax.broadcasted_iota(jnp.int32, (LANES, page_rows), 0) >> 3) & (DA_HEADS - 1)
    col_head = lax.broadcasted_iota(jnp.int32, (LANES, page_rows), 1) & (DA_HEADS - 1)
    head_ok = row_head == col_head
    s_new = _dot_nt(qmat, kn_ref[...].astype(BF16))
    r_i = lax.broadcasted_iota(jnp.int32, s_new.shape, 0)
    c_i = lax.broadcasted_iota(jnp.int32, s_new.shape, 1)
    ok_new = (((r_i >> 3) & (DA_HEADS - 1)) == (c_i & (DA_HEADS - 1))) & ((c_i >> 3) <= (r_i & 7))
    s_new = jnp.where(ok_new, s_new, -jnp.inf)
    m = jnp.max(s_new, axis=1, keepdims=True)
    for p in range(n_pages):
        s = _dot_nt(qmat, k_pages[p][...].astype(BF16))
        s = jnp.where(head_ok, s, -jnp.inf)
        s_ref[:, p * page_rows:(p + 1) * page_rows] = s
        m = jnp.maximum(m, jnp.max(s, axis=1, keepdims=True))
    p_new = jnp.exp(s_new - m)
    den = jnp.sum(p_new, axis=1, keepdims=True)
    acc = _dot(p_new.astype(BF16), vn_ref[...].astype(BF16))
    for p in range(n_pages):
        e = jnp.exp(s_ref[:, p * page_rows:(p + 1) * page_rows] - m)
        den = den + jnp.sum(e, axis=1, keepdims=True)
        acc = acc + _dot(e.astype(BF16), v_pages[p][...].astype(BF16))
    on = acc * (1.0 / den)
    half = LANES // 2
    o = on[0:half] - lam * on[half:LANES]
    w_row = w_ref[...]
    for h in range(DA_HEADS):
        o_ref[:, h * HEAD_W:(h + 1) * HEAD_W] = _sub_ln(o[h * 8:(h + 1) * 8], w_row, post_scale)


def _attn_sample(page_table_flat, lam, q, k_new, v_new, w_row, cache_k, cache_v, *, layer, post_scale):
    n_req = q.shape[0] // 8
    n_pages = page_table_flat.shape[0] // n_req
    page_rows = PAGE_SIZE * DA_HEADS

    def page_spec(j):
        return pl.BlockSpec((None, None, page_rows, HEAD_W), lambda b, pt: (layer, pt[b * n_pages + j], 0, 0))

    tok = pl.BlockSpec((8, DA_WIDTH), lambda b, pt: (b, 0))
    new = pl.BlockSpec((None, 8 * DA_HEADS, HEAD_W), lambda b, pt: (b, 0, 0))
    pages = [page_spec(j) for j in range(n_pages)]
    grid_spec = pltpu.PrefetchScalarGridSpec(
        num_scalar_prefetch=1,
        grid=(n_req,),
        in_specs=[pl.BlockSpec(memory_space=pltpu.SMEM), tok, new, new,
                  pl.BlockSpec(w_row.shape, lambda b, pt: (0, 0))] + pages + pages,
        out_specs=tok,
        scratch_shapes=[pltpu.VMEM((LANES, n_pages * page_rows), F32)],
    )
    return pl.pallas_call(
        functools.partial(_attn_sample_kernel, n_pages=n_pages, post_scale=post_scale),
        grid_spec=grid_spec,
        out_shape=jax.ShapeDtypeStruct((n_req * 8, DA_WIDTH), F32),
        compiler_params=_cparams(1),
    )(page_table_flat, lam, q, k_new, v_new, w_row, *([cache_k] * n_pages), *([cache_v] * n_pages))


CC_HIST = 32
CC_ROWS = 48


def _cc_taps(load_window, w_ref, b_ref, lanes, rows):
    first = CC_HIST - (CC_KERNEL - 1)
    nt = rows // 8
    acc = jnp.broadcast_to(b_ref[:, lanes][None], (nt, 8, LANES))
    for phase in range(8):
        n_tap = (CC_KERNEL - 1 - phase) // 8 + 1
        span = rows + 8 * (n_tap - 1)
        win = load_window(first + phase, span).reshape(span // 8, 8, LANES)
        for a in range(n_tap):
            acc = acc + w_ref[8 * a + phase, :, lanes][None] * win[a:a + nt]
    return acc.reshape(rows, LANES)


def _cc_prompt_kernel(u_ref, w_ref, b_ref, g_ref, beta_ref, o_ref, win_ref, conv_ref):
    def body(c, carry):
        r0 = pl.multiple_of(c * CC_ROWS, 16)
        win_ref[CC_HIST:CC_HIST + CC_ROWS, :] = u_ref[pl.ds(r0, CC_ROWS), :]

        @pl.when(c == 0)
        def _():
            win_ref[0:CC_HIST, :] = jnp.zeros((CC_HIST, CC_CH), F32)

        @pl.when(c > 0)
        def _():
            win_ref[0:CC_HIST, :] = u_ref[pl.ds(pl.multiple_of(jnp.maximum(r0 - CC_HIST, 0), 16), CC_HIST), :]

        for lb in range(CC_CH // LANES):
            lanes = slice(lb * LANES, (lb + 1) * LANES)
            conv_ref[:, lanes] = _cc_taps(lambda start, span: win_ref[start:start + span, lanes],
                                          w_ref, b_ref, lanes, CC_ROWS)
        o_ref[pl.ds(r0, CC_ROWS), :] = _silu(_layer_norm(conv_ref[...], g_ref[...],
                                                         beta_ref[...])).astype(o_ref.dtype)
        return carry

    lax.fori_loop(0, u_ref.shape[0] // CC_ROWS, body, 0)


def _cc_prompt(u, w, b, g, beta, *, batch, seq):
    blk = pl.BlockSpec((seq, CC_CH), lambda i: (i, 0))
    return pl.pallas_call(
        _cc_prompt_kernel,
        grid=(batch,),
        in_specs=[blk] + [_resident(c.shape) for c in (w, b, g, beta)],
        out_specs=blk,
        out_shape=jax.ShapeDtypeStruct((batch * seq, CC_CH), BF16),
        scratch_shapes=[pltpu.VMEM((CC_HIST + CC_ROWS, CC_CH), F32), pltpu.VMEM((CC_ROWS, CC_CH), F32)],
        compiler_params=_cparams(1),
    )(u, w, b, g, beta)


def _cc_sample_kernel(u_ref, prev_ref, w_ref, b_ref, g_ref, beta_ref, o_ref, ext_ref):
    ext_ref[:, 0:CC_HIST, :] = prev_ref[...]
    ext_ref[:, CC_HIST:CC_HIST + 8, :] = u_ref[...]
    first = CC_HIST - (CC_KERNEL - 1)
    acc = jnp.broadcast_to(b_ref[...][None], (SAMPLE_BLOCK, 8, CC_CH))
    for k in range(CC_KERNEL):
        acc = acc + w_ref[k][None] * ext_ref[:, first + k:first + k + 8, :]
    acc = acc.reshape(SAMPLE_ROWS, CC_CH)
    o_ref[...] = _silu(_layer_norm(acc, g_ref[...], beta_ref[...])).astype(o_ref.dtype)


def _cc_sample(u3, prev4, w, b, g, beta, *, layer):
    n_req = u3.shape[0]
    return pl.pallas_call(
        _cc_sample_kernel,
        grid=(n_req // SAMPLE_BLOCK,),
        in_specs=[pl.BlockSpec((SAMPLE_BLOCK, 8, CC_CH), lambda i: (i, 0, 0)),
                  pl.BlockSpec((None, SAMPLE_BLOCK, CC_HIST, CC_CH), lambda i: (layer, i, 0, 0))]
        + [_resident(c.shape) for c in (w, b, g, beta)],
        out_specs=pl.BlockSpec((SAMPLE_ROWS, CC_CH), lambda i: (i, 0)),
        out_shape=jax.ShapeDtypeStruct((n_req * 8, CC_CH), BF16),
        scratch_shapes=[pltpu.VMEM((SAMPLE_BLOCK, CC_HIST + 8, CC_CH), F32)],
        compiler_params=_cparams(1),
    )(u3, prev4, w, b, g, beta)


def _rope_tables(pos):
    half = ROT_DIM // 2
    inv_freq = ROPE_THETA ** (-jnp.arange(half, dtype=F32) * 2.0 / ROT_DIM)
    ang = pos.astype(F32)[:, None] * inv_freq[None, :]
    cos, sin = jnp.cos(ang), jnp.sin(ang)
    n = pos.shape[0]
    pad = jnp.zeros((n, DA_HEAD_DIM - ROT_DIM), F32)
    zeros = jnp.zeros((n, half), F32)
    cos_c = jnp.concatenate([cos, cos, pad + 1.0], axis=1)
    up_c = jnp.concatenate([-sin, zeros, pad], axis=1)
    dn_c = jnp.concatenate([zeros, sin, pad], axis=1)
    return tuple(jnp.concatenate([t, t], axis=1) for t in (cos_c, up_c, dn_c))


def _head_lanes(vec):
    v = vec.astype(F32).reshape(SSD_GROUPS, HEADS_PER_GROUP)
    return jnp.pad(v, ((0, 0), (0, LANES - HEADS_PER_GROUP))).reshape(1, SSD_GROUPS * LANES)


def _sublane_repeat(w):
    w = w.astype(F32)
    return jnp.broadcast_to(w[..., None, :], w.shape[:-1] + (8, w.shape[-1]))


def _prepare_layer(l, p):
    pts = [0]
    for s in IN_SIZES:
        pts.append(pts[-1] + s)
    w_in = p['w_in'][l]
    cols = [w_in[:, pts[i]:pts[i + 1]] for i in range(len(IN_SIZES))]
    w_dt = cols[2].reshape(D_MODEL, SSD_GROUPS, HEADS_PER_GROUP)
    w_dt = jnp.pad(w_dt, ((0, 0), (0, 0), (0, LANES - HEADS_PER_GROUP))).reshape(D_MODEL, SSD_GROUPS * LANES)
    row = lambda v: v.astype(F32).reshape(1, -1)
    lam_init = 0.8 - 0.6 * math.exp(-0.3 * l)
    lam = (jnp.exp(jnp.sum(p['da_lambda_q1'][l].astype(F32) * p['da_lambda_k1'][l].astype(F32)))
           - jnp.exp(jnp.sum(p['da_lambda_q2'][l].astype(F32) * p['da_lambda_k2'][l].astype(F32))) + lam_init)
    return dict(
        lam_init=lam_init, lam=lam.reshape(1).astype(F32),
        ln_g=[row(p['ln_g'][l, i]) for i in range(3)], ln_b=[row(p['ln_b'][l, i]) for i in range(3)],
        ff1_gu=p['w_ff1_gu'][l].astype(BF16), ff1_d=p['w_ff1_down'][l].astype(BF16),
        ff2_gu=p['w_ff2_gu'][l].astype(BF16), ff2_d=p['w_ff2_down'][l].astype(BF16),
        w_z=cols[0].astype(BF16), w_xbc=cols[1].astype(BF16), w_dt=w_dt.astype(BF16),
        w_q=cols[3].astype(BF16), w_k=cols[4].astype(BF16), w_v=cols[5].astype(BF16),
        w_ga=cols[6][:, :CC_CH].astype(BF16), w_gb=cols[6][:, CC_CH:].astype(BF16),
        w_gate=cols[7].astype(BF16), b_gate=row(p['b_gate'][l]),
        dt_bias=_head_lanes(p['ssd_dt_bias'][l]), a4=_head_lanes(-jnp.exp(p['ssd_a_log'][l].astype(F32))),
        conv_w=_sublane_repeat(p['ssd_conv_w'][l]), conv_b=_sublane_repeat(p['ssd_conv_b'][l]),
        d_e=jnp.repeat(p['ssd_d'][l].astype(F32), SSD_HEAD_DIM).reshape(1, SSD_INNER),
        norm_w=row(p['ssd_norm_w'][l]), ssd_out=p['ssd_out'][l].astype(BF16),
        subln_w=row(p['da_subln_w'][l]), da_out=p['da_out'][l].astype(BF16),
        cc_w=_sublane_repeat(p['cc_conv_w'][l]), cc_b=_sublane_repeat(p['cc_conv_b'][l]),
        cc_g=row(p['cc_ln_g'][l]), cc_beta=row(p['cc_ln_b'][l]), cc_out=p['cc_out'][l].astype(BF16),
        w_out=p['w_out'][l].astype(BF16),
    )


def _mixer_tail(h1, ys, yo, yc, lp, alpha, tm_merge, tm_ffn):
    h2 = _merge(h1, ys, yo, yc, lp['w_gate'], lp['b_gate'], lp['ssd_out'], lp['da_out'], lp['cc_out'], lp['w_out'],
                lp['ln_g'][1], lp['ln_b'][1], tm=tm_merge, alpha=alpha)
    return _ffn_ln(h2, lp['ff2_gu'], lp['ff2_d'], lp['ln_g'][2], lp['ln_b'][2], tm=tm_ffn, alpha=alpha)


def kernel(x_prompt, x_sample, cache_k, cache_v, state_ssm, state_ssd_conv, state_conf_conv, page_table,
           meta_tokens, ln_g, ln_b, w_ff1_gu, w_ff1_down, w_ff2_gu, w_ff2_down, w_in, b_gate,
           ssd_conv_w, ssd_conv_b, ssd_dt_bias, ssd_a_log, ssd_d, ssd_norm_w, ssd_out,
           da_lambda_q1, da_lambda_k1, da_lambda_q2, da_lambda_k2, da_subln_w, da_out,
           cc_conv_w, cc_conv_b, cc_ln_g, cc_ln_b, cc_out, w_out):
    params = dict(ln_g=ln_g, ln_b=ln_b, w_ff1_gu=w_ff1_gu, w_ff1_down=w_ff1_down, w_ff2_gu=w_ff2_gu,
                  w_ff2_down=w_ff2_down, w_in=w_in, b_gate=b_gate, ssd_conv_w=ssd_conv_w, ssd_conv_b=ssd_conv_b,
                  ssd_dt_bias=ssd_dt_bias, ssd_a_log=ssd_a_log, ssd_d=ssd_d, ssd_norm_w=ssd_norm_w, ssd_out=ssd_out,
                  da_lambda_q1=da_lambda_q1, da_lambda_k1=da_lambda_k1, da_lambda_q2=da_lambda_q2,
                  da_lambda_k2=da_lambda_k2, da_subln_w=da_subln_w, da_out=da_out, cc_conv_w=cc_conv_w,
                  cc_conv_b=cc_conv_b, cc_ln_g=cc_ln_g, cc_ln_b=cc_ln_b, cc_out=cc_out, w_out=w_out)
    depth = w_in.shape[0]
    alpha = (2.0 * depth) ** 0.25
    bp, seq = x_prompt.shape[0], x_prompt.shape[1]
    lp_len = seq + N_META
    ds, dec_len = x_sample.shape[0], x_sample.shape[1]
    assert dec_len == 8 and lp_len % 48 == 0 and ds % SAMPLE_BLOCK == 0
    n_pages = page_table.shape[1]
    past = n_pages * cache_k.shape[2]
    n_pool = cache_k.shape[1]

    tm_p = lp_len // 3
    tm_pm = 384
    tm_s = 512
    assert (bp * lp_len) % tm_pm == 0 and (ds * dec_len) % tm_s == 0

    hp = jnp.concatenate([jnp.broadcast_to(meta_tokens.astype(F32)[None], (bp, N_META, D_MODEL)),
                          x_prompt.astype(F32)], axis=1).reshape(bp * lp_len, D_MODEL)
    hs = x_sample.astype(F32).reshape(ds * dec_len, D_MODEL)
    tab_p = _rope_tables(jnp.arange(lp_len))
    tab_s = tuple(jnp.tile(t, (tm_s // dec_len, 1)) for t in _rope_tables(past + jnp.arange(dec_len)))
    ck = cache_k.reshape(depth, n_pool, PAGE_SIZE * DA_HEADS, HEAD_W)
    cv = cache_v.reshape(depth, n_pool, PAGE_SIZE * DA_HEADS, HEAD_W)
    pt_flat = page_table.reshape(-1).astype(jnp.int32)

    h0_all = state_ssm.astype(F32).reshape(depth, ds, SSD_INNER, SSD_STATE)
    conv_prev4 = jnp.pad(state_ssd_conv.astype(F32), ((0, 0), (0, 0), (8 - (SSD_CONV - 1), 0), (0, 0)))
    cc_prev4 = jnp.pad(state_conf_conv.astype(F32), ((0, 0), (0, 0), (CC_HIST - (CC_KERNEL - 1), 0), (0, 0)))
    kv_p, h1_buf = None, None

    outs = {k: [] for k in ('ks', 'vs', 'hp', 'cp', 'cs', 'up', 'us')}
    for l in range(depth):
        lp = _prepare_layer(l, params)
        post = 1.0 - lp['lam_init']

        h1 = _ffn_ln(hp, lp['ff1_gu'], lp['ff1_d'], lp['ln_g'][0], lp['ln_b'][0], tm=tm_p, alpha=alpha)
        z, xbc, dt = _proj_ssd(h1, lp['w_z'], lp['w_xbc'], lp['w_dt'], lp['dt_bias'], tm=tm_p)
        q, k, v, u = _proj_mix(h1, tab_p, lp['w_q'], lp['w_k'], lp['w_v'], lp['w_ga'], lp['w_gb'],
                               tm=tm_p, q_dtype=BF16)
        ys, hfin = _ssd_prompt(xbc, z, dt, lp['conv_w'], lp['conv_b'], lp['a4'], lp['d_e'], lp['norm_w'],
                               batch=bp, seq=lp_len)
        att = _attn_prompt(lp['lam'], q, k, v, lp['subln_w'], batch=bp, seq=lp_len, post_scale=post,
                           prev_kv=kv_p)
        yo = att[0]
        kv_p = (k[None], v[None]) if kv_p is None else (att[1], att[2])
        yc = _cc_prompt(u, lp['cc_w'], lp['cc_b'], lp['cc_g'], lp['cc_beta'], batch=bp, seq=lp_len)
        hp = _mixer_tail(h1, ys, yo, yc, lp, alpha, tm_pm, tm_p)
        outs['hp'].append(hfin.reshape(bp, SSD_HEADS, SSD_HEAD_DIM, SSD_STATE))
        outs['cp'].append(xbc.reshape(bp, lp_len, SSD_XBC)[:, lp_len - (SSD_CONV - 1):])
        outs['up'].append(u.reshape(bp, lp_len, CC_CH)[:, lp_len - (CC_KERNEL - 1):])

        h1 = _ffn_ln(hs, lp['ff1_gu'], lp['ff1_d'], lp['ln_g'][0], lp['ln_b'][0], tm=tm_s, alpha=alpha)
        z, xbc, dt = _proj_ssd(h1, lp['w_z'], lp['w_xbc'], lp['w_dt'], lp['dt_bias'], tm=tm_s)
        q, k, v, u = _proj_mix(h1, tab_s, lp['w_q'], lp['w_k'], lp['w_v'], lp['w_ga'], lp['w_gb'],
                               tm=tm_s, q_dtype=F32)
        xbc3 = xbc.reshape(ds, dec_len, SSD_XBC)
        ys, h1_buf = _ssd_sample(xbc3, conv_prev4, z, dt, h0_all, lp['conv_w'], lp['conv_b'], lp['a4'], lp['d_e'],
                                 lp['norm_w'], layer=l, prev_h1=h1_buf)
        yo = _attn_sample(pt_flat, lp['lam'], q, k.reshape(ds, dec_len * DA_HEADS, HEAD_W),
                          v.reshape(ds, dec_len * DA_HEADS, HEAD_W), lp['subln_w'], ck, cv,
                          layer=l, post_scale=post)
        u3 = u.reshape(ds, dec_len, CC_CH)
        yc = _cc_sample(u3, cc_prev4, lp['cc_w'], lp['cc_b'], lp['cc_g'], lp['cc_beta'], layer=l)
        hs = _mixer_tail(h1, ys, yo, yc, lp, alpha, tm_s, tm_s)
        outs['ks'].append(k.reshape(ds, dec_len, DA_HEADS, HEAD_W))
        outs['vs'].append(v.reshape(ds, dec_len, DA_HEADS, HEAD_W))
        outs['cs'].append(xbc3)
        outs['us'].append(u3)

    y_prompt = hp.reshape(bp, lp_len, D_MODEL)[:, N_META:]
    y_sample = hs.reshape(ds, dec_len, D_MODEL)
    st = {k: jnp.stack(v) for k, v in outs.items()}
    k_prompt = kv_p[0].reshape(depth, bp, lp_len, DA_HEADS, HEAD_W)
    v_prompt = kv_p[1].reshape(depth, bp, lp_len, DA_HEADS, HEAD_W)
    ssm_sample = h1_buf.reshape(depth, ds, SSD_HEADS, SSD_HEAD_DIM, SSD_STATE)
    conv_sample = jnp.concatenate([state_ssd_conv.astype(F32), st['cs']], axis=2)[:, :, -(SSD_CONV - 1):]
    cc_sample = jnp.concatenate([state_conf_conv.astype(F32), st['us']], axis=2)[:, :, -(CC_KERNEL - 1):]
    return (y_prompt, y_sample, k_prompt, v_prompt, st['ks'], st['vs'], st['hp'], ssm_sample,
            st['cp'], conv_sample, st['up'], cc_sample)
```

```python
import functools
import math

import jax
import jax.numpy as jnp
from jax import lax
from jax.experimental import pallas as pl
from jax.experimental.pallas import tpu as pltpu

F32 = jnp.float32
BF16 = jnp.bfloat16

D_MODEL = 1024
N_META = 16
SSD_INNER = 2048
SSD_HEAD_DIM = 64
SSD_HEADS = 32
SSD_STATE = 128
SSD_GROUPS = 4
SSD_CONV = 4
SSD_CHUNK = 128
SSD_XBC = SSD_INNER + 2 * SSD_GROUPS * SSD_STATE
GROUP_CH = SSD_INNER // SSD_GROUPS
HEADS_PER_GROUP = SSD_HEADS // SSD_GROUPS
DA_HEAD_DIM = 64
DA_HEADS = 8
DA_WIDTH = 1024
HEAD_W = 2 * DA_HEAD_DIM
ROT_DIM = DA_HEAD_DIM // 4
ROPE_THETA = 500000.0
Q_BLOCK = 256
PAGE_SIZE = 128
CC_CH = 1024
CC_KERNEL = 31
FF_HIDDEN = 2816
N_BRANCH = 3
IN_SIZES = (SSD_INNER, SSD_XBC, SSD_HEADS, DA_WIDTH, DA_WIDTH, DA_WIDTH, 2 * CC_CH, N_BRANCH * D_MODEL)
LN_EPS = 1e-5

VMEM_LIMIT_V7X = 52 * 1024 * 1024
LANES = 128
SAMPLE_BLOCK = 16
SAMPLE_ROWS = SAMPLE_BLOCK * 8


def _cparams(n_axes):
    return pltpu.CompilerParams(dimension_semantics=("arbitrary",) * n_axes,
                                vmem_limit_bytes=VMEM_LIMIT_V7X)


def _resident(shape):
    nd = len(shape)
    return pl.BlockSpec(shape, lambda *_: (0,) * nd, pipeline_mode=pl.Buffered(1))


def _dot(a, b):
    return jnp.dot(a, b, preferred_element_type=F32)


def _dot_nt(a, b):
    return lax.dot_general(a, b, (((1,), (1,)), ((), ())), preferred_element_type=F32)


def _dot_exact(a, b):
    return jnp.dot(a, b, preferred_element_type=F32, precision=lax.Precision.HIGHEST)


def _sigmoid(x):
    return 0.5 * jnp.tanh(0.5 * x) + 0.5


def _silu(x):
    return x * _sigmoid(x)


def _layer_norm(x, g, b):
    xc = x - jnp.mean(x, axis=-1, keepdims=True)
    var = jnp.mean(xc * xc, axis=-1, keepdims=True)
    return xc * lax.rsqrt(var + LN_EPS) * g + b


FF_CHUNK = 256


def _ffn_kernel(x_ref, wgu_ref, wd_ref, g_ref, b_ref, o_ref, acc_ref, *, alpha):
    x = x_ref[...]
    xb = x.astype(BF16)
    for j in range(FF_HIDDEN // FF_CHUNK):
        lo = j * FF_CHUNK
        gate = _dot(xb, wgu_ref[:, lo:lo + FF_CHUNK])
        up = _dot(xb, wgu_ref[:, FF_HIDDEN + lo:FF_HIDDEN + lo + FF_CHUNK])
        act = (_silu(gate) * up).astype(BF16)
        down = _dot(act, wd_ref[lo:lo + FF_CHUNK, :])
        if j == 0:
            acc_ref[...] = down
        else:
            acc_ref[...] += down
    o_ref[...] = _layer_norm(alpha * x + 0.5 * acc_ref[...], g_ref[...], b_ref[...])


def _ffn_ln(x, wgu, wd, g, b, *, tm, alpha):
    m = x.shape[0]
    row = pl.BlockSpec((tm, D_MODEL), lambda i: (i, 0))
    return pl.pallas_call(
        functools.partial(_ffn_kernel, alpha=alpha),
        grid=(m // tm,),
        in_specs=[row, _resident(wgu.shape), _resident(wd.shape), _resident(g.shape), _resident(b.shape)],
        out_specs=row,
        out_shape=jax.ShapeDtypeStruct((m, D_MODEL), F32),
        scratch_shapes=[pltpu.VMEM((tm, D_MODEL), F32)],
        compiler_params=_cparams(1),
    )(x, wgu, wd, g, b)


def _softplus(x):
    return jnp.maximum(x, 0.0) + jnp.log1p(jnp.exp(-jnp.abs(x)))


def _rope_slice(x, cos_t, sin_up, sin_dn):
    return (x * cos_t + pltpu.roll(x, LANES - ROT_DIM // 2, axis=1) * sin_up
            + pltpu.roll(x, ROT_DIM // 2, axis=1) * sin_dn)


def _proj_ssd_kernel(x_ref, wz_ref, wxbc_ref, wdt_ref, dtb_ref, z_ref, xbc_ref, dt_ref):
    xb = x_ref[...].astype(BF16)
    cw = 512
    for c in range(SSD_INNER // cw):
        z_ref[:, c * cw:(c + 1) * cw] = _dot(xb, wz_ref[:, c * cw:(c + 1) * cw])
    for c in range(SSD_XBC // cw):
        xbc_ref[:, c * cw:(c + 1) * cw] = _dot(xb, wxbc_ref[:, c * cw:(c + 1) * cw])
    dt_ref[...] = _softplus(_dot(xb, wdt_ref[...]) + dtb_ref[...])


def _proj_ssd(x, wz, wxbc, wdt, dtb, *, tm):
    m = x.shape[0]

    def row(n):
        return pl.BlockSpec((tm, n), lambda i: (i, 0))

    return pl.pallas_call(
        _proj_ssd_kernel,
        grid=(m // tm,),
        in_specs=[row(D_MODEL), _resident(wz.shape), _resident(wxbc.shape), _resident(wdt.shape),
                  _resident(dtb.shape)],
        out_specs=[row(SSD_INNER), row(SSD_XBC), row(SSD_GROUPS * LANES)],
        out_shape=[jax.ShapeDtypeStruct((m, SSD_INNER), F32), jax.ShapeDtypeStruct((m, SSD_XBC), F32),
                   jax.ShapeDtypeStruct((m, SSD_GROUPS * LANES), F32)],
        compiler_params=_cparams(1),
    )(x, wz, wxbc, wdt, dtb)


def _proj_ssd_conv_kernel(x_ref, wz_ref, wxbc_ref, wdt_ref, dtb_ref, cw_ref, cb_ref,
                          zs_ref, xs_ref, bc_ref, dt_ref, tail_ref, win_ref, carry_ref, *, tiles_per_seq):
    tm = x_ref.shape[0]
    split = (tm // 2 + 15) // 16 * 16
    first = 8 - (SSD_CONV - 1)
    xb = x_ref[...].astype(BF16)

    @pl.when(pl.program_id(0) % tiles_per_seq == 0)
    def _():
        carry_ref[...] = jnp.zeros_like(carry_ref)

    cw = 512
    n_xbc, n_z = SSD_XBC // cw, SSD_INNER // cw

    def project(c):
        cols = slice(c * cw, (c + 1) * cw)
        win_ref[c % 2, 0:8, :] = carry_ref[:, cols]
        win_ref[c % 2, 8:8 + tm, :] = _dot(xb, wxbc_ref[:, cols])

    def conv(c):
        win = win_ref.at[c % 2]
        for j in range(cw // LANES):
            lanes = slice(j * LANES, (j + 1) * LANES)
            col = c * cw + j * LANES
            for r0, n in ((0, split), (split, tm - split)):
                taps = [win[first + k + r0:first + k + r0 + n, lanes] for k in range(SSD_CONV)]
                act = _conv_silu(taps, cw_ref[:, :, col:col + LANES], cb_ref[:, col:col + LANES])
                if col < SSD_INNER:
                    xs_ref[r0:r0 + n, col:col + LANES] = act
                else:
                    bc_ref[r0:r0 + n, col - SSD_INNER:col - SSD_INNER + LANES] = act.astype(BF16)
        tail = win[tm:tm + 8, :]
        tail_ref[:, c * cw:(c + 1) * cw] = tail
        carry_ref[:, c * cw:(c + 1) * cw] = tail

    project(0)
    for c in range(n_xbc):
        if c + 1 < n_xbc:
            project(c + 1)
        if c < n_z:
            zs_ref[:, c * cw:(c + 1) * cw] = _silu(_dot(xb, wz_ref[:, c * cw:(c + 1) * cw])).astype(BF16)
        conv(c)
    dt_ref[...] = _softplus(_dot(xb, wdt_ref[...]) + dtb_ref[...])


def _proj_ssd_conv(x, wz, wxbc, wdt, dtb, conv_w, conv_b, *, tm, tiles_per_seq):
    m = x.shape[0]
    assert tm % 16 == 0

    def row(n):
        return pl.BlockSpec((tm, n), lambda i: (i, 0))

    consts = (wz, wxbc, wdt, dtb, conv_w, conv_b)
    return pl.pallas_call(
        functools.partial(_proj_ssd_conv_kernel, tiles_per_seq=tiles_per_seq),
        grid=(m // tm,),
        in_specs=[row(D_MODEL)] + [_resident(c.shape) for c in consts],
        out_specs=[row(SSD_INNER), row(SSD_INNER), row(2 * SSD_GROUPS * SSD_STATE), row(SSD_GROUPS * LANES),
                   pl.BlockSpec((8, SSD_XBC), lambda i: (i // tiles_per_seq, 0))],
        out_shape=[jax.ShapeDtypeStruct((m, SSD_INNER), BF16), jax.ShapeDtypeStruct((m, SSD_INNER), F32),
                   jax.ShapeDtypeStruct((m, 2 * SSD_GROUPS * SSD_STATE), BF16),
                   jax.ShapeDtypeStruct((m, SSD_GROUPS * LANES), F32),
                   jax.ShapeDtypeStruct((m // (tm * tiles_per_seq) * 8, SSD_XBC), F32)],
        scratch_shapes=[pltpu.VMEM((2, 8 + tm, 512), F32), pltpu.VMEM((8, SSD_XBC), F32)],
        compiler_params=_cparams(1),
    )(x, *consts)


def _proj_mix_kernel(x_ref, cos_ref, sup_ref, sdn_ref, wq_ref, wk_ref, wv_ref, wga_ref, wgb_ref,
                     q_ref, k_ref, v_ref, u_ref):
    xb = x_ref[...].astype(BF16)
    cos_t, sin_up, sin_dn = cos_ref[...], sup_ref[...], sdn_ref[...]
    cw = 256
    for c in range(DA_WIDTH // cw):
        lo = c * cw
        qc = _dot(xb, wq_ref[:, lo:lo + cw])
        kc = _dot(xb, wk_ref[:, lo:lo + cw])
        for s in range(cw // HEAD_W):
            sl = slice(s * HEAD_W, (s + 1) * HEAD_W)
            dst = slice(lo + s * HEAD_W, lo + (s + 1) * HEAD_W)
            q_ref[:, dst] = (_rope_slice(qc[:, sl], cos_t, sin_up, sin_dn)
                             * (DA_HEAD_DIM ** -0.5)).astype(q_ref.dtype)
            k_ref[:, dst] = _rope_slice(kc[:, sl], cos_t, sin_up, sin_dn)
        v_ref[:, lo:lo + cw] = _dot(xb, wv_ref[:, lo:lo + cw])
        ga = _dot(xb, wga_ref[:, lo:lo + cw])
        gb = _dot(xb, wgb_ref[:, lo:lo + cw])
        u_ref[:, lo:lo + cw] = ga * _sigmoid(gb)


def _proj_mix(x, tables, wq, wk, wv, wga, wgb, *, tm, q_dtype):
    m = x.shape[0]
    n_tab = tables[0].shape[0] // tm
    row = pl.BlockSpec((tm, D_MODEL), lambda i: (i, 0))
    tab = pl.BlockSpec((tm, LANES), lambda i: (i % n_tab, 0))
    w = _resident(wq.shape)
    return pl.pallas_call(
        _proj_mix_kernel,
        grid=(m // tm,),
        in_specs=[row, tab, tab, tab, w, w, w, w, w],
        out_specs=[row, row, row, row],
        out_shape=[jax.ShapeDtypeStruct((m, D_MODEL), q_dtype)] + [jax.ShapeDtypeStruct((m, D_MODEL), F32)] * 3,
        compiler_params=_cparams(1),
    )(x, *tables, wq, wk, wv, wga, wgb)


def _merge_kernel(h_ref, ys_ref, yo_ref, yc_ref, wgate_ref, bgate_ref, wssd_ref, wda_ref, wcc_ref, wout_ref,
                  g_ref, b_ref, o_ref, *, alpha):
    h = h_ref[...]
    hb = h.astype(BF16)
    merged = None
    for i, (y_ref, w_ref) in enumerate(((ys_ref, wssd_ref), (yo_ref, wda_ref), (yc_ref, wcc_ref))):
        sl = slice(i * D_MODEL, (i + 1) * D_MODEL)
        gate = _sigmoid(_dot(hb, wgate_ref[:, sl]) + bgate_ref[:, sl])
        term = gate * _dot(y_ref[...].astype(BF16), w_ref[...])
        merged = term if merged is None else merged + term
    mix = _dot(merged.astype(BF16), wout_ref[...])
    o_ref[...] = _layer_norm(alpha * h + mix, g_ref[...], b_ref[...])


def _merge(h, ys, yo, yc, wgate, bgate, wssd, wda, wcc, wout, g, b, *, tm, alpha):
    m = h.shape[0]

    def row(n):
        return pl.BlockSpec((tm, n), lambda i: (i, 0))

    consts = (wgate, bgate, wssd, wda, wcc, wout, g, b)
    return pl.pallas_call(
        functools.partial(_merge_kernel, alpha=alpha),
        grid=(m // tm,),
        in_specs=[row(D_MODEL), row(SSD_INNER), row(DA_WIDTH), row(CC_CH)] + [_resident(c.shape) for c in consts],
        out_specs=row(D_MODEL),
        out_shape=jax.ShapeDtypeStruct((m, D_MODEL), F32),
        compiler_params=_cparams(1),
    )(h, ys, yo, yc, *consts)


def _pair_cols(x, h0, h1, lane_lo):
    shape = (x.shape[0], LANES)
    return jnp.where(lane_lo[:x.shape[0]], jnp.broadcast_to(x[:, h0:h0 + 1], shape),
                     jnp.broadcast_to(x[:, h1:h1 + 1], shape))


def _ssd_intra(xs, bc, cc, dt, acol, mask, lane_lo):
    arow = acol.T
    bb = bc.astype(BF16)
    cb = _dot_nt(cc.astype(BF16), bb)
    y_in, xdts, aces = [], [], []
    for j in range(HEADS_PER_GROUP // 2):
        h0, h1 = 2 * j, 2 * j + 1
        gs, wide = [], []
        for h in (h0, h1):
            a_h = jnp.broadcast_to(acol[:, h:h + 1], acol.shape)
            wide.append(a_h)
            gs.append((cb * jnp.exp(jnp.where(mask, a_h - arow[h:h + 1, :], -jnp.inf))).astype(BF16))
        gp = jnp.concatenate(gs, axis=1)
        xdt = xs[:, j * LANES:(j + 1) * LANES] * _pair_cols(dt, h0, h1, lane_lo)
        zero = jnp.zeros_like(xdt)
        xbd = jnp.concatenate([jnp.where(lane_lo, xdt, zero), jnp.where(lane_lo, zero, xdt)], axis=0).astype(BF16)
        y_in.append(_dot(gp, xbd))
        xdts.append(xdt)
        aces.append(jnp.where(lane_lo, wide[0], wide[1]))
    return y_in, xdts, aces, bb


def _ssd_finish(ys, xs, gate, d_row, nw_row):
    gated, ss = [], None
    for j, y in enumerate(ys):
        sl = slice(j * LANES, (j + 1) * LANES)
        yz = (y + d_row[:, sl] * xs[:, sl]) * gate[:, sl]
        gated.append(yz)
        ss = yz * yz if ss is None else ss + yz * yz
    rs = lax.rsqrt(jnp.sum(ss, axis=1, keepdims=True) * (1.0 / GROUP_CH) + LN_EPS)
    return [(yz * rs * nw_row[:, j * LANES:(j + 1) * LANES]).astype(BF16) for j, yz in enumerate(gated)]


def _conv_silu(taps, w_ref, b_ref):
    rows, ch = taps[0].shape
    acc = jnp.broadcast_to(b_ref[...][None], (rows // 8, 8, ch))
    for k in range(SSD_CONV):
        acc = acc + w_ref[k][None] * taps[k].reshape(rows // 8, 8, ch)
    return _silu(acc).reshape(rows, ch)


def _ssd_prompt_kernel(xs_ref, bm_ref, cm_ref, zs_ref, dt_ref, a_ref, d_ref, nw_ref, y_ref, hfin_ref, state_ref):
    L = SSD_CHUNK
    rows = lax.broadcasted_iota(jnp.int32, (L, LANES), 0)
    cols = lax.broadcasted_iota(jnp.int32, (L, LANES), 1)
    causal = rows >= cols
    tri = causal.astype(F32)
    lane_lo = cols < SSD_HEAD_DIM
    row_lo = rows < SSD_HEAD_DIM
    state_ref[...] = jnp.zeros_like(state_ref)

    def local_part(r0, meta):
        xs = xs_ref[pl.ds(r0, L), :]
        bc = bm_ref[pl.ds(r0, L), :]
        cc = cm_ref[pl.ds(r0, L), :]
        dt = dt_ref[pl.ds(r0, L), :]
        if meta:
            dt = jnp.where(rows < N_META, dt, 0.0)
        acol = _dot_exact(tri, dt * a_ref[...])
        alast = acol[L - 1:L, :]
        y_in, xdts, aces, bb = _ssd_intra(xs, bc, cc, dt, acol, causal, lane_lo)
        pairs = []
        for j in range(HEADS_PER_GROUP // 2):
            h0, h1 = 2 * j, 2 * j + 1
            ace = aces[j]
            ale = jnp.where(lane_lo[0:1], jnp.broadcast_to(alast[:, h0:h0 + 1], (1, LANES)),
                            jnp.broadcast_to(alast[:, h1:h1 + 1], (1, LANES)))
            xw = xdts[j] * jnp.exp(ale - ace)
            upd = _dot(xw.T.astype(BF16), bb)
            dec = jnp.where(row_lo, jnp.broadcast_to(jnp.exp(alast[:, h0:h0 + 1]), (L, LANES)),
                            jnp.broadcast_to(jnp.exp(alast[:, h1:h1 + 1]), (L, LANES)))
            pairs.append((y_in[j], jnp.exp(ace), upd, dec))
        return xs, cc.astype(BF16), pairs

    def state_part(r0, xs, cb16, pairs):
        ys = []
        for j, (y_loc, eace, upd, dec) in enumerate(pairs):
            st = state_ref[j * LANES:(j + 1) * LANES, :]
            ys.append(y_loc + _dot_nt(cb16, st.astype(BF16)) * eace)
            state_ref[j * LANES:(j + 1) * LANES, :] = st * dec + upd
        outs = _ssd_finish(ys, xs, zs_ref[pl.ds(r0, L), :].astype(F32), d_ref[...], nw_ref[...])
        for j, o in enumerate(outs):
            y_ref[pl.ds(r0, L), j * LANES:(j + 1) * LANES] = o

    state_part(0, *local_part(0, True))

    def body(c, carry):
        r_a = pl.multiple_of(N_META + 2 * c * L, 16)
        r_b = pl.multiple_of(N_META + (2 * c + 1) * L, 16)
        loc_a = local_part(r_a, False)
        loc_b = local_part(r_b, False)
        state_part(r_a, *loc_a)
        state_part(r_b, *loc_b)
        return carry

    n_chunks = (y_ref.shape[0] - N_META) // L
    assert n_chunks % 2 == 0
    lax.fori_loop(0, n_chunks // 2, body, 0)
    hfin_ref[...] = state_ref[...]


def _ssd_prompt(xs, bcm, zs, dt, a4, d_e, nw, *, batch, seq):
    def seq_block(width, col0=0):
        return pl.BlockSpec((seq, width), lambda b, g: (b, g + col0))

    def par_block(rows, width):
        return pl.BlockSpec((rows, width), lambda b, g: (0, g))

    return pl.pallas_call(
        _ssd_prompt_kernel,
        grid=(batch, SSD_GROUPS),
        in_specs=[seq_block(GROUP_CH), seq_block(SSD_STATE), seq_block(SSD_STATE, SSD_GROUPS),
                  seq_block(GROUP_CH), seq_block(LANES),
                  par_block(1, LANES), par_block(1, GROUP_CH), par_block(1, GROUP_CH)],
        out_specs=[seq_block(GROUP_CH), pl.BlockSpec((None, GROUP_CH, SSD_STATE), lambda b, g: (b, g, 0))],
        out_shape=[jax.ShapeDtypeStruct((batch * seq, SSD_INNER), BF16),
                   jax.ShapeDtypeStruct((batch, SSD_INNER, SSD_STATE), F32)],
        scratch_shapes=[pltpu.VMEM((GROUP_CH, SSD_STATE), F32)],
        compiler_params=_cparams(2),
    )(xs, bcm, bcm, zs, dt, a4, d_e, nw)


def _ssd_sample_kernel(xs_ref, bm_ref, cm_ref, px_ref, pb_ref, pc_ref, z_ref, dt_ref, h0_ref,
                       wx_ref, wb_ref, wc_ref, bx_ref, bb_ref, bc_ref, a_ref, d_ref, nw_ref, *rest, n_prev):
    y_ref, h1_ref, ext_ref, flat_ref, xwt_ref, yst_ref = rest[-6:]
    if n_prev:
        h1_ref[0:n_prev] = rest[0][...]
    R = SAMPLE_ROWS
    rows = lax.broadcasted_iota(jnp.int32, (R, LANES), 0)
    cols = lax.broadcasted_iota(jnp.int32, (R, LANES), 1)
    same = (rows >> 3) == (cols >> 3)
    causal = same & (rows >= cols)
    lane_lo = cols < SSD_HEAD_DIM
    srcs = ((xs_ref, px_ref, 0, GROUP_CH), (bm_ref, pb_ref, GROUP_CH, GROUP_CH + SSD_STATE),
            (cm_ref, pc_ref, GROUP_CH + SSD_STATE, GROUP_CH + 2 * SSD_STATE))
    for cur, prev, lo, hi in srcs:
        ext_ref[:, 0:8, lo:hi] = prev[...]
        ext_ref[:, 8:16, lo:hi] = cur[...]
    first = 8 - (SSD_CONV - 1)
    for k in range(SSD_CONV):
        flat_ref[k] = ext_ref[:, first + k:first + k + 8, :].reshape(R, GROUP_CH + 2 * SSD_STATE)

    def conv(lo, hi, w_ref, b_ref):
        return _conv_silu([flat_ref[k, :, lo:hi] for k in range(SSD_CONV)], w_ref, b_ref)

    xs = conv(0, GROUP_CH, wx_ref, bx_ref)
    bc = conv(GROUP_CH, GROUP_CH + SSD_STATE, wb_ref, bb_ref)
    cc = conv(GROUP_CH + SSD_STATE, GROUP_CH + 2 * SSD_STATE, wc_ref, bc_ref)
    dt = dt_ref[...]
    dta = dt * a_ref[...]
    acol = _dot_exact(causal.astype(F32), dta)
    total = _dot_exact(same.astype(F32), dta)
    y_in, xdts, aces, _ = _ssd_intra(xs, bc, cc, dt, acol, causal, lane_lo)
    for j in range(HEADS_PER_GROUP // 2):
        xw = xdts[j] * jnp.exp(_pair_cols(total, 2 * j, 2 * j + 1, lane_lo) - aces[j])
        xwt_ref[j * LANES:(j + 1) * LANES, :] = xw.T.astype(BF16)
    xwt = xwt_ref[...]
    for i in range(SAMPLE_BLOCK):
        r = slice(8 * i, 8 * i + 8)
        st = h0_ref[i]
        c16 = jnp.concatenate([cc[r], jnp.zeros((8, SSD_STATE), F32)], axis=0).astype(BF16)
        yst_ref[r, :] = _dot_nt(c16, st.astype(BF16))[0:8]
        b_i = jnp.where((rows >> 3) == i, bc, 0.0).astype(BF16)
        upd = _dot(xwt, b_i)
        tot_i = total[8 * i:8 * i + 1, :]
        for h in range(HEADS_PER_GROUP):
            hs = slice(h * SSD_HEAD_DIM, (h + 1) * SSD_HEAD_DIM)
            h1_ref[n_prev, i, hs, :] = st[hs] * jnp.exp(tot_i[:, h:h + 1]) + upd[hs]
    ys = [y_in[j] + yst_ref[:, j * LANES:(j + 1) * LANES] * jnp.exp(aces[j])
          for j in range(HEADS_PER_GROUP // 2)]
    outs = _ssd_finish(ys, xs, _silu(z_ref[...]), d_ref[...], nw_ref[...])
    for j, o in enumerate(outs):
        y_ref[:, j * LANES:(j + 1) * LANES] = o


def _ssd_sample(xbc3, prev4, z, dt, h0_all, conv_w, conv_b, a4, d_e, nw, *, layer, prev_h1=None):
    n_req = xbc3.shape[0]
    n_prev = 0 if prev_h1 is None else prev_h1.shape[0]
    g_b = SSD_INNER // SSD_STATE
    R = SAMPLE_ROWS
    width = GROUP_CH + 2 * SSD_STATE

    def tok_block(w, col0=0):
        return pl.BlockSpec((SAMPLE_BLOCK, 8, w), lambda i, g: (i, 0, g + col0))

    def prev_block(w, col0=0):
        return pl.BlockSpec((None, SAMPLE_BLOCK, 8, w), lambda i, g: (layer, i, 0, g + col0))

    def row_block(w):
        return pl.BlockSpec((R, w), lambda i, g: (i, g))

    def par_block(rows, w, col0=0):
        return pl.BlockSpec((rows, w), lambda i, g: (0, g + col0))

    def tap_block(w, col0=0):
        return pl.BlockSpec((SSD_CONV, 8, w), lambda i, g: (0, 0, g + col0))

    state = pl.BlockSpec((None, SAMPLE_BLOCK, GROUP_CH, SSD_STATE), lambda i, g: (layer, i, g, 0))
    toks = [tok_block(GROUP_CH), tok_block(SSD_STATE, g_b), tok_block(SSD_STATE, g_b + SSD_GROUPS)]
    prevs = [prev_block(GROUP_CH), prev_block(SSD_STATE, g_b), prev_block(SSD_STATE, g_b + SSD_GROUPS)]
    in_specs = toks + prevs + [row_block(GROUP_CH), row_block(LANES), state,
                               tap_block(GROUP_CH), tap_block(SSD_STATE, g_b),
                               tap_block(SSD_STATE, g_b + SSD_GROUPS),
                               par_block(8, GROUP_CH), par_block(8, SSD_STATE, g_b),
                               par_block(8, SSD_STATE, g_b + SSD_GROUPS),
                               par_block(1, LANES), par_block(1, GROUP_CH), par_block(1, GROUP_CH)]
    args = [xbc3, xbc3, xbc3, prev4, prev4, prev4, z, dt, h0_all, conv_w, conv_w, conv_w, conv_b, conv_b, conv_b,
            a4, d_e, nw]
    def stack_block(n):
        return pl.BlockSpec((n, SAMPLE_BLOCK, GROUP_CH, SSD_STATE), lambda i, g: (0, i, g, 0))

    if n_prev:
        in_specs = in_specs + [stack_block(n_prev)]
        args = args + [prev_h1]
    return pl.pallas_call(
        functools.partial(_ssd_sample_kernel, n_prev=n_prev),
        grid=(n_req // SAMPLE_BLOCK, SSD_GROUPS),
        in_specs=in_specs,
        out_specs=[row_block(GROUP_CH), stack_block(n_prev + 1)],
        out_shape=[jax.ShapeDtypeStruct((n_req * 8, SSD_INNER), BF16),
                   jax.ShapeDtypeStruct((n_prev + 1, n_req, SSD_INNER, SSD_STATE), F32)],
        scratch_shapes=[pltpu.VMEM((SAMPLE_BLOCK, 16, width), F32), pltpu.VMEM((SSD_CONV, R, width), F32),
                        pltpu.VMEM((GROUP_CH, R), BF16), pltpu.VMEM((R, GROUP_CH), F32)],
        compiler_params=_cparams(2),
    )(*args)


def _sub_ln(o, w_row, post_scale):
    return o * lax.rsqrt(jnp.mean(o * o, axis=-1, keepdims=True) + LN_EPS) * w_row * post_scale


def _split_components(q):
    lane_lo = lax.broadcasted_iota(jnp.int32, q.shape, 1) < DA_HEAD_DIM
    zero = jnp.zeros_like(q)
    return jnp.concatenate([jnp.where(lane_lo, q, zero), jnp.where(lane_lo, zero, q)], axis=0)


def _attn_prompt_kernel(lam_ref, q_ref, k_ref, v_ref, w_ref, *rest, n_prev, post_scale):
    kb_ref, vb_ref = rest[-2:]
    if n_prev:
        kp_ref, vp_ref, o_ref, kall_ref, vall_ref = rest[:5]
        kall_ref[0:n_prev] = kp_ref[...]
        vall_ref[0:n_prev] = vp_ref[...]
        kall_ref[n_prev] = k_ref[...]
        vall_ref[n_prev] = v_ref[...]
    else:
        o_ref = rest[0]
    lam = lam_ref[0]
    kb_ref[...] = k_ref[...].astype(BF16)
    vb_ref[:, 0:HEAD_W] = v_ref[...].astype(BF16)
    vb_ref[:, HEAD_W:2 * HEAD_W] = jnp.ones((vb_ref.shape[0], HEAD_W), BF16)
    k_meta = kb_ref[0:LANES]
    v_meta = vb_ref[0:LANES]
    w_row = w_ref[...]

    def attend(r0, n_q, parts):
        m = None
        for s, _ in parts:
            pm = jnp.max(s, axis=1, keepdims=True)
            m = pm if m is None else jnp.maximum(m, pm)
        acc = None
        for s, vals in parts:
            p = jnp.exp((s - m).astype(BF16))
            term = _dot(p, vals)
            acc = term if acc is None else acc + term
        on = acc[:, 0:HEAD_W] * (1.0 / acc[:, HEAD_W:HEAD_W + 1])
        o = on[0:n_q] - lam * on[n_q:2 * n_q]
        o_ref[r0:r0 + n_q, :] = _sub_ln(o, w_row, post_scale).astype(o_ref.dtype)

    q2 = _split_components(q_ref[0:N_META])
    s = _dot_nt(q2, k_meta)
    qpos = lax.broadcasted_iota(jnp.int32, s.shape, 0) & (N_META - 1)
    kpos = lax.broadcasted_iota(jnp.int32, s.shape, 1)
    attend(0, N_META, [(jnp.where(kpos <= qpos, s, -jnp.inf), v_meta)])

    n_blk = (q_ref.shape[0] - N_META) // Q_BLOCK
    blk_shape = (2 * Q_BLOCK, Q_BLOCK)
    kcol = lax.broadcasted_iota(jnp.int32, blk_shape, 1)
    meta_ok = lax.broadcasted_iota(jnp.int32, (2 * Q_BLOCK, LANES), 1) < N_META
    diag_ok = kcol <= (lax.broadcasted_iota(jnp.int32, blk_shape, 0) & (Q_BLOCK - 1))

    def scores(i):
        r0 = N_META + i * Q_BLOCK
        q2 = _split_components(q_ref[r0:r0 + Q_BLOCK])
        parts = [(jnp.where(meta_ok, _dot_nt(q2, k_meta), -jnp.inf), v_meta)]
        if i > 0:
            parts.append((_dot_nt(q2, kb_ref[N_META:r0]), vb_ref[N_META:r0]))
        parts.append((jnp.where(diag_ok, _dot_nt(q2, kb_ref[r0:r0 + Q_BLOCK]), -jnp.inf),
                      vb_ref[r0:r0 + Q_BLOCK]))
        return parts

    nxt = scores(0)
    for i in range(n_blk):
        cur, nxt = nxt, (scores(i + 1) if i + 1 < n_blk else None)
        attend(N_META + i * Q_BLOCK, Q_BLOCK, cur)


def _attn_prompt(lam, q, k, v, w_row, *, batch, seq, post_scale, prev_kv=None):
    blk = pl.BlockSpec((seq, HEAD_W), lambda b, h: (b, h))
    in_specs = [pl.BlockSpec(memory_space=pltpu.SMEM), blk, blk, blk, _resident(w_row.shape)]
    args = [lam, q, k, v, w_row]
    out_specs = [blk]
    out_shape = [jax.ShapeDtypeStruct((batch * seq, DA_WIDTH), BF16)]
    n_prev = 0
    if prev_kv is not None:
        n_prev = prev_kv[0].shape[0]
        in_specs += [pl.BlockSpec((n_prev, seq, HEAD_W), lambda b, h: (0, b, h))] * 2
        args += list(prev_kv)
        out_specs += [pl.BlockSpec((n_prev + 1, seq, HEAD_W), lambda b, h: (0, b, h))] * 2
        out_shape += [jax.ShapeDtypeStruct((n_prev + 1, batch * seq, DA_WIDTH), F32)] * 2
    return pl.pallas_call(
        functools.partial(_attn_prompt_kernel, n_prev=n_prev, post_scale=post_scale),
        grid=(batch, DA_HEADS),
        in_specs=in_specs,
        out_specs=out_specs,
        out_shape=out_shape,
        scratch_shapes=[pltpu.VMEM((seq, HEAD_W), BF16), pltpu.VMEM((seq, 2 * HEAD_W), BF16)],
        compiler_params=_cparams(2),
    )(*args)


def _attn_sample_kernel(pt_ref, lam_ref, q_ref, kn_ref, vn_ref, w_ref, *refs, n_pages, post_scale):
    del pt_ref
    k_pages = refs[:n_pages]
    v_pages = refs[n_pages:2 * n_pages]
    o_ref, s_ref = refs[2 * n_pages], refs[2 * n_pages + 1]
    lam = lam_ref[0]
    page_rows = PAGE_SIZE * DA_HEADS
    new_rows = 8 * DA_HEADS
    q = q_ref[...]
    lane = lax.broadcasted_iota(jnp.int32, (8, HEAD_W), 1)
    pieces = []
    for c in range(2):
        for h in range(DA_HEADS):
            qh = q[:, h * HEAD_W:(h + 1) * HEAD_W]
            keep = lane < DA_HEAD_DIM if c == 0 else lane >= DA_HEAD_DIM
            pieces.append(jnp.where(keep, qh, 0.0))
    qmat = jnp.concatenate(pieces, axis=0).astype(BF16)
    row_head = (lax.broadcasted_iota(jnp.int32, (LANES, page_rows), 0) >> 3) & (DA_HEADS - 1)
    col_head = lax.broadcasted_iota(jnp.int32, (LANES, page_rows), 1) & (DA_HEADS - 1)
    head_ok = row_head == col_head
    s_new = _dot_nt(qmat, kn_ref[...].astype(BF16))
    r_i = lax.broadcasted_iota(jnp.int32, s_new.shape, 0)
    c_i = lax.broadcasted_iota(jnp.int32, s_new.shape, 1)
    ok_new = (((r_i >> 3) & (DA_HEADS - 1)) == (c_i & (DA_HEADS - 1))) & ((c_i >> 3) <= (r_i & 7))
    s_new = jnp.where(ok_new, s_new, -jnp.inf)
    m = jnp.max(s_new, axis=1, keepdims=True)
    for p in range(n_pages):
        s = _dot_nt(qmat, k_pages[p][...].astype(BF16))
        s = jnp.where(head_ok, s, -jnp.inf)
        s_ref[:, p * page_rows:(p + 1) * page_rows] = s
        m = jnp.maximum(m, jnp.max(s, axis=1, keepdims=True))
    p_new = jnp.exp(s_new - m)
    den = jnp.sum(p_new, axis=1, keepdims=True)
    acc = _dot(p_new.astype(BF16), vn_ref[...].astype(BF16))
    for p in range(n_pages):
        e = jnp.exp(s_ref[:, p * page_rows:(p + 1) * page_rows] - m)
        den = den + jnp.sum(e, axis=1, keepdims=True)
        acc = acc + _dot(e.astype(BF16), v_pages[p][...].astype(BF16))
    on = acc * (1.0 / den)
    half = LANES // 2
    o = on[0:half] - lam * on[half:LANES]
    w_row = w_ref[...]
    for h in range(DA_HEADS):
        o_ref[:, h * HEAD_W:(h + 1) * HEAD_W] = _sub_ln(o[h * 8:(h + 1) * 8], w_row, post_scale)


def _attn_sample(page_table_flat, lam, q, k_new, v_new, w_row, cache_k, cache_v, *, layer, post_scale):
    n_req = q.shape[0] // 8
    n_pages = page_table_flat.shape[0] // n_req
    page_rows = PAGE_SIZE * DA_HEADS

    def page_spec(j):
        return pl.BlockSpec((None, None, page_rows, HEAD_W), lambda b, pt: (layer, pt[b * n_pages + j], 0, 0))

    tok = pl.BlockSpec((8, DA_WIDTH), lambda b, pt: (b, 0))
    new = pl.BlockSpec((None, 8 * DA_HEADS, HEAD_W), lambda b, pt: (b, 0, 0))
    pages = [page_spec(j) for j in range(n_pages)]
    grid_spec = pltpu.PrefetchScalarGridSpec(
        num_scalar_prefetch=1,
        grid=(n_req,),
        in_specs=[pl.BlockSpec(memory_space=pltpu.SMEM), tok, new, new,
                  pl.BlockSpec(w_row.shape, lambda b, pt: (0, 0))] + pages + pages,
        out_specs=tok,
        scratch_shapes=[pltpu.VMEM((LANES, n_pages * page_rows), F32)],
    )
    return pl.pallas_call(
        functools.partial(_attn_sample_kernel, n_pages=n_pages, post_scale=post_scale),
        grid_spec=grid_spec,
        out_shape=jax.ShapeDtypeStruct((n_req * 8, DA_WIDTH), F32),
        compiler_params=_cparams(1),
    )(page_table_flat, lam, q, k_new, v_new, w_row, *([cache_k] * n_pages), *([cache_v] * n_pages))


CC_HIST = 32
CC_ROWS = 48


def _cc_taps(load_window, w_ref, b_ref, lanes, rows):
    first = CC_HIST - (CC_KERNEL - 1)
    nt = rows // 8
    acc = jnp.broadcast_to(b_ref[:, lanes][None], (nt, 8, LANES))
    for phase in range(8):
        n_tap = (CC_KERNEL - 1 - phase) // 8 + 1
        span = rows + 8 * (n_tap - 1)
        win = load_window(first + phase, span).reshape(span // 8, 8, LANES)
        for a in range(n_tap):
            acc = acc + w_ref[8 * a + phase, :, lanes][None] * win[a:a + nt]
    return acc.reshape(rows, LANES)


def _cc_prompt_kernel(u_ref, w_ref, b_ref, g_ref, beta_ref, o_ref, win_ref, conv_ref):
    def body(c, carry):
        r0 = pl.multiple_of(c * CC_ROWS, 16)
        win_ref[CC_HIST:CC_HIST + CC_ROWS, :] = u_ref[pl.ds(r0, CC_ROWS), :]

        @pl.when(c == 0)
        def _():
            win_ref[0:CC_HIST, :] = jnp.zeros((CC_HIST, CC_CH), F32)

        @pl.when(c > 0)
        def _():
            win_ref[0:CC_HIST, :] = u_ref[pl.ds(pl.multiple_of(jnp.maximum(r0 - CC_HIST, 0), 16), CC_HIST), :]

        for lb in range(CC_CH // LANES):
            lanes = slice(lb * LANES, (lb + 1) * LANES)
            conv_ref[:, lanes] = _cc_taps(lambda start, span: win_ref[start:start + span, lanes],
                                          w_ref, b_ref, lanes, CC_ROWS)
        o_ref[pl.ds(r0, CC_ROWS), :] = _silu(_layer_norm(conv_ref[...], g_ref[...],
                                                         beta_ref[...])).astype(o_ref.dtype)
        return carry

    lax.fori_loop(0, u_ref.shape[0] // CC_ROWS, body, 0)


def _cc_prompt(u, w, b, g, beta, *, batch, seq):
    blk = pl.BlockSpec((seq, CC_CH), lambda i: (i, 0))
    return pl.pallas_call(
        _cc_prompt_kernel,
        grid=(batch,),
        in_specs=[blk] + [_resident(c.shape) for c in (w, b, g, beta)],
        out_specs=blk,
        out_shape=jax.ShapeDtypeStruct((batch * seq, CC_CH), BF16),
        scratch_shapes=[pltpu.VMEM((CC_HIST + CC_ROWS, CC_CH), F32), pltpu.VMEM((CC_ROWS, CC_CH), F32)],
        compiler_params=_cparams(1),
    )(u, w, b, g, beta)


def _cc_sample_kernel(u_ref, prev_ref, w_ref, b_ref, g_ref, beta_ref, o_ref, ext_ref):
    ext_ref[:, 0:CC_HIST, :] = prev_ref[...]
    ext_ref[:, CC_HIST:CC_HIST + 8, :] = u_ref[...]
    first = CC_HIST - (CC_KERNEL - 1)
    acc = jnp.broadcast_to(b_ref[...][None], (SAMPLE_BLOCK, 8, CC_CH))
    for k in range(CC_KERNEL):
        acc = acc + w_ref[k][None] * ext_ref[:, first + k:first + k + 8, :]
    acc = acc.reshape(SAMPLE_ROWS, CC_CH)
    o_ref[...] = _silu(_layer_norm(acc, g_ref[...], beta_ref[...])).astype(o_ref.dtype)


def _cc_sample(u3, prev4, w, b, g, beta, *, layer):
    n_req = u3.shape[0]
    return pl.pallas_call(
        _cc_sample_kernel,
        grid=(n_req // SAMPLE_BLOCK,),
        in_specs=[pl.BlockSpec((SAMPLE_BLOCK, 8, CC_CH), lambda i: (i, 0, 0)),
                  pl.BlockSpec((None, SAMPLE_BLOCK, CC_HIST, CC_CH), lambda i: (layer, i, 0, 0))]
        + [_resident(c.shape) for c in (w, b, g, beta)],
        out_specs=pl.BlockSpec((SAMPLE_ROWS, CC_CH), lambda i: (i, 0)),
        out_shape=jax.ShapeDtypeStruct((n_req * 8, CC_CH), BF16),
        scratch_shapes=[pltpu.VMEM((SAMPLE_BLOCK, CC_HIST + 8, CC_CH), F32)],
        compiler_params=_cparams(1),
    )(u3, prev4, w, b, g, beta)


def _rope_tables(pos):
    half = ROT_DIM // 2
    inv_freq = ROPE_THETA ** (-jnp.arange(half, dtype=F32) * 2.0 / ROT_DIM)
    ang = pos.astype(F32)[:, None] * inv_freq[None, :]
    cos, sin = jnp.cos(ang), jnp.sin(ang)
    n = pos.shape[0]
    pad = jnp.zeros((n, DA_HEAD_DIM - ROT_DIM), F32)
    zeros = jnp.zeros((n, half), F32)
    cos_c = jnp.concatenate([cos, cos, pad + 1.0], axis=1)
    up_c = jnp.concatenate([-sin, zeros, pad], axis=1)
    dn_c = jnp.concatenate([zeros, sin, pad], axis=1)
    return tuple(jnp.concatenate([t, t], axis=1) for t in (cos_c, up_c, dn_c))


def _head_lanes(vec):
    v = vec.astype(F32).reshape(SSD_GROUPS, HEADS_PER_GROUP)
    return jnp.pad(v, ((0, 0), (0, LANES - HEADS_PER_GROUP))).reshape(1, SSD_GROUPS * LANES)


def _sublane_repeat(w):
    w = w.astype(F32)
    return jnp.broadcast_to(w[..., None, :], w.shape[:-1] + (8, w.shape[-1]))


def _prepare_layer(l, p):
    pts = [0]
    for s in IN_SIZES:
        pts.append(pts[-1] + s)
    w_in = p['w_in'][l]
    cols = [w_in[:, pts[i]:pts[i + 1]] for i in range(len(IN_SIZES))]
    w_dt = cols[2].reshape(D_MODEL, SSD_GROUPS, HEADS_PER_GROUP)
    w_dt = jnp.pad(w_dt, ((0, 0), (0, 0), (0, LANES - HEADS_PER_GROUP))).reshape(D_MODEL, SSD_GROUPS * LANES)
    row = lambda v: v.astype(F32).reshape(1, -1)
    lam_init = 0.8 - 0.6 * math.exp(-0.3 * l)
    lam = (jnp.exp(jnp.sum(p['da_lambda_q1'][l].astype(F32) * p['da_lambda_k1'][l].astype(F32)))
           - jnp.exp(jnp.sum(p['da_lambda_q2'][l].astype(F32) * p['da_lambda_k2'][l].astype(F32))) + lam_init)
    return dict(
        lam_init=lam_init, lam=lam.reshape(1).astype(F32),
        ln_g=[row(p['ln_g'][l, i]) for i in range(3)], ln_b=[row(p['ln_b'][l, i]) for i in range(3)],
        ff1_gu=p['w_ff1_gu'][l].astype(BF16), ff1_d=p['w_ff1_down'][l].astype(BF16),
        ff2_gu=p['w_ff2_gu'][l].astype(BF16), ff2_d=p['w_ff2_down'][l].astype(BF16),
        w_z=cols[0].astype(BF16), w_xbc=cols[1].astype(BF16), w_dt=w_dt.astype(BF16),
        w_q=cols[3].astype(BF16), w_k=cols[4].astype(BF16), w_v=cols[5].astype(BF16),
        w_ga=cols[6][:, :CC_CH].astype(BF16), w_gb=cols[6][:, CC_CH:].astype(BF16),
        w_gate=cols[7].astype(BF16), b_gate=row(p['b_gate'][l]),
        dt_bias=_head_lanes(p['ssd_dt_bias'][l]), a4=_head_lanes(-jnp.exp(p['ssd_a_log'][l].astype(F32))),
        conv_w=_sublane_repeat(p['ssd_conv_w'][l]), conv_b=_sublane_repeat(p['ssd_conv_b'][l]),
        d_e=jnp.repeat(p['ssd_d'][l].astype(F32), SSD_HEAD_DIM).reshape(1, SSD_INNER),
        norm_w=row(p['ssd_norm_w'][l]), ssd_out=p['ssd_out'][l].astype(BF16),
        subln_w=row(p['da_subln_w'][l]), da_out=p['da_out'][l].astype(BF16),
        cc_w=_sublane_repeat(p['cc_conv_w'][l]), cc_b=_sublane_repeat(p['cc_conv_b'][l]),
        cc_g=row(p['cc_ln_g'][l]), cc_beta=row(p['cc_ln_b'][l]), cc_out=p['cc_out'][l].astype(BF16),
        w_out=p['w_out'][l].astype(BF16),
    )


def _mixer_tail(h1, ys, yo, yc, lp, alpha, tm_merge, tm_ffn):
    h2 = _merge(h1, ys, yo, yc, lp['w_gate'], lp['b_gate'], lp['ssd_out'], lp['da_out'], lp['cc_out'], lp['w_out'],
                lp['ln_g'][1], lp['ln_b'][1], tm=tm_merge, alpha=alpha)
    return _ffn_ln(h2, lp['ff2_gu'], lp['ff2_d'], lp['ln_g'][2], lp['ln_b'][2], tm=tm_ffn, alpha=alpha)


def kernel(x_prompt, x_sample, cache_k, cache_v, state_ssm, state_ssd_conv, state_conf_conv, page_table,
           meta_tokens, ln_g, ln_b, w_ff1_gu, w_ff1_down, w_ff2_gu, w_ff2_down, w_in, b_gate,
           ssd_conv_w, ssd_conv_b, ssd_dt_bias, ssd_a_log, ssd_d, ssd_norm_w, ssd_out,
           da_lambda_q1, da_lambda_k1, da_lambda_q2, da_lambda_k2, da_subln_w, da_out,
           cc_conv_w, cc_conv_b, cc_ln_g, cc_ln_b, cc_out, w_out):
    params = dict(ln_g=ln_g, ln_b=ln_b, w_ff1_gu=w_ff1_gu, w_ff1_down=w_ff1_down, w_ff2_gu=w_ff2_gu,
                  w_ff2_down=w_ff2_down, w_in=w_in, b_gate=b_gate, ssd_conv_w=ssd_conv_w, ssd_conv_b=ssd_conv_b,
                  ssd_dt_bias=ssd_dt_bias, ssd_a_log=ssd_a_log, ssd_d=ssd_d, ssd_norm_w=ssd_norm_w, ssd_out=ssd_out,
                  da_lambda_q1=da_lambda_q1, da_lambda_k1=da_lambda_k1, da_lambda_q2=da_lambda_q2,
                  da_lambda_k2=da_lambda_k2, da_subln_w=da_subln_w, da_out=da_out, cc_conv_w=cc_conv_w,
                  cc_conv_b=cc_conv_b, cc_ln_g=cc_ln_g, cc_ln_b=cc_ln_b, cc_out=cc_out, w_out=w_out)
    depth = w_in.shape[0]
    alpha = (2.0 * depth) ** 0.25
    bp, seq = x_prompt.shape[0], x_prompt.shape[1]
    lp_len = seq + N_META
    ds, dec_len = x_sample.shape[0], x_sample.shape[1]
    assert dec_len == 8 and lp_len % 48 == 0 and ds % SAMPLE_BLOCK == 0
    n_pages = page_table.shape[1]
    past = n_pages * cache_k.shape[2]
    n_pool = cache_k.shape[1]

    tm_p = lp_len // 3
    tm_pm = 384
    tm_s = 512
    assert (bp * lp_len) % tm_pm == 0 and (ds * dec_len) % tm_s == 0

    hp = jnp.concatenate([jnp.broadcast_to(meta_tokens.astype(F32)[None], (bp, N_META, D_MODEL)),
                          x_prompt.astype(F32)], axis=1).reshape(bp * lp_len, D_MODEL)
    hs = x_sample.astype(F32).reshape(ds * dec_len, D_MODEL)
    tab_p = _rope_tables(jnp.arange(lp_len))
    tab_s = tuple(jnp.tile(t, (tm_s // dec_len, 1)) for t in _rope_tables(past + jnp.arange(dec_len)))
    ck = cache_k.reshape(depth, n_pool, PAGE_SIZE * DA_HEADS, HEAD_W)
    cv = cache_v.reshape(depth, n_pool, PAGE_SIZE * DA_HEADS, HEAD_W)
    pt_flat = page_table.reshape(-1).astype(jnp.int32)

    h0_all = state_ssm.astype(F32).reshape(depth, ds, SSD_INNER, SSD_STATE)
    conv_prev4 = jnp.pad(state_ssd_conv.astype(F32), ((0, 0), (0, 0), (8 - (SSD_CONV - 1), 0), (0, 0)))
    cc_prev4 = jnp.pad(state_conf_conv.astype(F32), ((0, 0), (0, 0), (CC_HIST - (CC_KERNEL - 1), 0), (0, 0)))
    kv_p, h1_buf = None, None

    outs = {k: [] for k in ('ks', 'vs', 'hp', 'cp', 'cs', 'up', 'us')}
    for l in range(depth):
        lp = _prepare_layer(l, params)
        post = 1.0 - lp['lam_init']

        h1 = _ffn_ln(hp, lp['ff1_gu'], lp['ff1_d'], lp['ln_g'][0], lp['ln_b'][0], tm=tm_p, alpha=alpha)
        zs, xs, bcm, dt, conv_tail = _proj_ssd_conv(h1, lp['w_z'], lp['w_xbc'], lp['w_dt'], lp['dt_bias'],
                                                     lp['conv_w'], lp['conv_b'], tm=tm_p, tiles_per_seq=lp_len // tm_p)
        q, k, v, u = _proj_mix(h1, tab_p, lp['w_q'], lp['w_k'], lp['w_v'], lp['w_ga'], lp['w_gb'],
                               tm=tm_p, q_dtype=BF16)
        ys, hfin = _ssd_prompt(xs, bcm, zs, dt, lp['a4'], lp['d_e'], lp['norm_w'], batch=bp, seq=lp_len)
        att = _attn_prompt(lp['lam'], q, k, v, lp['subln_w'], batch=bp, seq=lp_len, post_scale=post,
                           prev_kv=kv_p)
        yo = att[0]
        kv_p = (k[None], v[None]) if kv_p is None else (att[1], att[2])
        yc = _cc_prompt(u, lp['cc_w'], lp['cc_b'], lp['cc_g'], lp['cc_beta'], batch=bp, seq=lp_len)
        hp = _mixer_tail(h1, ys, yo, yc, lp, alpha, tm_pm, tm_p)
        outs['hp'].append(hfin.reshape(bp, SSD_HEADS, SSD_HEAD_DIM, SSD_STATE))
        outs['cp'].append(conv_tail.reshape(bp, 8, SSD_XBC)[:, 8 - (SSD_CONV - 1):])
        outs['up'].append(u.reshape(bp, lp_len, CC_CH)[:, lp_len - (CC_KERNEL - 1):])

        h1 = _ffn_ln(hs, lp['ff1_gu'], lp['ff1_d'], lp['ln_g'][0], lp['ln_b'][0], tm=tm_s, alpha=alpha)
        z, xbc, dt = _proj_ssd(h1, lp['w_z'], lp['w_xbc'], lp['w_dt'], lp['dt_bias'], tm=tm_s)
        q, k, v, u = _proj_mix(h1, tab_s, lp['w_q'], lp['w_k'], lp['w_v'], lp['w_ga'], lp['w_gb'],
                               tm=tm_s, q_dtype=F32)
        xbc3 = xbc.reshape(ds, dec_len, SSD_XBC)
        ys, h1_buf = _ssd_sample(xbc3, conv_prev4, z, dt, h0_all, lp['conv_w'], lp['conv_b'], lp['a4'], lp['d_e'],
                                 lp['norm_w'], layer=l, prev_h1=h1_buf)
        yo = _attn_sample(pt_flat, lp['lam'], q, k.reshape(ds, dec_len * DA_HEADS, HEAD_W),
                          v.reshape(ds, dec_len * DA_HEADS, HEAD_W), lp['subln_w'], ck, cv,
                          layer=l, post_scale=post)
        u3 = u.reshape(ds, dec_len, CC_CH)
        yc = _cc_sample(u3, cc_prev4, lp['cc_w'], lp['cc_b'], lp['cc_g'], lp['cc_beta'], layer=l)
        hs = _mixer_tail(h1, ys, yo, yc, lp, alpha, tm_s, tm_s)
        outs['ks'].append(k.reshape(ds, dec_len, DA_HEADS, HEAD_W))
        outs['vs'].append(v.reshape(ds, dec_len, DA_HEADS, HEAD_W))
        outs['cs'].append(xbc3)
        outs['us'].append(u3)

    y_prompt = hp.reshape(bp, lp_len, D_MODEL)[:, N_META:]
    y_sample = hs.reshape(ds, dec_len, D_MODEL)
    st = {k: jnp.stack(v) for k, v in outs.items()}
    k_prompt = kv_p[0].reshape(depth, bp, lp_len, DA_HEADS, HEAD_W)
    v_prompt = kv_p[1].reshape(depth, bp, lp_len, DA_HEADS, HEAD_W)
    ssm_sample = h1_buf.reshape(depth, ds, SSD_HEADS, SSD_HEAD_DIM, SSD_STATE)
    conv_sample = jnp.concatenate([state_ssd_conv.astype(F32), st['cs']], axis=2)[:, :, -(SSD_CONV - 1):]
    cc_sample = jnp.concatenate([state_conf_conv.astype(F32), st['us']], axis=2)[:, :, -(CC_KERNEL - 1):]
    return (y_prompt, y_sample, k_prompt, v_prompt, st['ks'], st['vs'], st['hp'], ssm_sample,
            st['cp'], conv_sample, st['up'], cc_sample)
```

```python
import functools
import math

import jax
import jax.numpy as jnp
from jax import lax
from jax.experimental import pallas as pl
from jax.experimental.pallas import tpu as pltpu

F32 = jnp.float32
BF16 = jnp.bfloat16

D_MODEL = 1024
N_META = 16
SSD_INNER = 2048
SSD_HEAD_DIM = 64
SSD_HEADS = 32
SSD_STATE = 128
SSD_GROUPS = 4
SSD_CONV = 4
SSD_CHUNK = 128
SSD_XBC = SSD_INNER + 2 * SSD_GROUPS * SSD_STATE
GROUP_CH = SSD_INNER // SSD_GROUPS
HEADS_PER_GROUP = SSD_HEADS // SSD_GROUPS
DA_HEAD_DIM = 64
DA_HEADS = 8
DA_WIDTH = 1024
HEAD_W = 2 * DA_HEAD_DIM
ROT_DIM = DA_HEAD_DIM // 4
ROPE_THETA = 500000.0
Q_BLOCK = 256
PAGE_SIZE = 128
CC_CH = 1024
CC_KERNEL = 31
FF_HIDDEN = 2816
N_BRANCH = 3
IN_SIZES = (SSD_INNER, SSD_XBC, SSD_HEADS, DA_WIDTH, DA_WIDTH, DA_WIDTH, 2 * CC_CH, N_BRANCH * D_MODEL)
LN_EPS = 1e-5

VMEM_LIMIT_V7X = 52 * 1024 * 1024
LANES = 128
SAMPLE_BLOCK = 16
SAMPLE_ROWS = SAMPLE_BLOCK * 8


def _cparams(n_axes):
    return pltpu.CompilerParams(dimension_semantics=("arbitrary",) * n_axes,
                                vmem_limit_bytes=VMEM_LIMIT_V7X)


def _resident(shape):
    nd = len(shape)
    return pl.BlockSpec(shape, lambda *_: (0,) * nd, pipeline_mode=pl.Buffered(1))


def _dot(a, b):
    return jnp.dot(a, b, preferred_element_type=F32)


def _dot_nt(a, b):
    return lax.dot_general(a, b, (((1,), (1,)), ((), ())), preferred_element_type=F32)


def _dot_exact(a, b):
    return jnp.dot(a, b, preferred_element_type=F32, precision=lax.Precision.HIGHEST)


def _sigmoid(x):
    return 0.5 * jnp.tanh(0.5 * x) + 0.5


def _silu(x):
    return x * _sigmoid(x)


def _layer_norm(x, g, b):
    xc = x - jnp.mean(x, axis=-1, keepdims=True)
    var = jnp.mean(xc * xc, axis=-1, keepdims=True)
    return xc * lax.rsqrt(var + LN_EPS) * g + b


FF_CHUNK = 256


def _ffn_kernel(x_ref, wgu_ref, wd_ref, g_ref, b_ref, o_ref, acc_ref, *, alpha):
    x = x_ref[...]
    xb = x.astype(BF16)
    for j in range(FF_HIDDEN // FF_CHUNK):
        lo = j * FF_CHUNK
        gate = _dot(xb, wgu_ref[:, lo:lo + FF_CHUNK])
        up = _dot(xb, wgu_ref[:, FF_HIDDEN + lo:FF_HIDDEN + lo + FF_CHUNK])
        act = (_silu(gate) * up).astype(BF16)
        down = _dot(act, wd_ref[lo:lo + FF_CHUNK, :])
        if j == 0:
            acc_ref[...] = down
        else:
            acc_ref[...] += down
    o_ref[...] = _layer_norm(alpha * x + 0.5 * acc_ref[...], g_ref[...], b_ref[...])


def _ffn_ln(x, wgu, wd, g, b, *, tm, alpha):
    m = x.shape[0]
    row = pl.BlockSpec((tm, D_MODEL), lambda i: (i, 0))
    return pl.pallas_call(
        functools.partial(_ffn_kernel, alpha=alpha),
        grid=(m // tm,),
        in_specs=[row, _resident(wgu.shape), _resident(wd.shape), _resident(g.shape), _resident(b.shape)],
        out_specs=row,
        out_shape=jax.ShapeDtypeStruct((m, D_MODEL), F32),
        scratch_shapes=[pltpu.VMEM((tm, D_MODEL), F32)],
        compiler_params=_cparams(1),
    )(x, wgu, wd, g, b)


def _softplus(x):
    return jnp.maximum(x, 0.0) + jnp.log1p(jnp.exp(-jnp.abs(x)))


def _rope_slice(x, cos_t, sin_up, sin_dn):
    return (x * cos_t + pltpu.roll(x, LANES - ROT_DIM // 2, axis=1) * sin_up
            + pltpu.roll(x, ROT_DIM // 2, axis=1) * sin_dn)


def _proj_ssd_kernel(x_ref, wz_ref, wxbc_ref, wdt_ref, dtb_ref, z_ref, xbc_ref, dt_ref):
    xb = x_ref[...].astype(BF16)
    cw = 512
    for c in range(SSD_INNER // cw):
        z_ref[:, c * cw:(c + 1) * cw] = _dot(xb, wz_ref[:, c * cw:(c + 1) * cw])
    for c in range(SSD_XBC // cw):
        xbc_ref[:, c * cw:(c + 1) * cw] = _dot(xb, wxbc_ref[:, c * cw:(c + 1) * cw])
    dt_ref[...] = _softplus(_dot(xb, wdt_ref[...]) + dtb_ref[...])


def _proj_ssd(x, wz, wxbc, wdt, dtb, *, tm):
    m = x.shape[0]

    def row(n):
        return pl.BlockSpec((tm, n), lambda i: (i, 0))

    return pl.pallas_call(
        _proj_ssd_kernel,
        grid=(m // tm,),
        in_specs=[row(D_MODEL), _resident(wz.shape), _resident(wxbc.shape), _resident(wdt.shape),
                  _resident(dtb.shape)],
        out_specs=[row(SSD_INNER), row(SSD_XBC), row(SSD_GROUPS * LANES)],
        out_shape=[jax.ShapeDtypeStruct((m, SSD_INNER), F32), jax.ShapeDtypeStruct((m, SSD_XBC), F32),
                   jax.ShapeDtypeStruct((m, SSD_GROUPS * LANES), F32)],
        compiler_params=_cparams(1),
    )(x, wz, wxbc, wdt, dtb)


def _proj_ssd_conv_kernel(x_ref, wz_ref, wxbc_ref, wdt_ref, dtb_ref, cw_ref, cb_ref,
                          zs_ref, xs_ref, bc_ref, dt_ref, tail_ref, win_ref, carry_ref, *, tiles_per_seq):
    tm = x_ref.shape[0]
    split = (tm // 2 + 15) // 16 * 16
    first = 8 - (SSD_CONV - 1)
    xb = x_ref[...].astype(BF16)

    @pl.when(pl.program_id(0) % tiles_per_seq == 0)
    def _():
        carry_ref[...] = jnp.zeros_like(carry_ref)

    cw = 512
    n_xbc, n_z = SSD_XBC // cw, SSD_INNER // cw

    def project(c):
        cols = slice(c * cw, (c + 1) * cw)
        win_ref[c % 2, 0:8, :] = carry_ref[:, cols]
        win_ref[c % 2, 8:8 + tm, :] = _dot(xb, wxbc_ref[:, cols])

    def conv(c):
        win = win_ref.at[c % 2]
        for j in range(cw // LANES):
            lanes = slice(j * LANES, (j + 1) * LANES)
            col = c * cw + j * LANES
            for r0, n in ((0, split), (split, tm - split)):
                taps = [win[first + k + r0:first + k + r0 + n, lanes] for k in range(SSD_CONV)]
                act = _conv_silu(taps, cw_ref[:, :, col:col + LANES], cb_ref[:, col:col + LANES])
                if col < SSD_INNER:
                    xs_ref[r0:r0 + n, col:col + LANES] = act
                else:
                    bc_ref[r0:r0 + n, col - SSD_INNER:col - SSD_INNER + LANES] = act.astype(BF16)
        tail = win[tm:tm + 8, :]
        tail_ref[:, c * cw:(c + 1) * cw] = tail
        carry_ref[:, c * cw:(c + 1) * cw] = tail

    project(0)
    for c in range(n_xbc):
        if c + 1 < n_xbc:
            project(c + 1)
        if c < n_z:
            zs_ref[:, c * cw:(c + 1) * cw] = _silu(_dot(xb, wz_ref[:, c * cw:(c + 1) * cw])).astype(BF16)
        conv(c)
    dt_ref[...] = _softplus(_dot(xb, wdt_ref[...]) + dtb_ref[...])


def _proj_ssd_conv(x, wz, wxbc, wdt, dtb, conv_w, conv_b, *, tm, tiles_per_seq):
    m = x.shape[0]
    assert tm % 16 == 0

    def row(n):
        return pl.BlockSpec((tm, n), lambda i: (i, 0))

    consts = (wz, wxbc, wdt, dtb, conv_w, conv_b)
    return pl.pallas_call(
        functools.partial(_proj_ssd_conv_kernel, tiles_per_seq=tiles_per_seq),
        grid=(m // tm,),
        in_specs=[row(D_MODEL)] + [_resident(c.shape) for c in consts],
        out_specs=[row(SSD_INNER), row(SSD_INNER), row(2 * SSD_GROUPS * SSD_STATE), row(SSD_GROUPS * LANES),
                   pl.BlockSpec((8, SSD_XBC), lambda i: (i // tiles_per_seq, 0))],
        out_shape=[jax.ShapeDtypeStruct((m, SSD_INNER), BF16), jax.ShapeDtypeStruct((m, SSD_INNER), F32),
                   jax.ShapeDtypeStruct((m, 2 * SSD_GROUPS * SSD_STATE), BF16),
                   jax.ShapeDtypeStruct((m, SSD_GROUPS * LANES), F32),
                   jax.ShapeDtypeStruct((m // (tm * tiles_per_seq) * 8, SSD_XBC), F32)],
        scratch_shapes=[pltpu.VMEM((2, 8 + tm, 512), F32), pltpu.VMEM((8, SSD_XBC), F32)],
        compiler_params=_cparams(1),
    )(x, *consts)


def _proj_mix_kernel(x_ref, cos_ref, sup_ref, sdn_ref, wq_ref, wk_ref, wv_ref, wga_ref, wgb_ref,
                     q_ref, k_ref, v_ref, u_ref):
    xb = x_ref[...].astype(BF16)
    cos_t, sin_up, sin_dn = cos_ref[...], sup_ref[...], sdn_ref[...]
    cw = 256
    for c in range(DA_WIDTH // cw):
        lo = c * cw
        qc = _dot(xb, wq_ref[:, lo:lo + cw])
        kc = _dot(xb, wk_ref[:, lo:lo + cw])
        for s in range(cw // HEAD_W):
            sl = slice(s * HEAD_W, (s + 1) * HEAD_W)
            dst = slice(lo + s * HEAD_W, lo + (s + 1) * HEAD_W)
            q_ref[:, dst] = (_rope_slice(qc[:, sl], cos_t, sin_up, sin_dn)
                             * (DA_HEAD_DIM ** -0.5)).astype(q_ref.dtype)
            k_ref[:, dst] = _rope_slice(kc[:, sl], cos_t, sin_up, sin_dn)
        v_ref[:, lo:lo + cw] = _dot(xb, wv_ref[:, lo:lo + cw])
        ga = _dot(xb, wga_ref[:, lo:lo + cw])
        gb = _dot(xb, wgb_ref[:, lo:lo + cw])
        u_ref[:, lo:lo + cw] = ga * _sigmoid(gb)


def _proj_mix_cc_kernel(x_ref, cos_ref, sup_ref, sdn_ref, wq_ref, wk_ref, wv_ref, wga_ref, wgb_ref,
                        ccw_ref, ccb_ref, g_ref, beta_ref, q_ref, k_ref, v_ref, c_ref, utail_ref,
                        win_ref, conv_ref, *, tiles_per_seq):
    tm = x_ref.shape[0]
    xb = x_ref[...].astype(BF16)
    cos_t, sin_up, sin_dn = cos_ref[...], sup_ref[...], sdn_ref[...]

    @pl.when(pl.program_id(0) % tiles_per_seq == 0)
    def _():
        win_ref[0:CC_HIST, :] = jnp.zeros((CC_HIST, CC_CH), F32)

    pieces = [(r, min(CC_PIECE, tm - r)) for r in range(0, tm, CC_PIECE)]

    def conv(lo, hi):
        for lb in range(lo // LANES, hi // LANES):
            lanes = slice(lb * LANES, (lb + 1) * LANES)
            for r0, n in pieces:
                conv_ref[r0:r0 + n, lanes] = _cc_taps(
                    lambda start, span: win_ref[r0 + start:r0 + start + span, lanes], ccw_ref, ccb_ref, lanes, n)

    cw = 256
    for c in range(DA_WIDTH // cw):
        lo = c * cw
        qc = _dot(xb, wq_ref[:, lo:lo + cw])
        kc = _dot(xb, wk_ref[:, lo:lo + cw])
        for s in range(cw // HEAD_W):
            sl = slice(s * HEAD_W, (s + 1) * HEAD_W)
            dst = slice(lo + s * HEAD_W, lo + (s + 1) * HEAD_W)
            q_ref[:, dst] = (_rope_slice(qc[:, sl], cos_t, sin_up, sin_dn)
                             * (DA_HEAD_DIM ** -0.5)).astype(q_ref.dtype)
            k_ref[:, dst] = _rope_slice(kc[:, sl], cos_t, sin_up, sin_dn)
        v_ref[:, lo:lo + cw] = _dot(xb, wv_ref[:, lo:lo + cw])
        ga = _dot(xb, wga_ref[:, lo:lo + cw])
        gb = _dot(xb, wgb_ref[:, lo:lo + cw])
        win_ref[CC_HIST:CC_HIST + tm, lo:lo + cw] = ga * _sigmoid(gb)
        if c > 0:
            conv(lo - cw, lo)
    conv(DA_WIDTH - cw, DA_WIDTH)
    c_ref[...] = _silu(_layer_norm(conv_ref[...], g_ref[...], beta_ref[...])).astype(c_ref.dtype)
    tail = win_ref[tm:tm + CC_HIST, :]
    utail_ref[...] = tail
    win_ref[0:CC_HIST, :] = tail


def _proj_mix_cc(x, tables, wq, wk, wv, wga, wgb, ccw, ccb, g, beta, *, tm, tiles_per_seq):
    m = x.shape[0]
    n_tab = tables[0].shape[0] // tm
    row = pl.BlockSpec((tm, D_MODEL), lambda i: (i, 0))
    tab = pl.BlockSpec((tm, LANES), lambda i: (i % n_tab, 0))
    w = _resident(wq.shape)
    consts = (ccw, ccb, g, beta)
    return pl.pallas_call(
        functools.partial(_proj_mix_cc_kernel, tiles_per_seq=tiles_per_seq),
        grid=(m // tm,),
        in_specs=[row, tab, tab, tab, w, w, w, w, w] + [_resident(c.shape) for c in consts],
        out_specs=[row, row, row, row, pl.BlockSpec((CC_HIST, CC_CH), lambda i: (i // tiles_per_seq, 0))],
        out_shape=[jax.ShapeDtypeStruct((m, D_MODEL), BF16), jax.ShapeDtypeStruct((m, D_MODEL), F32),
                   jax.ShapeDtypeStruct((m, D_MODEL), F32), jax.ShapeDtypeStruct((m, CC_CH), BF16),
                   jax.ShapeDtypeStruct((m // (tm * tiles_per_seq) * CC_HIST, CC_CH), F32)],
        scratch_shapes=[pltpu.VMEM((CC_HIST + tm, CC_CH), F32), pltpu.VMEM((tm, CC_CH), F32)],
        compiler_params=_cparams(1),
    )(x, *tables, wq, wk, wv, wga, wgb, *consts)


def _proj_mix(x, tables, wq, wk, wv, wga, wgb, *, tm, q_dtype):
    m = x.shape[0]
    n_tab = tables[0].shape[0] // tm
    row = pl.BlockSpec((tm, D_MODEL), lambda i: (i, 0))
    tab = pl.BlockSpec((tm, LANES), lambda i: (i % n_tab, 0))
    w = _resident(wq.shape)
    return pl.pallas_call(
        _proj_mix_kernel,
        grid=(m // tm,),
        in_specs=[row, tab, tab, tab, w, w, w, w, w],
        out_specs=[row, row, row, row],
        out_shape=[jax.ShapeDtypeStruct((m, D_MODEL), q_dtype)] + [jax.ShapeDtypeStruct((m, D_MODEL), F32)] * 3,
        compiler_params=_cparams(1),
    )(x, *tables, wq, wk, wv, wga, wgb)


def _merge_kernel(h_ref, ys_ref, yo_ref, yc_ref, wgate_ref, bgate_ref, wssd_ref, wda_ref, wcc_ref, wout_ref,
                  g_ref, b_ref, o_ref, *, alpha):
    h = h_ref[...]
    hb = h.astype(BF16)
    merged = None
    for i, (y_ref, w_ref) in enumerate(((ys_ref, wssd_ref), (yo_ref, wda_ref), (yc_ref, wcc_ref))):
        sl = slice(i * D_MODEL, (i + 1) * D_MODEL)
        gate = _sigmoid(_dot(hb, wgate_ref[:, sl]) + bgate_ref[:, sl])
        term = gate * _dot(y_ref[...].astype(BF16), w_ref[...])
        merged = term if merged is None else merged + term
    mix = _dot(merged.astype(BF16), wout_ref[...])
    o_ref[...] = _layer_norm(alpha * h + mix, g_ref[...], b_ref[...])


def _merge(h, ys, yo, yc, wgate, bgate, wssd, wda, wcc, wout, g, b, *, tm, alpha):
    m = h.shape[0]

    def row(n):
        return pl.BlockSpec((tm, n), lambda i: (i, 0))

    consts = (wgate, bgate, wssd, wda, wcc, wout, g, b)
    return pl.pallas_call(
        functools.partial(_merge_kernel, alpha=alpha),
        grid=(m // tm,),
        in_specs=[row(D_MODEL), row(SSD_INNER), row(DA_WIDTH), row(CC_CH)] + [_resident(c.shape) for c in consts],
        out_specs=row(D_MODEL),
        out_shape=jax.ShapeDtypeStruct((m, D_MODEL), F32),
        compiler_params=_cparams(1),
    )(h, ys, yo, yc, *consts)


def _pair_cols(x, h0, h1, lane_lo):
    shape = (x.shape[0], LANES)
    return jnp.where(lane_lo[:x.shape[0]], jnp.broadcast_to(x[:, h0:h0 + 1], shape),
                     jnp.broadcast_to(x[:, h1:h1 + 1], shape))


def _ssd_intra(xs, bc, cc, dt, acol, mask, lane_lo):
    arow = acol.T
    bb = bc.astype(BF16)
    cb = _dot_nt(cc.astype(BF16), bb)
    y_in, xdts, aces = [], [], []
    for j in range(HEADS_PER_GROUP // 2):
        h0, h1 = 2 * j, 2 * j + 1
        gs, wide = [], []
        for h in (h0, h1):
            a_h = jnp.broadcast_to(acol[:, h:h + 1], acol.shape)
            wide.append(a_h)
            gs.append((cb * jnp.exp(jnp.where(mask, a_h - arow[h:h + 1, :], -jnp.inf))).astype(BF16))
        gp = jnp.concatenate(gs, axis=1)
        xdt = xs[:, j * LANES:(j + 1) * LANES] * _pair_cols(dt, h0, h1, lane_lo)
        zero = jnp.zeros_like(xdt)
        xbd = jnp.concatenate([jnp.where(lane_lo, xdt, zero), jnp.where(lane_lo, zero, xdt)], axis=0).astype(BF16)
        y_in.append(_dot(gp, xbd))
        xdts.append(xdt)
        aces.append(jnp.where(lane_lo, wide[0], wide[1]))
    return y_in, xdts, aces, bb


def _ssd_finish(ys, xs, gate, d_row, nw_row):
    gated, ss = [], None
    for j, y in enumerate(ys):
        sl = slice(j * LANES, (j + 1) * LANES)
        yz = (y + d_row[:, sl] * xs[:, sl]) * gate[:, sl]
        gated.append(yz)
        ss = yz * yz if ss is None else ss + yz * yz
    rs = lax.rsqrt(jnp.sum(ss, axis=1, keepdims=True) * (1.0 / GROUP_CH) + LN_EPS)
    return [(yz * rs * nw_row[:, j * LANES:(j + 1) * LANES]).astype(BF16) for j, yz in enumerate(gated)]


def _conv_silu(taps, w_ref, b_ref):
    rows, ch = taps[0].shape
    acc = jnp.broadcast_to(b_ref[...][None], (rows // 8, 8, ch))
    for k in range(SSD_CONV):
        acc = acc + w_ref[k][None] * taps[k].reshape(rows // 8, 8, ch)
    return _silu(acc).reshape(rows, ch)


def _ssd_prompt_kernel(xs_ref, bm_ref, cm_ref, zs_ref, dt_ref, a_ref, d_ref, nw_ref, y_ref, hfin_ref, state_ref):
    L = SSD_CHUNK
    rows = lax.broadcasted_iota(jnp.int32, (L, LANES), 0)
    cols = lax.broadcasted_iota(jnp.int32, (L, LANES), 1)
    causal = rows >= cols
    tri = causal.astype(F32)
    lane_lo = cols < SSD_HEAD_DIM
    row_lo = rows < SSD_HEAD_DIM
    state_ref[...] = jnp.zeros_like(state_ref)

    def local_part(r0, meta):
        xs = xs_ref[pl.ds(r0, L), :]
        bc = bm_ref[pl.ds(r0, L), :]
        cc = cm_ref[pl.ds(r0, L), :]
        dt = dt_ref[pl.ds(r0, L), :]
        if meta:
            dt = jnp.where(rows < N_META, dt, 0.0)
        acol = _dot_exact(tri, dt * a_ref[...])
        alast = acol[L - 1:L, :]
        y_in, xdts, aces, bb = _ssd_intra(xs, bc, cc, dt, acol, causal, lane_lo)
        pairs = []
        for j in range(HEADS_PER_GROUP // 2):
            h0, h1 = 2 * j, 2 * j + 1
            ace = aces[j]
            ale = jnp.where(lane_lo[0:1], jnp.broadcast_to(alast[:, h0:h0 + 1], (1, LANES)),
                            jnp.broadcast_to(alast[:, h1:h1 + 1], (1, LANES)))
            xw = xdts[j] * jnp.exp(ale - ace)
            upd = _dot(xw.T.astype(BF16), bb)
            dec = jnp.where(row_lo, jnp.broadcast_to(jnp.exp(alast[:, h0:h0 + 1]), (L, LANES)),
                            jnp.broadcast_to(jnp.exp(alast[:, h1:h1 + 1]), (L, LANES)))
            pairs.append((y_in[j], jnp.exp(ace), upd, dec))
        return xs, cc.astype(BF16), pairs

    def state_part(r0, xs, cb16, pairs):
        ys = []
        for j, (y_loc, eace, upd, dec) in enumerate(pairs):
            st = state_ref[j * LANES:(j + 1) * LANES, :]
            ys.append(y_loc + _dot_nt(cb16, st.astype(BF16)) * eace)
            state_ref[j * LANES:(j + 1) * LANES, :] = st * dec + upd
        outs = _ssd_finish(ys, xs, zs_ref[pl.ds(r0, L), :].astype(F32), d_ref[...], nw_ref[...])
        for j, o in enumerate(outs):
            y_ref[pl.ds(r0, L), j * LANES:(j + 1) * LANES] = o

    state_part(0, *local_part(0, True))

    def body(c, carry):
        r_a = pl.multiple_of(N_META + 2 * c * L, 16)
        r_b = pl.multiple_of(N_META + (2 * c + 1) * L, 16)
        loc_a = local_part(r_a, False)
        loc_b = local_part(r_b, False)
        state_part(r_a, *loc_a)
        state_part(r_b, *loc_b)
        return carry

    n_chunks = (y_ref.shape[0] - N_META) // L
    assert n_chunks % 2 == 0
    lax.fori_loop(0, n_chunks // 2, body, 0)
    hfin_ref[...] = state_ref[...]


def _ssd_prompt(xs, bcm, zs, dt, a4, d_e, nw, *, batch, seq):
    def seq_block(width, col0=0):
        return pl.BlockSpec((seq, width), lambda b, g: (b, g + col0))

    def par_block(rows, width):
        return pl.BlockSpec((rows, width), lambda b, g: (0, g))

    return pl.pallas_call(
        _ssd_prompt_kernel,
        grid=(batch, SSD_GROUPS),
        in_specs=[seq_block(GROUP_CH), seq_block(SSD_STATE), seq_block(SSD_STATE, SSD_GROUPS),
                  seq_block(GROUP_CH), seq_block(LANES),
                  par_block(1, LANES), par_block(1, GROUP_CH), par_block(1, GROUP_CH)],
        out_specs=[seq_block(GROUP_CH), pl.BlockSpec((None, GROUP_CH, SSD_STATE), lambda b, g: (b, g, 0))],
        out_shape=[jax.ShapeDtypeStruct((batch * seq, SSD_INNER), BF16),
                   jax.ShapeDtypeStruct((batch, SSD_INNER, SSD_STATE), F32)],
        scratch_shapes=[pltpu.VMEM((GROUP_CH, SSD_STATE), F32)],
        compiler_params=_cparams(2),
    )(xs, bcm, bcm, zs, dt, a4, d_e, nw)


def _ssd_sample_kernel(xs_ref, bm_ref, cm_ref, px_ref, pb_ref, pc_ref, z_ref, dt_ref, h0_ref,
                       wx_ref, wb_ref, wc_ref, bx_ref, bb_ref, bc_ref, a_ref, d_ref, nw_ref, *rest, n_prev):
    y_ref, h1_ref, ext_ref, flat_ref, xwt_ref, yst_ref = rest[-6:]
    if n_prev:
        h1_ref[0:n_prev] = rest[0][...]
    R = SAMPLE_ROWS
    rows = lax.broadcasted_iota(jnp.int32, (R, LANES), 0)
    cols = lax.broadcasted_iota(jnp.int32, (R, LANES), 1)
    same = (rows >> 3) == (cols >> 3)
    causal = same & (rows >= cols)
    lane_lo = cols < SSD_HEAD_DIM
    srcs = ((xs_ref, px_ref, 0, GROUP_CH), (bm_ref, pb_ref, GROUP_CH, GROUP_CH + SSD_STATE),
            (cm_ref, pc_ref, GROUP_CH + SSD_STATE, GROUP_CH + 2 * SSD_STATE))
    for cur, prev, lo, hi in srcs:
        ext_ref[:, 0:8, lo:hi] = prev[...]
        ext_ref[:, 8:16, lo:hi] = cur[...]
    first = 8 - (SSD_CONV - 1)
    for k in range(SSD_CONV):
        flat_ref[k] = ext_ref[:, first + k:first + k + 8, :].reshape(R, GROUP_CH + 2 * SSD_STATE)

    def conv(lo, hi, w_ref, b_ref):
        return _conv_silu([flat_ref[k, :, lo:hi] for k in range(SSD_CONV)], w_ref, b_ref)

    xs = conv(0, GROUP_CH, wx_ref, bx_ref)
    bc = conv(GROUP_CH, GROUP_CH + SSD_STATE, wb_ref, bb_ref)
    cc = conv(GROUP_CH + SSD_STATE, GROUP_CH + 2 * SSD_STATE, wc_ref, bc_ref)
    dt = dt_ref[...]
    dta = dt * a_ref[...]
    acol = _dot_exact(causal.astype(F32), dta)
    total = _dot_exact(same.astype(F32), dta)
    y_in, xdts, aces, _ = _ssd_intra(xs, bc, cc, dt, acol, causal, lane_lo)
    for j in range(HEADS_PER_GROUP // 2):
        xw = xdts[j] * jnp.exp(_pair_cols(total, 2 * j, 2 * j + 1, lane_lo) - aces[j])
        xwt_ref[j * LANES:(j + 1) * LANES, :] = xw.T.astype(BF16)
    xwt = xwt_ref[...]
    for i in range(SAMPLE_BLOCK):
        r = slice(8 * i, 8 * i + 8)
        st = h0_ref[i]
        c16 = jnp.concatenate([cc[r], jnp.zeros((8, SSD_STATE), F32)], axis=0).astype(BF16)
        yst_ref[r, :] = _dot_nt(c16, st.astype(BF16))[0:8]
        b_i = jnp.where((rows >> 3) == i, bc, 0.0).astype(BF16)
        upd = _dot(xwt, b_i)
        tot_i = total[8 * i:8 * i + 1, :]
        for h in range(HEADS_PER_GROUP):
            hs = slice(h * SSD_HEAD_DIM, (h + 1) * SSD_HEAD_DIM)
            h1_ref[n_prev, i, hs, :] = st[hs] * jnp.exp(tot_i[:, h:h + 1]) + upd[hs]
    ys = [y_in[j] + yst_ref[:, j * LANES:(j + 1) * LANES] * jnp.exp(aces[j])
          for j in range(HEADS_PER_GROUP // 2)]
    outs = _ssd_finish(ys, xs, _silu(z_ref[...]), d_ref[...], nw_ref[...])
    for j, o in enumerate(outs):
        y_ref[:, j * LANES:(j + 1) * LANES] = o


def _ssd_sample(xbc3, prev4, z, dt, h0_all, conv_w, conv_b, a4, d_e, nw, *, layer, prev_h1=None):
    n_req = xbc3.shape[0]
    n_prev = 0 if prev_h1 is None else prev_h1.shape[0]
    g_b = SSD_INNER // SSD_STATE
    R = SAMPLE_ROWS
    width = GROUP_CH + 2 * SSD_STATE

    def tok_block(w, col0=0):
        return pl.BlockSpec((SAMPLE_BLOCK, 8, w), lambda i, g: (i, 0, g + col0))

    def prev_block(w, col0=0):
        return pl.BlockSpec((None, SAMPLE_BLOCK, 8, w), lambda i, g: (layer, i, 0, g + col0))

    def row_block(w):
        return pl.BlockSpec((R, w), lambda i, g: (i, g))

    def par_block(rows, w, col0=0):
        return pl.BlockSpec((rows, w), lambda i, g: (0, g + col0))

    def tap_block(w, col0=0):
        return pl.BlockSpec((SSD_CONV, 8, w), lambda i, g: (0, 0, g + col0))

    state = pl.BlockSpec((None, SAMPLE_BLOCK, GROUP_CH, SSD_STATE), lambda i, g: (layer, i, g, 0))
    toks = [tok_block(GROUP_CH), tok_block(SSD_STATE, g_b), tok_block(SSD_STATE, g_b + SSD_GROUPS)]
    prevs = [prev_block(GROUP_CH), prev_block(SSD_STATE, g_b), prev_block(SSD_STATE, g_b + SSD_GROUPS)]
    in_specs = toks + prevs + [row_block(GROUP_CH), row_block(LANES), state,
                               tap_block(GROUP_CH), tap_block(SSD_STATE, g_b),
                               tap_block(SSD_STATE, g_b + SSD_GROUPS),
                               par_block(8, GROUP_CH), par_block(8, SSD_STATE, g_b),
                               par_block(8, SSD_STATE, g_b + SSD_GROUPS),
                               par_block(1, LANES), par_block(1, GROUP_CH), par_block(1, GROUP_CH)]
    args = [xbc3, xbc3, xbc3, prev4, prev4, prev4, z, dt, h0_all, conv_w, conv_w, conv_w, conv_b, conv_b, conv_b,
            a4, d_e, nw]
    def stack_block(n):
        return pl.BlockSpec((n, SAMPLE_BLOCK, GROUP_CH, SSD_STATE), lambda i, g: (0, i, g, 0))

    if n_prev:
        in_specs = in_specs + [stack_block(n_prev)]
        args = args + [prev_h1]
    return pl.pallas_call(
        functools.partial(_ssd_sample_kernel, n_prev=n_prev),
        grid=(n_req // SAMPLE_BLOCK, SSD_GROUPS),
        in_specs=in_specs,
        out_specs=[row_block(GROUP_CH), stack_block(n_prev + 1)],
        out_shape=[jax.ShapeDtypeStruct((n_req * 8, SSD_INNER), BF16),
                   jax.ShapeDtypeStruct((n_prev + 1, n_req, SSD_INNER, SSD_STATE), F32)],
        scratch_shapes=[pltpu.VMEM((SAMPLE_BLOCK, 16, width), F32), pltpu.VMEM((SSD_CONV, R, width), F32),
                        pltpu.VMEM((GROUP_CH, R), BF16), pltpu.VMEM((R, GROUP_CH), F32)],
        compiler_params=_cparams(2),
    )(*args)


def _sub_ln(o, w_row, post_scale):
    return o * lax.rsqrt(jnp.mean(o * o, axis=-1, keepdims=True) + LN_EPS) * w_row * post_scale


def _split_components(q):
    lane_lo = lax.broadcasted_iota(jnp.int32, q.shape, 1) < DA_HEAD_DIM
    zero = jnp.zeros_like(q)
    return jnp.concatenate([jnp.where(lane_lo, q, zero), jnp.where(lane_lo, zero, q)], axis=0)


def _attn_prompt_kernel(lam_ref, q_ref, k_ref, v_ref, w_ref, *rest, n_prev, post_scale):
    kb_ref, vb_ref = rest[-2:]
    if n_prev:
        kp_ref, vp_ref, o_ref, kall_ref, vall_ref = rest[:5]
        kall_ref[0:n_prev] = kp_ref[...]
        vall_ref[0:n_prev] = vp_ref[...]
        kall_ref[n_prev] = k_ref[...]
        vall_ref[n_prev] = v_ref[...]
    else:
        o_ref = rest[0]
    lam = lam_ref[0]
    kb_ref[...] = k_ref[...].astype(BF16)
    vb_ref[:, 0:HEAD_W] = v_ref[...].astype(BF16)
    vb_ref[:, HEAD_W:2 * HEAD_W] = jnp.ones((vb_ref.shape[0], HEAD_W), BF16)
    k_meta = kb_ref[0:LANES]
    v_meta = vb_ref[0:LANES]
    w_row = w_ref[...]

    def attend(r0, n_q, parts):
        m = None
        for s, _ in parts:
            pm = jnp.max(s, axis=1, keepdims=True)
            m = pm if m is None else jnp.maximum(m, pm)
        acc = None
        for s, vals in parts:
            p = jnp.exp((s - m).astype(BF16))
            term = _dot(p, vals)
            acc = term if acc is None else acc + term
        on = acc[:, 0:HEAD_W] * (1.0 / acc[:, HEAD_W:HEAD_W + 1])
        o = on[0:n_q] - lam * on[n_q:2 * n_q]
        o_ref[r0:r0 + n_q, :] = _sub_ln(o, w_row, post_scale).astype(o_ref.dtype)

    q2 = _split_components(q_ref[0:N_META])
    s = _dot_nt(q2, k_meta)
    qpos = lax.broadcasted_iota(jnp.int32, s.shape, 0) & (N_META - 1)
    kpos = lax.broadcasted_iota(jnp.int32, s.shape, 1)
    attend(0, N_META, [(jnp.where(kpos <= qpos, s, -jnp.inf), v_meta)])

    n_blk = (q_ref.shape[0] - N_META) // Q_BLOCK
    blk_shape = (2 * Q_BLOCK, Q_BLOCK)
    kcol = lax.broadcasted_iota(jnp.int32, blk_shape, 1)
    meta_ok = lax.broadcasted_iota(jnp.int32, (2 * Q_BLOCK, LANES), 1) < N_META
    diag_ok = kcol <= (lax.broadcasted_iota(jnp.int32, blk_shape, 0) & (Q_BLOCK - 1))

    def scores(i):
        r0 = N_META + i * Q_BLOCK
        q2 = _split_components(q_ref[r0:r0 + Q_BLOCK])
        parts = [(jnp.where(meta_ok, _dot_nt(q2, k_meta), -jnp.inf), v_meta)]
        if i > 0:
            parts.append((_dot_nt(q2, kb_ref[N_META:r0]), vb_ref[N_META:r0]))
        parts.append((jnp.where(diag_ok, _dot_nt(q2, kb_ref[r0:r0 + Q_BLOCK]), -jnp.inf),
                      vb_ref[r0:r0 + Q_BLOCK]))
        return parts

    nxt = scores(0)
    for i in range(n_blk):
        cur, nxt = nxt, (scores(i + 1) if i + 1 < n_blk else None)
        attend(N_META + i * Q_BLOCK, Q_BLOCK, cur)


def _attn_prompt(lam, q, k, v, w_row, *, batch, seq, post_scale, prev_kv=None):
    blk = pl.BlockSpec((seq, HEAD_W), lambda b, h: (b, h))
    in_specs = [pl.BlockSpec(memory_space=pltpu.SMEM), blk, blk, blk, _resident(w_row.shape)]
    args = [lam, q, k, v, w_row]
    out_specs = [blk]
    out_shape = [jax.ShapeDtypeStruct((batch * seq, DA_WIDTH), BF16)]
    n_prev = 0
    if prev_kv is not None:
        n_prev = prev_kv[0].shape[0]
        in_specs += [pl.BlockSpec((n_prev, seq, HEAD_W), lambda b, h: (0, b, h))] * 2
        args += list(prev_kv)
        out_specs += [pl.BlockSpec((n_prev + 1, seq, HEAD_W), lambda b, h: (0, b, h))] * 2
        out_shape += [jax.ShapeDtypeStruct((n_prev + 1, batch * seq, DA_WIDTH), F32)] * 2
    return pl.pallas_call(
        functools.partial(_attn_prompt_kernel, n_prev=n_prev, post_scale=post_scale),
        grid=(batch, DA_HEADS),
        in_specs=in_specs,
        out_specs=out_specs,
        out_shape=out_shape,
        scratch_shapes=[pltpu.VMEM((seq, HEAD_W), BF16), pltpu.VMEM((seq, 2 * HEAD_W), BF16)],
        compiler_params=_cparams(2),
    )(*args)


def _attn_sample_kernel(pt_ref, lam_ref, q_ref, kn_ref, vn_ref, w_ref, *refs, n_pages, post_scale):
    del pt_ref
    k_pages = refs[:n_pages]
    v_pages = refs[n_pages:2 * n_pages]
    o_ref, s_ref = refs[2 * n_pages], refs[2 * n_pages + 1]
    lam = lam_ref[0]
    page_rows = PAGE_SIZE * DA_HEADS
    new_rows = 8 * DA_HEADS
    q = q_ref[...]
    lane = lax.broadcasted_iota(jnp.int32, (8, HEAD_W), 1)
    pieces = []
    for c in range(2):
        for h in range(DA_HEADS):
            qh = q[:, h * HEAD_W:(h + 1) * HEAD_W]
            keep = lane < DA_HEAD_DIM if c == 0 else lane >= DA_HEAD_DIM
            pieces.append(jnp.where(keep, qh, 0.0))
    qmat = jnp.concatenate(pieces, axis=0).astype(BF16)
    row_head = (lax.broadcasted_iota(jnp.int32, (LANES, page_rows), 0) >> 3) & (DA_HEADS - 1)
    col_head = lax.broadcasted_iota(jnp.int32, (LANES, page_rows), 1) & (DA_HEADS - 1)
    head_ok = row_head == col_head
    s_new = _dot_nt(qmat, kn_ref[...].astype(BF16))
    r_i = lax.broadcasted_iota(jnp.int32, s_new.shape, 0)
    c_i = lax.broadcasted_iota(jnp.int32, s_new.shape, 1)
    ok_new = (((r_i >> 3) & (DA_HEADS - 1)) == (c_i & (DA_HEADS - 1))) & ((c_i >> 3) <= (r_i & 7))
    s_new = jnp.where(ok_new, s_new, -jnp.inf)
    m = jnp.max(s_new, axis=1, keepdims=True)
    for p in range(n_pages):
        s = _dot_nt(qmat, k_pages[p][...].astype(BF16))
        s = jnp.where(head_ok, s, -jnp.inf)
        s_ref[:, p * page_rows:(p + 1) * page_rows] = s
        m = jnp.maximum(m, jnp.max(s, axis=1, keepdims=True))
    p_new = jnp.exp(s_new - m)
    den = jnp.sum(p_new, axis=1, keepdims=True)
    acc = _dot(p_new.astype(BF16), vn_ref[...].astype(BF16))
    for p in range(n_pages):
        e = jnp.exp(s_ref[:, p * page_rows:(p + 1) * page_rows] - m)
        den = den + jnp.sum(e, axis=1, keepdims=True)
        acc = acc + _dot(e.astype(BF16), v_pages[p][...].astype(BF16))
    on = acc * (1.0 / den)
    half = LANES // 2
    o = on[0:half] - lam * on[half:LANES]
    w_row = w_ref[...]
    for h in range(DA_HEADS):
        o_ref[:, h * HEAD_W:(h + 1) * HEAD_W] = _sub_ln(o[h * 8:(h + 1) * 8], w_row, post_scale)


def _attn_sample(page_table_flat, lam, q, k_new, v_new, w_row, cache_k, cache_v, *, layer, post_scale):
    n_req = q.shape[0] // 8
    n_pages = page_table_flat.shape[0] // n_req
    page_rows = PAGE_SIZE * DA_HEADS

    def page_spec(j):
        return pl.BlockSpec((None, None, page_rows, HEAD_W), lambda b, pt: (layer, pt[b * n_pages + j], 0, 0))

    tok = pl.BlockSpec((8, DA_WIDTH), lambda b, pt: (b, 0))
    new = pl.BlockSpec((None, 8 * DA_HEADS, HEAD_W), lambda b, pt: (b, 0, 0))
    pages = [page_spec(j) for j in range(n_pages)]
    grid_spec = pltpu.PrefetchScalarGridSpec(
        num_scalar_prefetch=1,
        grid=(n_req,),
        in_specs=[pl.BlockSpec(memory_space=pltpu.SMEM), tok, new, new,
                  pl.BlockSpec(w_row.shape, lambda b, pt: (0, 0))] + pages + pages,
        out_specs=tok,
        scratch_shapes=[pltpu.VMEM((LANES, n_pages * page_rows), F32)],
    )
    return pl.pallas_call(
        functools.partial(_attn_sample_kernel, n_pages=n_pages, post_scale=post_scale),
        grid_spec=grid_spec,
        out_shape=jax.ShapeDtypeStruct((n_req * 8, DA_WIDTH), F32),
        compiler_params=_cparams(1),
    )(page_table_flat, lam, q, k_new, v_new, w_row, *([cache_k] * n_pages), *([cache_v] * n_pages))


CC_HIST = 32
CC_PIECE = 96


def _cc_taps(load_window, w_ref, b_ref, lanes, rows):
    first = CC_HIST - (CC_KERNEL - 1)
    nt = rows // 8
    acc = jnp.broadcast_to(b_ref[:, lanes][None], (nt, 8, LANES))
    for phase in range(8):
        n_tap = (CC_KERNEL - 1 - phase) // 8 + 1
        span = rows + 8 * (n_tap - 1)
        win = load_window(first + phase, span).reshape(span // 8, 8, LANES)
        for a in range(n_tap):
            acc = acc + w_ref[8 * a + phase, :, lanes][None] * win[a:a + nt]
    return acc.reshape(rows, LANES)


def _cc_sample_kernel(u_ref, prev_ref, w_ref, b_ref, g_ref, beta_ref, o_ref, ext_ref):
    ext_ref[:, 0:CC_HIST, :] = prev_ref[...]
    ext_ref[:, CC_HIST:CC_HIST + 8, :] = u_ref[...]
    first = CC_HIST - (CC_KERNEL - 1)
    acc = jnp.broadcast_to(b_ref[...][None], (SAMPLE_BLOCK, 8, CC_CH))
    for k in range(CC_KERNEL):
        acc = acc + w_ref[k][None] * ext_ref[:, first + k:first + k + 8, :]
    acc = acc.reshape(SAMPLE_ROWS, CC_CH)
    o_ref[...] = _silu(_layer_norm(acc, g_ref[...], beta_ref[...])).astype(o_ref.dtype)


def _cc_sample(u3, prev4, w, b, g, beta, *, layer):
    n_req = u3.shape[0]
    return pl.pallas_call(
        _cc_sample_kernel,
        grid=(n_req // SAMPLE_BLOCK,),
        in_specs=[pl.BlockSpec((SAMPLE_BLOCK, 8, CC_CH), lambda i: (i, 0, 0)),
                  pl.BlockSpec((None, SAMPLE_BLOCK, CC_HIST, CC_CH), lambda i: (layer, i, 0, 0))]
        + [_resident(c.shape) for c in (w, b, g, beta)],
        out_specs=pl.BlockSpec((SAMPLE_ROWS, CC_CH), lambda i: (i, 0)),
        out_shape=jax.ShapeDtypeStruct((n_req * 8, CC_CH), BF16),
        scratch_shapes=[pltpu.VMEM((SAMPLE_BLOCK, CC_HIST + 8, CC_CH), F32)],
        compiler_params=_cparams(1),
    )(u3, prev4, w, b, g, beta)


def _rope_tables(pos):
    half = ROT_DIM // 2
    inv_freq = ROPE_THETA ** (-jnp.arange(half, dtype=F32) * 2.0 / ROT_DIM)
    ang = pos.astype(F32)[:, None] * inv_freq[None, :]
    cos, sin = jnp.cos(ang), jnp.sin(ang)
    n = pos.shape[0]
    pad = jnp.zeros((n, DA_HEAD_DIM - ROT_DIM), F32)
    zeros = jnp.zeros((n, half), F32)
    cos_c = jnp.concatenate([cos, cos, pad + 1.0], axis=1)
    up_c = jnp.concatenate([-sin, zeros, pad], axis=1)
    dn_c = jnp.concatenate([zeros, sin, pad], axis=1)
    return tuple(jnp.concatenate([t, t], axis=1) for t in (cos_c, up_c, dn_c))


def _head_lanes(vec):
    v = vec.astype(F32).reshape(SSD_GROUPS, HEADS_PER_GROUP)
    return jnp.pad(v, ((0, 0), (0, LANES - HEADS_PER_GROUP))).reshape(1, SSD_GROUPS * LANES)


def _sublane_repeat(w):
    w = w.astype(F32)
    return jnp.broadcast_to(w[..., None, :], w.shape[:-1] + (8, w.shape[-1]))


def _prepare_layer(l, p):
    pts = [0]
    for s in IN_SIZES:
        pts.append(pts[-1] + s)
    w_in = p['w_in'][l]
    cols = [w_in[:, pts[i]:pts[i + 1]] for i in range(len(IN_SIZES))]
    w_dt = cols[2].reshape(D_MODEL, SSD_GROUPS, HEADS_PER_GROUP)
    w_dt = jnp.pad(w_dt, ((0, 0), (0, 0), (0, LANES - HEADS_PER_GROUP))).reshape(D_MODEL, SSD_GROUPS * LANES)
    row = lambda v: v.astype(F32).reshape(1, -1)
    lam_init = 0.8 - 0.6 * math.exp(-0.3 * l)
    lam = (jnp.exp(jnp.sum(p['da_lambda_q1'][l].astype(F32) * p['da_lambda_k1'][l].astype(F32)))
           - jnp.exp(jnp.sum(p['da_lambda_q2'][l].astype(F32) * p['da_lambda_k2'][l].astype(F32))) + lam_init)
    return dict(
        lam_init=lam_init, lam=lam.reshape(1).astype(F32),
        ln_g=[row(p['ln_g'][l, i]) for i in range(3)], ln_b=[row(p['ln_b'][l, i]) for i in range(3)],
        ff1_gu=p['w_ff1_gu'][l].astype(BF16), ff1_d=p['w_ff1_down'][l].astype(BF16),
        ff2_gu=p['w_ff2_gu'][l].astype(BF16), ff2_d=p['w_ff2_down'][l].astype(BF16),
        w_z=cols[0].astype(BF16), w_xbc=cols[1].astype(BF16), w_dt=w_dt.astype(BF16),
        w_q=cols[3].astype(BF16), w_k=cols[4].astype(BF16), w_v=cols[5].astype(BF16),
        w_ga=cols[6][:, :CC_CH].astype(BF16), w_gb=cols[6][:, CC_CH:].astype(BF16),
        w_gate=cols[7].astype(BF16), b_gate=row(p['b_gate'][l]),
        dt_bias=_head_lanes(p['ssd_dt_bias'][l]), a4=_head_lanes(-jnp.exp(p['ssd_a_log'][l].astype(F32))),
        conv_w=_sublane_repeat(p['ssd_conv_w'][l]), conv_b=_sublane_repeat(p['ssd_conv_b'][l]),
        d_e=jnp.repeat(p['ssd_d'][l].astype(F32), SSD_HEAD_DIM).reshape(1, SSD_INNER),
        norm_w=row(p['ssd_norm_w'][l]), ssd_out=p['ssd_out'][l].astype(BF16),
        subln_w=row(p['da_subln_w'][l]), da_out=p['da_out'][l].astype(BF16),
        cc_w=_sublane_repeat(p['cc_conv_w'][l]), cc_b=_sublane_repeat(p['cc_conv_b'][l]),
        cc_g=row(p['cc_ln_g'][l]), cc_beta=row(p['cc_ln_b'][l]), cc_out=p['cc_out'][l].astype(BF16),
        w_out=p['w_out'][l].astype(BF16),
    )


def _mixer_tail(h1, ys, yo, yc, lp, alpha, tm_merge, tm_ffn):
    h2 = _merge(h1, ys, yo, yc, lp['w_gate'], lp['b_gate'], lp['ssd_out'], lp['da_out'], lp['cc_out'], lp['w_out'],
                lp['ln_g'][1], lp['ln_b'][1], tm=tm_merge, alpha=alpha)
    return _ffn_ln(h2, lp['ff2_gu'], lp['ff2_d'], lp['ln_g'][2], lp['ln_b'][2], tm=tm_ffn, alpha=alpha)


def kernel(x_prompt, x_sample, cache_k, cache_v, state_ssm, state_ssd_conv, state_conf_conv, page_table,
           meta_tokens, ln_g, ln_b, w_ff1_gu, w_ff1_down, w_ff2_gu, w_ff2_down, w_in, b_gate,
           ssd_conv_w, ssd_conv_b, ssd_dt_bias, ssd_a_log, ssd_d, ssd_norm_w, ssd_out,
           da_lambda_q1, da_lambda_k1, da_lambda_q2, da_lambda_k2, da_subln_w, da_out,
           cc_conv_w, cc_conv_b, cc_ln_g, cc_ln_b, cc_out, w_out):
    params = dict(ln_g=ln_g, ln_b=ln_b, w_ff1_gu=w_ff1_gu, w_ff1_down=w_ff1_down, w_ff2_gu=w_ff2_gu,
                  w_ff2_down=w_ff2_down, w_in=w_in, b_gate=b_gate, ssd_conv_w=ssd_conv_w, ssd_conv_b=ssd_conv_b,
                  ssd_dt_bias=ssd_dt_bias, ssd_a_log=ssd_a_log, ssd_d=ssd_d, ssd_norm_w=ssd_norm_w, ssd_out=ssd_out,
                  da_lambda_q1=da_lambda_q1, da_lambda_k1=da_lambda_k1, da_lambda_q2=da_lambda_q2,
                  da_lambda_k2=da_lambda_k2, da_subln_w=da_subln_w, da_out=da_out, cc_conv_w=cc_conv_w,
                  cc_conv_b=cc_conv_b, cc_ln_g=cc_ln_g, cc_ln_b=cc_ln_b, cc_out=cc_out, w_out=w_out)
    depth = w_in.shape[0]
    alpha = (2.0 * depth) ** 0.25
    bp, seq = x_prompt.shape[0], x_prompt.shape[1]
    lp_len = seq + N_META
    ds, dec_len = x_sample.shape[0], x_sample.shape[1]
    assert dec_len == 8 and lp_len % 48 == 0 and seq % (2 * SSD_CHUNK) == 0 and ds % SAMPLE_BLOCK == 0
    n_pages = page_table.shape[1]
    past = n_pages * cache_k.shape[2]
    n_pool = cache_k.shape[1]

    tm_p = lp_len // 3
    tm_pm = 384
    tm_s = 512
    assert (bp * lp_len) % tm_pm == 0 and (ds * dec_len) % tm_s == 0

    hp = jnp.concatenate([jnp.broadcast_to(meta_tokens.astype(F32)[None], (bp, N_META, D_MODEL)),
                          x_prompt.astype(F32)], axis=1).reshape(bp * lp_len, D_MODEL)
    hs = x_sample.astype(F32).reshape(ds * dec_len, D_MODEL)
    tab_p = _rope_tables(jnp.arange(lp_len))
    tab_s = tuple(jnp.tile(t, (tm_s // dec_len, 1)) for t in _rope_tables(past + jnp.arange(dec_len)))
    ck = cache_k.reshape(depth, n_pool, PAGE_SIZE * DA_HEADS, HEAD_W)
    cv = cache_v.reshape(depth, n_pool, PAGE_SIZE * DA_HEADS, HEAD_W)
    pt_flat = page_table.reshape(-1).astype(jnp.int32)

    h0_all = state_ssm.astype(F32).reshape(depth, ds, SSD_INNER, SSD_STATE)
    conv_prev4 = jnp.pad(state_ssd_conv.astype(F32), ((0, 0), (0, 0), (8 - (SSD_CONV - 1), 0), (0, 0)))
    cc_prev4 = jnp.pad(state_conf_conv.astype(F32), ((0, 0), (0, 0), (CC_HIST - (CC_KERNEL - 1), 0), (0, 0)))
    kv_p, h1_buf = None, None

    outs = {k: [] for k in ('ks', 'vs', 'hp', 'cp', 'cs', 'up', 'us')}
    for l in range(depth):
        lp = _prepare_layer(l, params)
        post = 1.0 - lp['lam_init']

        h1 = _ffn_ln(hp, lp['ff1_gu'], lp['ff1_d'], lp['ln_g'][0], lp['ln_b'][0], tm=tm_p, alpha=alpha)
        zs, xs, bcm, dt, conv_tail = _proj_ssd_conv(h1, lp['w_z'], lp['w_xbc'], lp['w_dt'], lp['dt_bias'],
                                                     lp['conv_w'], lp['conv_b'], tm=tm_p, tiles_per_seq=lp_len // tm_p)
        q, k, v, yc, u_tail = _proj_mix_cc(h1, tab_p, lp['w_q'], lp['w_k'], lp['w_v'], lp['w_ga'], lp['w_gb'],
                                           lp['cc_w'], lp['cc_b'], lp['cc_g'], lp['cc_beta'],
                                           tm=tm_p, tiles_per_seq=lp_len // tm_p)
        ys, hfin = _ssd_prompt(xs, bcm, zs, dt, lp['a4'], lp['d_e'], lp['norm_w'], batch=bp, seq=lp_len)
        att = _attn_prompt(lp['lam'], q, k, v, lp['subln_w'], batch=bp, seq=lp_len, post_scale=post,
                           prev_kv=kv_p)
        yo = att[0]
        kv_p = (k[None], v[None]) if kv_p is None else (att[1], att[2])
        hp = _mixer_tail(h1, ys, yo, yc, lp, alpha, tm_pm, tm_p)
        outs['hp'].append(hfin.reshape(bp, SSD_HEADS, SSD_HEAD_DIM, SSD_STATE))
        outs['cp'].append(conv_tail.reshape(bp, 8, SSD_XBC)[:, 8 - (SSD_CONV - 1):])
        outs['up'].append(u_tail.reshape(bp, CC_HIST, CC_CH)[:, CC_HIST - (CC_KERNEL - 1):])

        h1 = _ffn_ln(hs, lp['ff1_gu'], lp['ff1_d'], lp['ln_g'][0], lp['ln_b'][0], tm=tm_s, alpha=alpha)
        z, xbc, dt = _proj_ssd(h1, lp['w_z'], lp['w_xbc'], lp['w_dt'], lp['dt_bias'], tm=tm_s)
        q, k, v, u = _proj_mix(h1, tab_s, lp['w_q'], lp['w_k'], lp['w_v'], lp['w_ga'], lp['w_gb'],
                               tm=tm_s, q_dtype=F32)
        xbc3 = xbc.reshape(ds, dec_len, SSD_XBC)
        ys, h1_buf = _ssd_sample(xbc3, conv_prev4, z, dt, h0_all, lp['conv_w'], lp['conv_b'], lp['a4'], lp['d_e'],
                                 lp['norm_w'], layer=l, prev_h1=h1_buf)
        yo = _attn_sample(pt_flat, lp['lam'], q, k.reshape(ds, dec_len * DA_HEADS, HEAD_W),
                          v.reshape(ds, dec_len * DA_HEADS, HEAD_W), lp['subln_w'], ck, cv,
                          layer=l, post_scale=post)
        u3 = u.reshape(ds, dec_len, CC_CH)
        yc = _cc_sample(u3, cc_prev4, lp['cc_w'], lp['cc_b'], lp['cc_g'], lp['cc_beta'], layer=l)
        hs = _mixer_tail(h1, ys, yo, yc, lp, alpha, tm_s, tm_s)
        outs['ks'].append(k.reshape(ds, dec_len, DA_HEADS, HEAD_W))
        outs['vs'].append(v.reshape(ds, dec_len, DA_HEADS, HEAD_W))
        outs['cs'].append(xbc3)
        outs['us'].append(u3)

    y_prompt = hp.reshape(bp, lp_len, D_MODEL)[:, N_META:]
    y_sample = hs.reshape(ds, dec_len, D_MODEL)
    st = {k: jnp.stack(v) for k, v in outs.items()}
    k_prompt = kv_p[0].reshape(depth, bp, lp_len, DA_HEADS, HEAD_W)
    v_prompt = kv_p[1].reshape(depth, bp, lp_len, DA_HEADS, HEAD_W)
    ssm_sample = h1_buf.reshape(depth, ds, SSD_HEADS, SSD_HEAD_DIM, SSD_STATE)
    conv_sample = jnp.concatenate([state_ssd_conv.astype(F32), st['cs']], axis=2)[:, :, -(SSD_CONV - 1):]
    cc_sample = jnp.concatenate([state_conf_conv.astype(F32), st['us']], axis=2)[:, :, -(CC_KERNEL - 1):]
    return (y_prompt, y_sample, k_prompt, v_prompt, st['ks'], st['vs'], st['hp'], ssm_sample,
            st['cp'], conv_sample, st['up'], cc_sample)
```

```python
import functools
import math

import jax
import jax.numpy as jnp
from jax import lax
from jax.experimental import pallas as pl
from jax.experimental.pallas import tpu as pltpu

F32 = jnp.float32
BF16 = jnp.bfloat16

D_MODEL = 1024
N_META = 16
SSD_INNER = 2048
SSD_HEAD_DIM = 64
SSD_HEADS = 32
SSD_STATE = 128
SSD_GROUPS = 4
SSD_CONV = 4
SSD_CHUNK = 128
SSD_UNROLL = 8
SSD_XBC = SSD_INNER + 2 * SSD_GROUPS * SSD_STATE
GROUP_CH = SSD_INNER // SSD_GROUPS
HEADS_PER_GROUP = SSD_HEADS // SSD_GROUPS
DA_HEAD_DIM = 64
DA_HEADS = 8
DA_WIDTH = 1024
HEAD_W = 2 * DA_HEAD_DIM
ROT_DIM = DA_HEAD_DIM // 4
ROPE_THETA = 500000.0
Q_BLOCK = 256
PAGE_SIZE = 128
CC_CH = 1024
CC_KERNEL = 31
FF_HIDDEN = 2816
N_BRANCH = 3
IN_SIZES = (SSD_INNER, SSD_XBC, SSD_HEADS, DA_WIDTH, DA_WIDTH, DA_WIDTH, 2 * CC_CH, N_BRANCH * D_MODEL)
LN_EPS = 1e-5

VMEM_LIMIT_V7X = 52 * 1024 * 1024
LANES = 128
SAMPLE_BLOCK = 16
SAMPLE_ROWS = SAMPLE_BLOCK * 8


def _cparams(n_axes):
    return pltpu.CompilerParams(dimension_semantics=("arbitrary",) * n_axes,
                                vmem_limit_bytes=VMEM_LIMIT_V7X)


def _resident(shape):
    nd = len(shape)
    return pl.BlockSpec(shape, lambda *_: (0,) * nd, pipeline_mode=pl.Buffered(1))


def _dot(a, b):
    return jnp.dot(a, b, preferred_element_type=F32)


def _dot_nt(a, b):
    return lax.dot_general(a, b, (((1,), (1,)), ((), ())), preferred_element_type=F32)


def _dot_exact(a, b):
    return jnp.dot(a, b, preferred_element_type=F32, precision=lax.Precision.HIGHEST)


def _sigmoid(x):
    return 0.5 * jnp.tanh(0.5 * x) + 0.5


def _silu(x):
    return x * _sigmoid(x)


def _layer_norm(x, g, b):
    xc = x - jnp.mean(x, axis=-1, keepdims=True)
    var = jnp.mean(xc * xc, axis=-1, keepdims=True)
    return xc * lax.rsqrt(var + LN_EPS) * g + b


FF_CHUNK = 256


def _ffn_kernel(x_ref, wgu_ref, wd_ref, g_ref, b_ref, o_ref, acc_ref, *, alpha):
    x = x_ref[...]
    xb = x.astype(BF16)
    for j in range(FF_HIDDEN // FF_CHUNK):
        lo = j * FF_CHUNK
        gate = _dot(xb, wgu_ref[:, lo:lo + FF_CHUNK])
        up = _dot(xb, wgu_ref[:, FF_HIDDEN + lo:FF_HIDDEN + lo + FF_CHUNK])
        act = (_silu(gate) * up).astype(BF16)
        down = _dot(act, wd_ref[lo:lo + FF_CHUNK, :])
        if j == 0:
            acc_ref[...] = down
        else:
            acc_ref[...] += down
    o_ref[...] = _layer_norm(alpha * x + 0.5 * acc_ref[...], g_ref[...], b_ref[...])


def _ffn_ln(x, wgu, wd, g, b, *, tm, alpha):
    m = x.shape[0]
    row = pl.BlockSpec((tm, D_MODEL), lambda i: (i, 0))
    return pl.pallas_call(
        functools.partial(_ffn_kernel, alpha=alpha),
        grid=(m // tm,),
        in_specs=[row, _resident(wgu.shape), _resident(wd.shape), _resident(g.shape), _resident(b.shape)],
        out_specs=row,
        out_shape=jax.ShapeDtypeStruct((m, D_MODEL), F32),
        scratch_shapes=[pltpu.VMEM((tm, D_MODEL), F32)],
        compiler_params=_cparams(1),
    )(x, wgu, wd, g, b)


def _softplus(x):
    return jnp.maximum(x, 0.0) + jnp.log1p(jnp.exp(-jnp.abs(x)))


def _rope_slice(x, cos_t, sin_up, sin_dn):
    return (x * cos_t + pltpu.roll(x, LANES - ROT_DIM // 2, axis=1) * sin_up
            + pltpu.roll(x, ROT_DIM // 2, axis=1) * sin_dn)


def _proj_ssd_kernel(x_ref, wz_ref, wxbc_ref, wdt_ref, dtb_ref, z_ref, xbc_ref, dt_ref):
    xb = x_ref[...].astype(BF16)
    cw = 512
    for c in range(SSD_INNER // cw):
        z_ref[:, c * cw:(c + 1) * cw] = _dot(xb, wz_ref[:, c * cw:(c + 1) * cw])
    for c in range(SSD_XBC // cw):
        xbc_ref[:, c * cw:(c + 1) * cw] = _dot(xb, wxbc_ref[:, c * cw:(c + 1) * cw])
    dt_ref[...] = _softplus(_dot(xb, wdt_ref[...]) + dtb_ref[...])


def _proj_ssd(x, wz, wxbc, wdt, dtb, *, tm):
    m = x.shape[0]

    def row(n):
        return pl.BlockSpec((tm, n), lambda i: (i, 0))

    return pl.pallas_call(
        _proj_ssd_kernel,
        grid=(m // tm,),
        in_specs=[row(D_MODEL), _resident(wz.shape), _resident(wxbc.shape), _resident(wdt.shape),
                  _resident(dtb.shape)],
        out_specs=[row(SSD_INNER), row(SSD_XBC), row(SSD_GROUPS * LANES)],
        out_shape=[jax.ShapeDtypeStruct((m, SSD_INNER), F32), jax.ShapeDtypeStruct((m, SSD_XBC), F32),
                   jax.ShapeDtypeStruct((m, SSD_GROUPS * LANES), F32)],
        compiler_params=_cparams(1),
    )(x, wz, wxbc, wdt, dtb)


def _proj_ssd_conv_kernel(x_ref, wz_ref, wxbc_ref, wdt_ref, dtb_ref, cw_ref, cb_ref,
                          zs_ref, xs_ref, bc_ref, dt_ref, tail_ref, win_ref, carry_ref, *, tiles_per_seq):
    tm = x_ref.shape[0]
    split = (tm // 2 + 15) // 16 * 16
    first = 8 - (SSD_CONV - 1)
    xb = x_ref[...].astype(BF16)

    @pl.when(pl.program_id(0) % tiles_per_seq == 0)
    def _():
        carry_ref[...] = jnp.zeros_like(carry_ref)

    cw = 512
    n_xbc, n_z = SSD_XBC // cw, SSD_INNER // cw

    def project(c):
        cols = slice(c * cw, (c + 1) * cw)
        win_ref[c % 2, 0:8, :] = carry_ref[:, cols]
        win_ref[c % 2, 8:8 + tm, :] = _dot(xb, wxbc_ref[:, cols])

    def conv(c):
        win = win_ref.at[c % 2]
        for j in range(cw // LANES):
            lanes = slice(j * LANES, (j + 1) * LANES)
            col = c * cw + j * LANES
            for r0, n in ((0, split), (split, tm - split)):
                taps = [win[first + k + r0:first + k + r0 + n, lanes] for k in range(SSD_CONV)]
                act = _conv_silu(taps, cw_ref[:, :, col:col + LANES], cb_ref[:, col:col + LANES])
                if col < SSD_INNER:
                    xs_ref[r0:r0 + n, col:col + LANES] = act
                else:
                    bc_ref[r0:r0 + n, col - SSD_INNER:col - SSD_INNER + LANES] = act.astype(BF16)
        tail = win[tm:tm + 8, :]
        tail_ref[:, c * cw:(c + 1) * cw] = tail
        carry_ref[:, c * cw:(c + 1) * cw] = tail

    project(0)
    for c in range(n_xbc):
        if c + 1 < n_xbc:
            project(c + 1)
        if c < n_z:
            zs_ref[:, c * cw:(c + 1) * cw] = _silu(_dot(xb, wz_ref[:, c * cw:(c + 1) * cw])).astype(BF16)
        conv(c)
    dt_ref[...] = _softplus(_dot(xb, wdt_ref[...]) + dtb_ref[...])


def _proj_ssd_conv(x, wz, wxbc, wdt, dtb, conv_w, conv_b, *, tm, tiles_per_seq):
    m = x.shape[0]
    assert tm % 16 == 0

    def row(n):
        return pl.BlockSpec((tm, n), lambda i: (i, 0))

    consts = (wz, wxbc, wdt, dtb, conv_w, conv_b)
    return pl.pallas_call(
        functools.partial(_proj_ssd_conv_kernel, tiles_per_seq=tiles_per_seq),
        grid=(m // tm,),
        in_specs=[row(D_MODEL)] + [_resident(c.shape) for c in consts],
        out_specs=[row(SSD_INNER), row(SSD_INNER), row(2 * SSD_GROUPS * SSD_STATE), row(SSD_GROUPS * LANES),
                   pl.BlockSpec((8, SSD_XBC), lambda i: (i // tiles_per_seq, 0))],
        out_shape=[jax.ShapeDtypeStruct((m, SSD_INNER), BF16), jax.ShapeDtypeStruct((m, SSD_INNER), F32),
                   jax.ShapeDtypeStruct((m, 2 * SSD_GROUPS * SSD_STATE), BF16),
                   jax.ShapeDtypeStruct((m, SSD_GROUPS * LANES), F32),
                   jax.ShapeDtypeStruct((m // (tm * tiles_per_seq) * 8, SSD_XBC), F32)],
        scratch_shapes=[pltpu.VMEM((2, 8 + tm, 512), F32), pltpu.VMEM((8, SSD_XBC), F32)],
        compiler_params=_cparams(1),
    )(x, *consts)


def _proj_mix_kernel(x_ref, cos_ref, sup_ref, sdn_ref, wq_ref, wk_ref, wv_ref, wga_ref, wgb_ref,
                     q_ref, k_ref, v_ref, u_ref):
    xb = x_ref[...].astype(BF16)
    cos_t, sin_up, sin_dn = cos_ref[...], sup_ref[...], sdn_ref[...]
    cw = 256
    for c in range(DA_WIDTH // cw):
        lo = c * cw
        qc = _dot(xb, wq_ref[:, lo:lo + cw])
        kc = _dot(xb, wk_ref[:, lo:lo + cw])
        for s in range(cw // HEAD_W):
            sl = slice(s * HEAD_W, (s + 1) * HEAD_W)
            dst = slice(lo + s * HEAD_W, lo + (s + 1) * HEAD_W)
            q_ref[:, dst] = (_rope_slice(qc[:, sl], cos_t, sin_up, sin_dn)
                             * (DA_HEAD_DIM ** -0.5)).astype(q_ref.dtype)
            k_ref[:, dst] = _rope_slice(kc[:, sl], cos_t, sin_up, sin_dn)
        v_ref[:, lo:lo + cw] = _dot(xb, wv_ref[:, lo:lo + cw])
        ga = _dot(xb, wga_ref[:, lo:lo + cw])
        gb = _dot(xb, wgb_ref[:, lo:lo + cw])
        u_ref[:, lo:lo + cw] = ga * _sigmoid(gb)


def _proj_mix_cc_kernel(x_ref, cos_ref, sup_ref, sdn_ref, wq_ref, wk_ref, wv_ref, wga_ref, wgb_ref,
                        ccw_ref, ccb_ref, g_ref, beta_ref, q_ref, k_ref, v_ref, c_ref, utail_ref,
                        win_ref, conv_ref, *, tiles_per_seq):
    tm = x_ref.shape[0]
    xb = x_ref[...].astype(BF16)
    cos_t, sin_up, sin_dn = cos_ref[...], sup_ref[...], sdn_ref[...]

    @pl.when(pl.program_id(0) % tiles_per_seq == 0)
    def _():
        win_ref[0:CC_HIST, :] = jnp.zeros((CC_HIST, CC_CH), F32)

    pieces = [(r, min(CC_PIECE, tm - r)) for r in range(0, tm, CC_PIECE)]

    def conv(lo, hi):
        for lb in range(lo // LANES, hi // LANES):
            lanes = slice(lb * LANES, (lb + 1) * LANES)
            for r0, n in pieces:
                conv_ref[r0:r0 + n, lanes] = _cc_taps(
                    lambda start, span: win_ref[r0 + start:r0 + start + span, lanes], ccw_ref, ccb_ref, lanes, n)

    cw = 256
    for c in range(DA_WIDTH // cw):
        lo = c * cw
        qc = _dot(xb, wq_ref[:, lo:lo + cw])
        kc = _dot(xb, wk_ref[:, lo:lo + cw])
        for s in range(cw // HEAD_W):
            sl = slice(s * HEAD_W, (s + 1) * HEAD_W)
            dst = slice(lo + s * HEAD_W, lo + (s + 1) * HEAD_W)
            q_ref[:, dst] = (_rope_slice(qc[:, sl], cos_t, sin_up, sin_dn)
                             * (DA_HEAD_DIM ** -0.5)).astype(q_ref.dtype)
            k_ref[:, dst] = _rope_slice(kc[:, sl], cos_t, sin_up, sin_dn)
        v_ref[:, lo:lo + cw] = _dot(xb, wv_ref[:, lo:lo + cw])
        ga = _dot(xb, wga_ref[:, lo:lo + cw])
        gb = _dot(xb, wgb_ref[:, lo:lo + cw])
        win_ref[CC_HIST:CC_HIST + tm, lo:lo + cw] = ga * _sigmoid(gb)
        if c > 0:
            conv(lo - cw, lo)
    conv(DA_WIDTH - cw, DA_WIDTH)
    c_ref[...] = _silu(_layer_norm(conv_ref[...], g_ref[...], beta_ref[...])).astype(c_ref.dtype)
    tail = win_ref[tm:tm + CC_HIST, :]
    utail_ref[...] = tail
    win_ref[0:CC_HIST, :] = tail


def _proj_mix_cc(x, tables, wq, wk, wv, wga, wgb, ccw, ccb, g, beta, *, tm, tiles_per_seq):
    m = x.shape[0]
    n_tab = tables[0].shape[0] // tm
    row = pl.BlockSpec((tm, D_MODEL), lambda i: (i, 0))
    tab = pl.BlockSpec((tm, LANES), lambda i: (i % n_tab, 0))
    w = _resident(wq.shape)
    consts = (ccw, ccb, g, beta)
    return pl.pallas_call(
        functools.partial(_proj_mix_cc_kernel, tiles_per_seq=tiles_per_seq),
        grid=(m // tm,),
        in_specs=[row, tab, tab, tab, w, w, w, w, w] + [_resident(c.shape) for c in consts],
        out_specs=[row, row, row, row, pl.BlockSpec((CC_HIST, CC_CH), lambda i: (i // tiles_per_seq, 0))],
        out_shape=[jax.ShapeDtypeStruct((m, D_MODEL), BF16), jax.ShapeDtypeStruct((m, D_MODEL), F32),
                   jax.ShapeDtypeStruct((m, D_MODEL), F32), jax.ShapeDtypeStruct((m, CC_CH), BF16),
                   jax.ShapeDtypeStruct((m // (tm * tiles_per_seq) * CC_HIST, CC_CH), F32)],
        scratch_shapes=[pltpu.VMEM((CC_HIST + tm, CC_CH), F32), pltpu.VMEM((tm, CC_CH), F32)],
        compiler_params=_cparams(1),
    )(x, *tables, wq, wk, wv, wga, wgb, *consts)


def _proj_mix(x, tables, wq, wk, wv, wga, wgb, *, tm, q_dtype):
    m = x.shape[0]
    n_tab = tables[0].shape[0] // tm
    row = pl.BlockSpec((tm, D_MODEL), lambda i: (i, 0))
    tab = pl.BlockSpec((tm, LANES), lambda i: (i % n_tab, 0))
    w = _resident(wq.shape)
    return pl.pallas_call(
        _proj_mix_kernel,
        grid=(m // tm,),
        in_specs=[row, tab, tab, tab, w, w, w, w, w],
        out_specs=[row, row, row, row],
        out_shape=[jax.ShapeDtypeStruct((m, D_MODEL), q_dtype)] + [jax.ShapeDtypeStruct((m, D_MODEL), F32)] * 3,
        compiler_params=_cparams(1),
    )(x, *tables, wq, wk, wv, wga, wgb)


def _merge_kernel(h_ref, ys_ref, yo_ref, yc_ref, wgate_ref, bgate_ref, wssd_ref, wda_ref, wcc_ref, wout_ref,
                  g_ref, b_ref, o_ref, *, alpha):
    h = h_ref[...]
    hb = h.astype(BF16)
    merged = None
    for i, (y_ref, w_ref) in enumerate(((ys_ref, wssd_ref), (yo_ref, wda_ref), (yc_ref, wcc_ref))):
        sl = slice(i * D_MODEL, (i + 1) * D_MODEL)
        gate = _sigmoid(_dot(hb, wgate_ref[:, sl]) + bgate_ref[:, sl])
        term = gate * _dot(y_ref[...].astype(BF16), w_ref[...])
        merged = term if merged is None else merged + term
    mix = _dot(merged.astype(BF16), wout_ref[...])
    o_ref[...] = _layer_norm(alpha * h + mix, g_ref[...], b_ref[...])


def _merge(h, ys, yo, yc, wgate, bgate, wssd, wda, wcc, wout, g, b, *, tm, alpha):
    m = h.shape[0]

    def row(n):
        return pl.BlockSpec((tm, n), lambda i: (i, 0))

    consts = (wgate, bgate, wssd, wda, wcc, wout, g, b)
    return pl.pallas_call(
        functools.partial(_merge_kernel, alpha=alpha),
        grid=(m // tm,),
        in_specs=[row(D_MODEL), row(SSD_INNER), row(DA_WIDTH), row(CC_CH)] + [_resident(c.shape) for c in consts],
        out_specs=row(D_MODEL),
        out_shape=jax.ShapeDtypeStruct((m, D_MODEL), F32),
        compiler_params=_cparams(1),
    )(h, ys, yo, yc, *consts)


def _pair_cols(x, h0, h1, lane_lo):
    shape = (x.shape[0], LANES)
    return jnp.where(lane_lo[:x.shape[0]], jnp.broadcast_to(x[:, h0:h0 + 1], shape),
                     jnp.broadcast_to(x[:, h1:h1 + 1], shape))


def _ssd_intra(xs, bc, cc, dt, acol, mask, lane_lo):
    arow = acol.T
    bb = bc.astype(BF16)
    cb = _dot_nt(cc.astype(BF16), bb)
    y_in, xdts, aces = [], [], []
    for j in range(HEADS_PER_GROUP // 2):
        h0, h1 = 2 * j, 2 * j + 1
        gs, wide = [], []
        for h in (h0, h1):
            a_h = jnp.broadcast_to(acol[:, h:h + 1], acol.shape)
            wide.append(a_h)
            gs.append((cb * jnp.exp(jnp.where(mask, a_h - arow[h:h + 1, :], -jnp.inf))).astype(BF16))
        gp = jnp.concatenate(gs, axis=1)
        xdt = xs[:, j * LANES:(j + 1) * LANES] * _pair_cols(dt, h0, h1, lane_lo)
        zero = jnp.zeros_like(xdt)
        xbd = jnp.concatenate([jnp.where(lane_lo, xdt, zero), jnp.where(lane_lo, zero, xdt)], axis=0).astype(BF16)
        y_in.append(_dot(gp, xbd))
        xdts.append(xdt)
        aces.append(jnp.where(lane_lo, wide[0], wide[1]))
    return y_in, xdts, aces, bb


def _ssd_finish(ys, xs, gate, d_row, nw_row):
    gated, ss = [], None
    for j, y in enumerate(ys):
        sl = slice(j * LANES, (j + 1) * LANES)
        yz = (y + d_row[:, sl] * xs[:, sl]) * gate[:, sl]
        gated.append(yz)
        ss = yz * yz if ss is None else ss + yz * yz
    rs = lax.rsqrt(jnp.sum(ss, axis=1, keepdims=True) * (1.0 / GROUP_CH) + LN_EPS)
    return [(yz * rs * nw_row[:, j * LANES:(j + 1) * LANES]).astype(BF16) for j, yz in enumerate(gated)]


def _conv_silu(taps, w_ref, b_ref):
    rows, ch = taps[0].shape
    acc = jnp.broadcast_to(b_ref[...][None], (rows // 8, 8, ch))
    for k in range(SSD_CONV):
        acc = acc + w_ref[k][None] * taps[k].reshape(rows // 8, 8, ch)
    return _silu(acc).reshape(rows, ch)


def _ssd_prompt_kernel(xs_ref, bm_ref, cm_ref, zs_ref, dt_ref, a_ref, d_ref, nw_ref, y_ref, hfin_ref, state_ref):
    L = SSD_CHUNK
    rows = lax.broadcasted_iota(jnp.int32, (L, LANES), 0)
    cols = lax.broadcasted_iota(jnp.int32, (L, LANES), 1)
    causal = rows >= cols
    tri = causal.astype(F32)
    lane_lo = cols < SSD_HEAD_DIM
    row_lo = rows < SSD_HEAD_DIM
    state_ref[...] = jnp.zeros_like(state_ref)

    def local_part(r0, meta):
        xs = xs_ref[pl.ds(r0, L), :]
        bc = bm_ref[pl.ds(r0, L), :]
        cc = cm_ref[pl.ds(r0, L), :]
        dt = dt_ref[pl.ds(r0, L), :]
        if meta:
            dt = jnp.where(rows < N_META, dt, 0.0)
        acol = _dot_exact(tri, dt * a_ref[...])
        alast = acol[L - 1:L, :]
        y_in, xdts, aces, bb = _ssd_intra(xs, bc, cc, dt, acol, causal, lane_lo)
        pairs = []
        for j in range(HEADS_PER_GROUP // 2):
            h0, h1 = 2 * j, 2 * j + 1
            ace = aces[j]
            ale = jnp.where(lane_lo[0:1], jnp.broadcast_to(alast[:, h0:h0 + 1], (1, LANES)),
                            jnp.broadcast_to(alast[:, h1:h1 + 1], (1, LANES)))
            xw = xdts[j] * jnp.exp(ale - ace)
            upd = _dot(xw.T.astype(BF16), bb)
            dec = jnp.where(row_lo, jnp.broadcast_to(jnp.exp(alast[:, h0:h0 + 1]), (L, LANES)),
                            jnp.broadcast_to(jnp.exp(alast[:, h1:h1 + 1]), (L, LANES)))
            pairs.append((y_in[j], jnp.exp(ace), upd, dec))
        return xs, cc.astype(BF16), pairs

    def state_part(r0, xs, cb16, pairs):
        ys = []
        for j, (y_loc, eace, upd, dec) in enumerate(pairs):
            st = state_ref[j * LANES:(j + 1) * LANES, :]
            ys.append(y_loc + _dot_nt(cb16, st.astype(BF16)) * eace)
            state_ref[j * LANES:(j + 1) * LANES, :] = st * dec + upd
        outs = _ssd_finish(ys, xs, zs_ref[pl.ds(r0, L), :].astype(F32), d_ref[...], nw_ref[...])
        for j, o in enumerate(outs):
            y_ref[pl.ds(r0, L), j * LANES:(j + 1) * LANES] = o

    state_part(0, *local_part(0, True))

    def body(c, carry):
        starts = [pl.multiple_of(N_META + (SSD_UNROLL * c + i) * L, 16) for i in range(SSD_UNROLL)]
        local = [local_part(r, False) for r in starts]
        for r, loc in zip(starts, local):
            state_part(r, *loc)
        return carry

    n_chunks = (y_ref.shape[0] - N_META) // L
    assert n_chunks % SSD_UNROLL == 0
    lax.fori_loop(0, n_chunks // SSD_UNROLL, body, 0)
    hfin_ref[...] = state_ref[...]


def _ssd_prompt(xs, bcm, zs, dt, a4, d_e, nw, *, batch, seq):
    def seq_block(width, col0=0):
        return pl.BlockSpec((seq, width), lambda b, g: (b, g + col0))

    def par_block(rows, width):
        return pl.BlockSpec((rows, width), lambda b, g: (0, g))

    return pl.pallas_call(
        _ssd_prompt_kernel,
        grid=(batch, SSD_GROUPS),
        in_specs=[seq_block(GROUP_CH), seq_block(SSD_STATE), seq_block(SSD_STATE, SSD_GROUPS),
                  seq_block(GROUP_CH), seq_block(LANES),
                  par_block(1, LANES), par_block(1, GROUP_CH), par_block(1, GROUP_CH)],
        out_specs=[seq_block(GROUP_CH), pl.BlockSpec((None, GROUP_CH, SSD_STATE), lambda b, g: (b, g, 0))],
        out_shape=[jax.ShapeDtypeStruct((batch * seq, SSD_INNER), BF16),
                   jax.ShapeDtypeStruct((batch, SSD_INNER, SSD_STATE), F32)],
        scratch_shapes=[pltpu.VMEM((GROUP_CH, SSD_STATE), F32)],
        compiler_params=_cparams(2),
    )(xs, bcm, bcm, zs, dt, a4, d_e, nw)


def _ssd_sample_kernel(xs_ref, bm_ref, cm_ref, px_ref, pb_ref, pc_ref, z_ref, dt_ref, h0_ref,
                       wx_ref, wb_ref, wc_ref, bx_ref, bb_ref, bc_ref, a_ref, d_ref, nw_ref, *rest, n_prev):
    y_ref, h1_ref, ext_ref, flat_ref, xwt_ref, yst_ref = rest[-6:]
    if n_prev:
        h1_ref[0:n_prev] = rest[0][...]
    R = SAMPLE_ROWS
    rows = lax.broadcasted_iota(jnp.int32, (R, LANES), 0)
    cols = lax.broadcasted_iota(jnp.int32, (R, LANES), 1)
    same = (rows >> 3) == (cols >> 3)
    causal = same & (rows >= cols)
    lane_lo = cols < SSD_HEAD_DIM
    srcs = ((xs_ref, px_ref, 0, GROUP_CH), (bm_ref, pb_ref, GROUP_CH, GROUP_CH + SSD_STATE),
            (cm_ref, pc_ref, GROUP_CH + SSD_STATE, GROUP_CH + 2 * SSD_STATE))
    for cur, prev, lo, hi in srcs:
        ext_ref[:, 0:8, lo:hi] = prev[...]
        ext_ref[:, 8:16, lo:hi] = cur[...]
    first = 8 - (SSD_CONV - 1)
    for k in range(SSD_CONV):
        flat_ref[k] = ext_ref[:, first + k:first + k + 8, :].reshape(R, GROUP_CH + 2 * SSD_STATE)

    def conv(lo, hi, w_ref, b_ref):
        return _conv_silu([flat_ref[k, :, lo:hi] for k in range(SSD_CONV)], w_ref, b_ref)

    xs = conv(0, GROUP_CH, wx_ref, bx_ref)
    bc = conv(GROUP_CH, GROUP_CH + SSD_STATE, wb_ref, bb_ref)
    cc = conv(GROUP_CH + SSD_STATE, GROUP_CH + 2 * SSD_STATE, wc_ref, bc_ref)
    dt = dt_ref[...]
    dta = dt * a_ref[...]
    acol = _dot_exact(causal.astype(F32), dta)
    total = _dot_exact(same.astype(F32), dta)
    y_in, xdts, aces, _ = _ssd_intra(xs, bc, cc, dt, acol, causal, lane_lo)
    for j in range(HEADS_PER_GROUP // 2):
        xw = xdts[j] * jnp.exp(_pair_cols(total, 2 * j, 2 * j + 1, lane_lo) - aces[j])
        xwt_ref[j * LANES:(j + 1) * LANES, :] = xw.T.astype(BF16)
    xwt = xwt_ref[...]
    for i in range(SAMPLE_BLOCK):
        r = slice(8 * i, 8 * i + 8)
        st = h0_ref[i]
        c16 = jnp.concatenate([cc[r], jnp.zeros((8, SSD_STATE), F32)], axis=0).astype(BF16)
        yst_ref[r, :] = _dot_nt(c16, st.astype(BF16))[0:8]
        b_i = jnp.where((rows >> 3) == i, bc, 0.0).astype(BF16)
        upd = _dot(xwt, b_i)
        tot_i = total[8 * i:8 * i + 1, :]
        for h in range(HEADS_PER_GROUP):
            hs = slice(h * SSD_HEAD_DIM, (h + 1) * SSD_HEAD_DIM)
            h1_ref[n_prev, i, hs, :] = st[hs] * jnp.exp(tot_i[:, h:h + 1]) + upd[hs]
    ys = [y_in[j] + yst_ref[:, j * LANES:(j + 1) * LANES] * jnp.exp(aces[j])
          for j in range(HEADS_PER_GROUP // 2)]
    outs = _ssd_finish(ys, xs, _silu(z_ref[...]), d_ref[...], nw_ref[...])
    for j, o in enumerate(outs):
        y_ref[:, j * LANES:(j + 1) * LANES] = o


def _ssd_sample(xbc3, prev4, z, dt, h0_all, conv_w, conv_b, a4, d_e, nw, *, layer, prev_h1=None):
    n_req = xbc3.shape[0]
    n_prev = 0 if prev_h1 is None else prev_h1.shape[0]
    g_b = SSD_INNER // SSD_STATE
    R = SAMPLE_ROWS
    width = GROUP_CH + 2 * SSD_STATE

    def tok_block(w, col0=0):
        return pl.BlockSpec((SAMPLE_BLOCK, 8, w), lambda i, g: (i, 0, g + col0))

    def prev_block(w, col0=0):
        return pl.BlockSpec((None, SAMPLE_BLOCK, 8, w), lambda i, g: (layer, i, 0, g + col0))

    def row_block(w):
        return pl.BlockSpec((R, w), lambda i, g: (i, g))

    def par_block(rows, w, col0=0):
        return pl.BlockSpec((rows, w), lambda i, g: (0, g + col0))

    def tap_block(w, col0=0):
        return pl.BlockSpec((SSD_CONV, 8, w), lambda i, g: (0, 0, g + col0))

    state = pl.BlockSpec((None, SAMPLE_BLOCK, GROUP_CH, SSD_STATE), lambda i, g: (layer, i, g, 0))
    toks = [tok_block(GROUP_CH), tok_block(SSD_STATE, g_b), tok_block(SSD_STATE, g_b + SSD_GROUPS)]
    prevs = [prev_block(GROUP_CH), prev_block(SSD_STATE, g_b), prev_block(SSD_STATE, g_b + SSD_GROUPS)]
    in_specs = toks + prevs + [row_block(GROUP_CH), row_block(LANES), state,
                               tap_block(GROUP_CH), tap_block(SSD_STATE, g_b),
                               tap_block(SSD_STATE, g_b + SSD_GROUPS),
                               par_block(8, GROUP_CH), par_block(8, SSD_STATE, g_b),
                               par_block(8, SSD_STATE, g_b + SSD_GROUPS),
                               par_block(1, LANES), par_block(1, GROUP_CH), par_block(1, GROUP_CH)]
    args = [xbc3, xbc3, xbc3, prev4, prev4, prev4, z, dt, h0_all, conv_w, conv_w, conv_w, conv_b, conv_b, conv_b,
            a4, d_e, nw]
    def stack_block(n):
        return pl.BlockSpec((n, SAMPLE_BLOCK, GROUP_CH, SSD_STATE), lambda i, g: (0, i, g, 0))

    if n_prev:
        in_specs = in_specs + [stack_block(n_prev)]
        args = args + [prev_h1]
    return pl.pallas_call(
        functools.partial(_ssd_sample_kernel, n_prev=n_prev),
        grid=(n_req // SAMPLE_BLOCK, SSD_GROUPS),
        in_specs=in_specs,
        out_specs=[row_block(GROUP_CH), stack_block(n_prev + 1)],
        out_shape=[jax.ShapeDtypeStruct((n_req * 8, SSD_INNER), BF16),
                   jax.ShapeDtypeStruct((n_prev + 1, n_req, SSD_INNER, SSD_STATE), F32)],
        scratch_shapes=[pltpu.VMEM((SAMPLE_BLOCK, 16, width), F32), pltpu.VMEM((SSD_CONV, R, width), F32),
                        pltpu.VMEM((GROUP_CH, R), BF16), pltpu.VMEM((R, GROUP_CH), F32)],
        compiler_params=_cparams(2),
    )(*args)


def _sub_ln(o, w_row, post_scale):
    return o * lax.rsqrt(jnp.mean(o * o, axis=-1, keepdims=True) + LN_EPS) * w_row * post_scale


def _split_components(q):
    lane_lo = lax.broadcasted_iota(jnp.int32, q.shape, 1) < DA_HEAD_DIM
    zero = jnp.zeros_like(q)
    return jnp.concatenate([jnp.where(lane_lo, q, zero), jnp.where(lane_lo, zero, q)], axis=0)


def _attn_prompt_kernel(lam_ref, q_ref, k_ref, v_ref, w_ref, *rest, n_prev, post_scale):
    kb_ref, vb_ref = rest[-2:]
    if n_prev:
        kp_ref, vp_ref, o_ref, kall_ref, vall_ref = rest[:5]
        kall_ref[0:n_prev] = kp_ref[...]
        vall_ref[0:n_prev] = vp_ref[...]
        kall_ref[n_prev] = k_ref[...]
        vall_ref[n_prev] = v_ref[...]
    else:
        o_ref = rest[0]
    lam = lam_ref[0]
    kb_ref[...] = k_ref[...].astype(BF16)
    vb_ref[:, 0:HEAD_W] = v_ref[...].astype(BF16)
    vb_ref[:, HEAD_W:2 * HEAD_W] = jnp.ones((vb_ref.shape[0], HEAD_W), BF16)
    k_meta = kb_ref[0:LANES]
    v_meta = vb_ref[0:LANES]
    w_row = w_ref[...]

    def attend(r0, n_q, parts):
        finish(r0, n_q, soft(parts))

    def soft(parts):
        m = None
        for s, _ in parts:
            pm = jnp.max(s, axis=1, keepdims=True)
            m = pm if m is None else jnp.maximum(m, pm)
        return [(jnp.exp((s - m).astype(BF16)), vals) for s, vals in parts]

    def finish(r0, n_q, probs):
        acc = None
        for p, vals in probs:
            term = _dot(p, vals)
            acc = term if acc is None else acc + term
        on = acc[:, 0:HEAD_W] * (1.0 / acc[:, HEAD_W:HEAD_W + 1])
        o = on[0:n_q] - lam * on[n_q:2 * n_q]
        o_ref[r0:r0 + n_q, :] = _sub_ln(o, w_row, post_scale).astype(o_ref.dtype)

    q2 = _split_components(q_ref[0:N_META])
    s = _dot_nt(q2, k_meta)
    qpos = lax.broadcasted_iota(jnp.int32, s.shape, 0) & (N_META - 1)
    kpos = lax.broadcasted_iota(jnp.int32, s.shape, 1)
    attend(0, N_META, [(jnp.where(kpos <= qpos, s, -jnp.inf), v_meta)])

    n_blk = (q_ref.shape[0] - N_META) // Q_BLOCK
    blk_shape = (2 * Q_BLOCK, Q_BLOCK)
    kcol = lax.broadcasted_iota(jnp.int32, blk_shape, 1)
    meta_ok = lax.broadcasted_iota(jnp.int32, (2 * Q_BLOCK, LANES), 1) < N_META
    diag_ok = kcol <= (lax.broadcasted_iota(jnp.int32, blk_shape, 0) & (Q_BLOCK - 1))

    def scores(i):
        r0 = N_META + i * Q_BLOCK
        q2 = _split_components(q_ref[r0:r0 + Q_BLOCK])
        parts = [(jnp.where(meta_ok, _dot_nt(q2, k_meta), -jnp.inf), v_meta)]
        if i > 0:
            parts.append((_dot_nt(q2, kb_ref[N_META:r0]), vb_ref[N_META:r0]))
        parts.append((jnp.where(diag_ok, _dot_nt(q2, kb_ref[r0:r0 + Q_BLOCK]), -jnp.inf),
                      vb_ref[r0:r0 + Q_BLOCK]))
        return parts

    nxt = scores(0)
    for i in range(n_blk):
        cur, nxt = nxt, (scores(i + 1) if i + 1 < n_blk else None)
        attend(N_META + i * Q_BLOCK, Q_BLOCK, cur)


def _attn_prompt(lam, q, k, v, w_row, *, batch, seq, post_scale, prev_kv=None):
    blk = pl.BlockSpec((seq, HEAD_W), lambda b, h: (b, h))
    in_specs = [pl.BlockSpec(memory_space=pltpu.SMEM), blk, blk, blk, _resident(w_row.shape)]
    args = [lam, q, k, v, w_row]
    out_specs = [blk]
    out_shape = [jax.ShapeDtypeStruct((batch * seq, DA_WIDTH), BF16)]
    n_prev = 0
    if prev_kv is not None:
        n_prev = prev_kv[0].shape[0]
        in_specs += [pl.BlockSpec((n_prev, seq, HEAD_W), lambda b, h: (0, b, h))] * 2
        args += list(prev_kv)
        out_specs += [pl.BlockSpec((n_prev + 1, seq, HEAD_W), lambda b, h: (0, b, h))] * 2
        out_shape += [jax.ShapeDtypeStruct((n_prev + 1, batch * seq, DA_WIDTH), F32)] * 2
    return pl.pallas_call(
        functools.partial(_attn_prompt_kernel, n_prev=n_prev, post_scale=post_scale),
        grid=(batch, DA_HEADS),
        in_specs=in_specs,
        out_specs=out_specs,
        out_shape=out_shape,
        scratch_shapes=[pltpu.VMEM((seq, HEAD_W), BF16), pltpu.VMEM((seq, 2 * HEAD_W), BF16)],
        compiler_params=_cparams(2),
    )(*args)


def _attn_sample_kernel(pt_ref, lam_ref, q_ref, kn_ref, vn_ref, w_ref, *refs, n_pages, post_scale):
    del pt_ref
    k_pages = refs[:n_pages]
    v_pages = refs[n_pages:2 * n_pages]
    o_ref, s_ref = refs[2 * n_pages], refs[2 * n_pages + 1]
    lam = lam_ref[0]
    q = q_ref[...]
    lane = lax.broadcasted_iota(jnp.int32, (8, HEAD_W), 1)
    kpos = lax.broadcasted_iota(jnp.int32, (16, PAGE_SIZE), 1)
    new_ok = kpos <= (lax.broadcasted_iota(jnp.int32, (16, PAGE_SIZE), 0) & 7)
    pad = jnp.zeros((PAGE_SIZE - 8, HEAD_W), F32)
    w_row = w_ref[...]

    def head_rows(ref, h, n):
        return ref[pl.ds(h, n, stride=DA_HEADS), :]

    def keys(h, p):
        if p < n_pages:
            return head_rows(k_pages[p], h, PAGE_SIZE).astype(BF16)
        return jnp.concatenate([head_rows(kn_ref, h, 8), pad], axis=0).astype(BF16)

    def values(h, p):
        if p < n_pages:
            return head_rows(v_pages[p], h, PAGE_SIZE).astype(BF16)
        return jnp.concatenate([head_rows(vn_ref, h, 8), pad], axis=0).astype(BF16)

    def first_pass(h):
        qh = q[:, h * HEAD_W:(h + 1) * HEAD_W]
        q2 = jnp.concatenate([jnp.where(lane < DA_HEAD_DIM, qh, 0.0), jnp.where(lane >= DA_HEAD_DIM, qh, 0.0)],
                             axis=0).astype(BF16)
        m = None
        for p in range(n_pages + 1):
            s = _dot_nt(q2, keys(h, p))
            if p == n_pages:
                s = jnp.where(new_ok, s, -jnp.inf)
            s_ref[h, :, p * PAGE_SIZE:(p + 1) * PAGE_SIZE] = s
            pm = jnp.max(s, axis=1, keepdims=True)
            m = pm if m is None else jnp.maximum(m, pm)
        return m

    def second_pass(h, m):
        den, acc = None, None
        for p in range(n_pages + 1):
            e = jnp.exp(s_ref[h, :, p * PAGE_SIZE:(p + 1) * PAGE_SIZE] - m)
            pd = jnp.sum(e, axis=1, keepdims=True)
            term = _dot(e.astype(BF16), values(h, p))
            den = pd if den is None else den + pd
            acc = term if acc is None else acc + term
        on = acc * (1.0 / den)
        o_ref[:, h * HEAD_W:(h + 1) * HEAD_W] = _sub_ln(on[0:8] - lam * on[8:16], w_row, post_scale)

    m_next = first_pass(0)
    for h in range(DA_HEADS):
        m_cur, m_next = m_next, (first_pass(h + 1) if h + 1 < DA_HEADS else None)
        second_pass(h, m_cur)


def _attn_sample(page_table_flat, lam, q, k_new, v_new, w_row, cache_k, cache_v, *, layer, post_scale):
    n_req = q.shape[0] // 8
    n_pages = page_table_flat.shape[0] // n_req
    page_rows = PAGE_SIZE * DA_HEADS

    def page_spec(j):
        return pl.BlockSpec((None, None, page_rows, HEAD_W), lambda b, pt: (layer, pt[b * n_pages + j], 0, 0))

    tok = pl.BlockSpec((8, DA_WIDTH), lambda b, pt: (b, 0))
    new = pl.BlockSpec((None, 8 * DA_HEADS, HEAD_W), lambda b, pt: (b, 0, 0))
    pages = [page_spec(j) for j in range(n_pages)]
    grid_spec = pltpu.PrefetchScalarGridSpec(
        num_scalar_prefetch=1,
        grid=(n_req,),
        in_specs=[pl.BlockSpec(memory_space=pltpu.SMEM), tok, new, new,
                  pl.BlockSpec(w_row.shape, lambda b, pt: (0, 0))] + pages + pages,
        out_specs=tok,
        scratch_shapes=[pltpu.VMEM((DA_HEADS, 16, (n_pages + 1) * PAGE_SIZE), F32)],
    )
    return pl.pallas_call(
        functools.partial(_attn_sample_kernel, n_pages=n_pages, post_scale=post_scale),
        grid_spec=grid_spec,
        out_shape=jax.ShapeDtypeStruct((n_req * 8, DA_WIDTH), F32),
        compiler_params=_cparams(1),
    )(page_table_flat, lam, q, k_new, v_new, w_row, *([cache_k] * n_pages), *([cache_v] * n_pages))


CC_HIST = 32
CC_PIECE = 96


def _cc_taps(load_window, w_ref, b_ref, lanes, rows):
    first = CC_HIST - (CC_KERNEL - 1)
    nt = rows // 8
    acc = jnp.broadcast_to(b_ref[:, lanes][None], (nt, 8, LANES))
    for phase in range(8):
        n_tap = (CC_KERNEL - 1 - phase) // 8 + 1
        span = rows + 8 * (n_tap - 1)
        win = load_window(first + phase, span).reshape(span // 8, 8, LANES)
        for a in range(n_tap):
            acc = acc + w_ref[8 * a + phase, :, lanes][None] * win[a:a + nt]
    return acc.reshape(rows, LANES)


def _cc_sample_kernel(u_ref, prev_ref, w_ref, b_ref, g_ref, beta_ref, o_ref, ext_ref):
    ext_ref[:, 0:CC_HIST, :] = prev_ref[...]
    ext_ref[:, CC_HIST:CC_HIST + 8, :] = u_ref[...]
    first = CC_HIST - (CC_KERNEL - 1)
    acc = jnp.broadcast_to(b_ref[...][None], (SAMPLE_BLOCK, 8, CC_CH))
    for k in range(CC_KERNEL):
        acc = acc + w_ref[k][None] * ext_ref[:, first + k:first + k + 8, :]
    acc = acc.reshape(SAMPLE_ROWS, CC_CH)
    o_ref[...] = _silu(_layer_norm(acc, g_ref[...], beta_ref[...])).astype(o_ref.dtype)


def _cc_sample(u3, prev4, w, b, g, beta, *, layer):
    n_req = u3.shape[0]
    return pl.pallas_call(
        _cc_sample_kernel,
        grid=(n_req // SAMPLE_BLOCK,),
        in_specs=[pl.BlockSpec((SAMPLE_BLOCK, 8, CC_CH), lambda i: (i, 0, 0)),
                  pl.BlockSpec((None, SAMPLE_BLOCK, CC_HIST, CC_CH), lambda i: (layer, i, 0, 0))]
        + [_resident(c.shape) for c in (w, b, g, beta)],
        out_specs=pl.BlockSpec((SAMPLE_ROWS, CC_CH), lambda i: (i, 0)),
        out_shape=jax.ShapeDtypeStruct((n_req * 8, CC_CH), BF16),
        scratch_shapes=[pltpu.VMEM((SAMPLE_BLOCK, CC_HIST + 8, CC_CH), F32)],
        compiler_params=_cparams(1),
    )(u3, prev4, w, b, g, beta)


def _rope_tables(pos):
    half = ROT_DIM // 2
    inv_freq = ROPE_THETA ** (-jnp.arange(half, dtype=F32) * 2.0 / ROT_DIM)
    ang = pos.astype(F32)[:, None] * inv_freq[None, :]
    cos, sin = jnp.cos(ang), jnp.sin(ang)
    n = pos.shape[0]
    pad = jnp.zeros((n, DA_HEAD_DIM - ROT_DIM), F32)
    zeros = jnp.zeros((n, half), F32)
    cos_c = jnp.concatenate([cos, cos, pad + 1.0], axis=1)
    up_c = jnp.concatenate([-sin, zeros, pad], axis=1)
    dn_c = jnp.concatenate([zeros, sin, pad], axis=1)
    return tuple(jnp.concatenate([t, t], axis=1) for t in (cos_c, up_c, dn_c))


def _head_lanes(vec):
    v = vec.astype(F32).reshape(SSD_GROUPS, HEADS_PER_GROUP)
    return jnp.pad(v, ((0, 0), (0, LANES - HEADS_PER_GROUP))).reshape(1, SSD_GROUPS * LANES)


def _sublane_repeat(w):
    w = w.astype(F32)
    return jnp.broadcast_to(w[..., None, :], w.shape[:-1] + (8, w.shape[-1]))


def _prepare_layer(l, p):
    pts = [0]
    for s in IN_SIZES:
        pts.append(pts[-1] + s)
    w_in = p['w_in'][l]
    cols = [w_in[:, pts[i]:pts[i + 1]] for i in range(len(IN_SIZES))]
    w_dt = cols[2].reshape(D_MODEL, SSD_GROUPS, HEADS_PER_GROUP)
    w_dt = jnp.pad(w_dt, ((0, 0), (0, 0), (0, LANES - HEADS_PER_GROUP))).reshape(D_MODEL, SSD_GROUPS * LANES)
    row = lambda v: v.astype(F32).reshape(1, -1)
    lam_init = 0.8 - 0.6 * math.exp(-0.3 * l)
    lam = (jnp.exp(jnp.sum(p['da_lambda_q1'][l].astype(F32) * p['da_lambda_k1'][l].astype(F32)))
           - jnp.exp(jnp.sum(p['da_lambda_q2'][l].astype(F32) * p['da_lambda_k2'][l].astype(F32))) + lam_init)
    return dict(
        lam_init=lam_init, lam=lam.reshape(1).astype(F32),
        ln_g=[row(p['ln_g'][l, i]) for i in range(3)], ln_b=[row(p['ln_b'][l, i]) for i in range(3)],
        ff1_gu=p['w_ff1_gu'][l].astype(BF16), ff1_d=p['w_ff1_down'][l].astype(BF16),
        ff2_gu=p['w_ff2_gu'][l].astype(BF16), ff2_d=p['w_ff2_down'][l].astype(BF16),
        w_z=cols[0].astype(BF16), w_xbc=cols[1].astype(BF16), w_dt=w_dt.astype(BF16),
        w_q=cols[3].astype(BF16), w_k=cols[4].astype(BF16), w_v=cols[5].astype(BF16),
        w_ga=cols[6][:, :CC_CH].astype(BF16), w_gb=cols[6][:, CC_CH:].astype(BF16),
        w_gate=cols[7].astype(BF16), b_gate=row(p['b_gate'][l]),
        dt_bias=_head_lanes(p['ssd_dt_bias'][l]), a4=_head_lanes(-jnp.exp(p['ssd_a_log'][l].astype(F32))),
        conv_w=_sublane_repeat(p['ssd_conv_w'][l]), conv_b=_sublane_repeat(p['ssd_conv_b'][l]),
        d_e=jnp.repeat(p['ssd_d'][l].astype(F32), SSD_HEAD_DIM).reshape(1, SSD_INNER),
        norm_w=row(p['ssd_norm_w'][l]), ssd_out=p['ssd_out'][l].astype(BF16),
        subln_w=row(p['da_subln_w'][l]), da_out=p['da_out'][l].astype(BF16),
        cc_w=_sublane_repeat(p['cc_conv_w'][l]), cc_b=_sublane_repeat(p['cc_conv_b'][l]),
        cc_g=row(p['cc_ln_g'][l]), cc_beta=row(p['cc_ln_b'][l]), cc_out=p['cc_out'][l].astype(BF16),
        w_out=p['w_out'][l].astype(BF16),
    )


def _mixer_tail(h1, ys, yo, yc, lp, alpha, tm_merge, tm_ffn):
    h2 = _merge(h1, ys, yo, yc, lp['w_gate'], lp['b_gate'], lp['ssd_out'], lp['da_out'], lp['cc_out'], lp['w_out'],
                lp['ln_g'][1], lp['ln_b'][1], tm=tm_merge, alpha=alpha)
    return _ffn_ln(h2, lp['ff2_gu'], lp['ff2_d'], lp['ln_g'][2], lp['ln_b'][2], tm=tm_ffn, alpha=alpha)


def kernel(x_prompt, x_sample, cache_k, cache_v, state_ssm, state_ssd_conv, state_conf_conv, page_table,
           meta_tokens, ln_g, ln_b, w_ff1_gu, w_ff1_down, w_ff2_gu, w_ff2_down, w_in, b_gate,
           ssd_conv_w, ssd_conv_b, ssd_dt_bias, ssd_a_log, ssd_d, ssd_norm_w, ssd_out,
           da_lambda_q1, da_lambda_k1, da_lambda_q2, da_lambda_k2, da_subln_w, da_out,
           cc_conv_w, cc_conv_b, cc_ln_g, cc_ln_b, cc_out, w_out):
    params = dict(ln_g=ln_g, ln_b=ln_b, w_ff1_gu=w_ff1_gu, w_ff1_down=w_ff1_down, w_ff2_gu=w_ff2_gu,
                  w_ff2_down=w_ff2_down, w_in=w_in, b_gate=b_gate, ssd_conv_w=ssd_conv_w, ssd_conv_b=ssd_conv_b,
                  ssd_dt_bias=ssd_dt_bias, ssd_a_log=ssd_a_log, ssd_d=ssd_d, ssd_norm_w=ssd_norm_w, ssd_out=ssd_out,
                  da_lambda_q1=da_lambda_q1, da_lambda_k1=da_lambda_k1, da_lambda_q2=da_lambda_q2,
                  da_lambda_k2=da_lambda_k2, da_subln_w=da_subln_w, da_out=da_out, cc_conv_w=cc_conv_w,
                  cc_conv_b=cc_conv_b, cc_ln_g=cc_ln_g, cc_ln_b=cc_ln_b, cc_out=cc_out, w_out=w_out)
    depth = w_in.shape[0]
    alpha = (2.0 * depth) ** 0.25
    bp, seq = x_prompt.shape[0], x_prompt.shape[1]
    lp_len = seq + N_META
    ds, dec_len = x_sample.shape[0], x_sample.shape[1]
    assert dec_len == 8 and lp_len % 48 == 0 and seq % (2 * SSD_CHUNK) == 0 and ds % SAMPLE_BLOCK == 0
    n_pages = page_table.shape[1]
    past = n_pages * cache_k.shape[2]
    n_pool = cache_k.shape[1]

    tm_p = lp_len // 3
    tm_pm = 384
    tm_s = 512
    assert (bp * lp_len) % tm_pm == 0 and (ds * dec_len) % tm_s == 0

    hp = jnp.concatenate([jnp.broadcast_to(meta_tokens.astype(F32)[None], (bp, N_META, D_MODEL)),
                          x_prompt.astype(F32)], axis=1).reshape(bp * lp_len, D_MODEL)
    hs = x_sample.astype(F32).reshape(ds * dec_len, D_MODEL)
    tab_p = _rope_tables(jnp.arange(lp_len))
    tab_s = tuple(jnp.tile(t, (tm_s // dec_len, 1)) for t in _rope_tables(past + jnp.arange(dec_len)))
    ck = cache_k.reshape(depth, n_pool, PAGE_SIZE * DA_HEADS, HEAD_W)
    cv = cache_v.reshape(depth, n_pool, PAGE_SIZE * DA_HEADS, HEAD_W)
    pt_flat = page_table.reshape(-1).astype(jnp.int32)

    h0_all = state_ssm.astype(F32).reshape(depth, ds, SSD_INNER, SSD_STATE)
    conv_prev4 = jnp.pad(state_ssd_conv.astype(F32), ((0, 0), (0, 0), (8 - (SSD_CONV - 1), 0), (0, 0)))
    cc_prev4 = jnp.pad(state_conf_conv.astype(F32), ((0, 0), (0, 0), (CC_HIST - (CC_KERNEL - 1), 0), (0, 0)))
    kv_p, h1_buf = None, None

    outs = {k: [] for k in ('ks', 'vs', 'hp', 'cp', 'cs', 'up', 'us')}
    for l in range(depth):
        lp = _prepare_layer(l, params)
        post = 1.0 - lp['lam_init']

        h1 = _ffn_ln(hp, lp['ff1_gu'], lp['ff1_d'], lp['ln_g'][0], lp['ln_b'][0], tm=tm_p, alpha=alpha)
        zs, xs, bcm, dt, conv_tail = _proj_ssd_conv(h1, lp['w_z'], lp['w_xbc'], lp['w_dt'], lp['dt_bias'],
                                                     lp['conv_w'], lp['conv_b'], tm=tm_p, tiles_per_seq=lp_len // tm_p)
        q, k, v, yc, u_tail = _proj_mix_cc(h1, tab_p, lp['w_q'], lp['w_k'], lp['w_v'], lp['w_ga'], lp['w_gb'],
                                           lp['cc_w'], lp['cc_b'], lp['cc_g'], lp['cc_beta'],
                                           tm=tm_p, tiles_per_seq=lp_len // tm_p)
        ys, hfin = _ssd_prompt(xs, bcm, zs, dt, lp['a4'], lp['d_e'], lp['norm_w'], batch=bp, seq=lp_len)
        att = _attn_prompt(lp['lam'], q, k, v, lp['subln_w'], batch=bp, seq=lp_len, post_scale=post,
                           prev_kv=kv_p)
        yo = att[0]
        kv_p = (k[None], v[None]) if kv_p is None else (att[1], att[2])
        hp = _mixer_tail(h1, ys, yo, yc, lp, alpha, tm_pm, tm_p)
        outs['hp'].append(hfin.reshape(bp, SSD_HEADS, SSD_HEAD_DIM, SSD_STATE))
        outs['cp'].append(conv_tail.reshape(bp, 8, SSD_XBC)[:, 8 - (SSD_CONV - 1):])
        outs['up'].append(u_tail.reshape(bp, CC_HIST, CC_CH)[:, CC_HIST - (CC_KERNEL - 1):])

        h1 = _ffn_ln(hs, lp['ff1_gu'], lp['ff1_d'], lp['ln_g'][0], lp['ln_b'][0], tm=tm_s, alpha=alpha)
        z, xbc, dt = _proj_ssd(h1, lp['w_z'], lp['w_xbc'], lp['w_dt'], lp['dt_bias'], tm=tm_s)
        q, k, v, u = _proj_mix(h1, tab_s, lp['w_q'], lp['w_k'], lp['w_v'], lp['w_ga'], lp['w_gb'],
                               tm=tm_s, q_dtype=F32)
        xbc3 = xbc.reshape(ds, dec_len, SSD_XBC)
        ys, h1_buf = _ssd_sample(xbc3, conv_prev4, z, dt, h0_all, lp['conv_w'], lp['conv_b'], lp['a4'], lp['d_e'],
                                 lp['norm_w'], layer=l, prev_h1=h1_buf)
        yo = _attn_sample(pt_flat, lp['lam'], q, k.reshape(ds, dec_len * DA_HEADS, HEAD_W),
                          v.reshape(ds, dec_len * DA_HEADS, HEAD_W), lp['subln_w'], ck, cv,
                          layer=l, post_scale=post)
        u3 = u.reshape(ds, dec_len, CC_CH)
        yc = _cc_sample(u3, cc_prev4, lp['cc_w'], lp['cc_b'], lp['cc_g'], lp['cc_beta'], layer=l)
        hs = _mixer_tail(h1, ys, yo, yc, lp, alpha, tm_s, tm_s)
        outs['ks'].append(k.reshape(ds, dec_len, DA_HEADS, HEAD_W))
        outs['vs'].append(v.reshape(ds, dec_len, DA_HEADS, HEAD_W))
        outs['cs'].append(xbc3)
        outs['us'].append(u3)

    y_prompt = hp.reshape(bp, lp_len, D_MODEL)[:, N_META:]
    y_sample = hs.reshape(ds, dec_len, D_MODEL)
    st = {k: jnp.stack(v) for k, v in outs.items()}
    k_prompt = kv_p[0].reshape(depth, bp, lp_len, DA_HEADS, HEAD_W)
    v_prompt = kv_p[1].reshape(depth, bp, lp_len, DA_HEADS, HEAD_W)
    ssm_sample = h1_buf.reshape(depth, ds, SSD_HEADS, SSD_HEAD_DIM, SSD_STATE)
    conv_sample = jnp.concatenate([state_ssd_conv.astype(F32), st['cs']], axis=2)[:, :, -(SSD_CONV - 1):]
    cc_sample = jnp.concatenate([state_conf_conv.astype(F32), st['us']], axis=2)[:, :, -(CC_KERNEL - 1):]
    return (y_prompt, y_sample, k_prompt, v_prompt, st['ks'], st['vs'], st['hp'], ssm_sample,
            st['cp'], conv_sample, st['up'], cc_sample)
```

```python
import functools
import math

import jax
import jax.numpy as jnp
from jax import lax
from jax.experimental import pallas as pl
from jax.experimental.pallas import tpu as pltpu

F32 = jnp.float32
BF16 = jnp.bfloat16

D_MODEL = 1024
N_META = 16
SSD_INNER = 2048
SSD_HEAD_DIM = 64
SSD_HEADS = 32
SSD_STATE = 128
SSD_GROUPS = 4
SSD_CONV = 4
SSD_CHUNK = 128
SSD_UNROLL = 8
SSD_XBC = SSD_INNER + 2 * SSD_GROUPS * SSD_STATE
GROUP_CH = SSD_INNER // SSD_GROUPS
HEADS_PER_GROUP = SSD_HEADS // SSD_GROUPS
DA_HEAD_DIM = 64
DA_HEADS = 8
DA_WIDTH = 1024
HEAD_W = 2 * DA_HEAD_DIM
ROT_DIM = DA_HEAD_DIM // 4
ROPE_THETA = 500000.0
Q_BLOCK = 256
PAGE_SIZE = 128
CC_CH = 1024
CC_KERNEL = 31
FF_HIDDEN = 2816
N_BRANCH = 3
IN_SIZES = (SSD_INNER, SSD_XBC, SSD_HEADS, DA_WIDTH, DA_WIDTH, DA_WIDTH, 2 * CC_CH, N_BRANCH * D_MODEL)
LN_EPS = 1e-5

VMEM_LIMIT_V7X = 52 * 1024 * 1024
LANES = 128
SAMPLE_BLOCK = 16
SAMPLE_ROWS = SAMPLE_BLOCK * 8


def _cparams(n_axes):
    return pltpu.CompilerParams(dimension_semantics=("arbitrary",) * n_axes,
                                vmem_limit_bytes=VMEM_LIMIT_V7X)


def _resident(shape):
    nd = len(shape)
    return pl.BlockSpec(shape, lambda *_: (0,) * nd, pipeline_mode=pl.Buffered(1))


def _dot(a, b):
    return jnp.dot(a, b, preferred_element_type=F32)


def _dot_nt(a, b):
    return lax.dot_general(a, b, (((1,), (1,)), ((), ())), preferred_element_type=F32)


def _dot_exact(a, b):
    return jnp.dot(a, b, preferred_element_type=F32, precision=lax.Precision.HIGHEST)


def _sigmoid(x):
    return 0.5 * jnp.tanh(0.5 * x) + 0.5


def _silu(x):
    return x * _sigmoid(x)


def _layer_norm(x, g, b):
    xc = x - jnp.mean(x, axis=-1, keepdims=True)
    var = jnp.mean(xc * xc, axis=-1, keepdims=True)
    return xc * lax.rsqrt(var + LN_EPS) * g + b


FF_CHUNK = 256


def _ffn_rows(x, wgu_ref, wd_ref, g_ref, b_ref, acc_ref, alpha):
    rows = x.shape[0]
    xb = x.astype(BF16)
    for j in range(FF_HIDDEN // FF_CHUNK):
        lo = j * FF_CHUNK
        gate = _dot(xb, wgu_ref[:, lo:lo + FF_CHUNK])
        up = _dot(xb, wgu_ref[:, FF_HIDDEN + lo:FF_HIDDEN + lo + FF_CHUNK])
        act = (_silu(gate) * up).astype(BF16)
        down = _dot(act, wd_ref[lo:lo + FF_CHUNK, :])
        if j == 0:
            acc_ref[0:rows, :] = down
        else:
            acc_ref[0:rows, :] += down
    return _layer_norm(alpha * x + 0.5 * acc_ref[0:rows, :], g_ref[...], b_ref[...])


def _ffn_kernel(x_ref, wgu_ref, wd_ref, g_ref, b_ref, o_ref, acc_ref, *, alpha):
    o_ref[...] = _ffn_rows(x_ref[...], wgu_ref, wd_ref, g_ref, b_ref, acc_ref, alpha)


def _ffn_first_kernel(x_ref, meta_ref, wgu_ref, wd_ref, g_ref, b_ref, o_ref, acc_ref, *, alpha):
    j = pl.program_id(1)
    sub = x_ref.shape[0]

    @pl.when(j == 0)
    def _():
        o_ref[0:N_META, :] = meta_ref[...]

    o_ref[pl.ds(pl.multiple_of(N_META + j * sub, 16), sub), :] = _ffn_rows(
        x_ref[...], wgu_ref, wd_ref, g_ref, b_ref, acc_ref, alpha)


def _ffn_first(x, meta, wgu, wd, g, b, *, batch, seq, sub, alpha):
    meta = _ffn_ln(meta, wgu, wd, g, b, tm=meta.shape[0], alpha=alpha)
    consts = (meta, wgu, wd, g, b)
    return pl.pallas_call(
        functools.partial(_ffn_first_kernel, alpha=alpha),
        grid=(batch, seq // sub),
        in_specs=[pl.BlockSpec((sub, D_MODEL), lambda i, j: (i * (seq // sub) + j, 0))]
        + [_resident(c.shape) for c in consts],
        out_specs=pl.BlockSpec((N_META + seq, D_MODEL), lambda i, j: (i, 0)),
        out_shape=jax.ShapeDtypeStruct((batch * (N_META + seq), D_MODEL), F32),
        scratch_shapes=[pltpu.VMEM((sub, D_MODEL), F32)],
        compiler_params=_cparams(2),
    )(x, *consts)


def _ffn_last_kernel(x_ref, wgu_ref, wd_ref, g_ref, b_ref, o_ref, acc_ref, *, alpha):
    sub = o_ref.shape[0]
    x = x_ref[pl.ds(pl.multiple_of(N_META + pl.program_id(1) * sub, 16), sub), :]
    o_ref[...] = _ffn_rows(x, wgu_ref, wd_ref, g_ref, b_ref, acc_ref, alpha)


def _ffn_last(x, wgu, wd, g, b, *, batch, seq, sub, alpha):
    consts = (wgu, wd, g, b)
    return pl.pallas_call(
        functools.partial(_ffn_last_kernel, alpha=alpha),
        grid=(batch, seq // sub),
        in_specs=[pl.BlockSpec((N_META + seq, D_MODEL), lambda i, j: (i, 0))] + [_resident(c.shape) for c in consts],
        out_specs=pl.BlockSpec((sub, D_MODEL), lambda i, j: (i * (seq // sub) + j, 0)),
        out_shape=jax.ShapeDtypeStruct((batch * seq, D_MODEL), F32),
        scratch_shapes=[pltpu.VMEM((sub, D_MODEL), F32)],
        compiler_params=_cparams(2),
    )(x, *consts)


def _ffn_ln(x, wgu, wd, g, b, *, tm, alpha):
    m = x.shape[0]
    row = pl.BlockSpec((tm, D_MODEL), lambda i: (i, 0))
    return pl.pallas_call(
        functools.partial(_ffn_kernel, alpha=alpha),
        grid=(m // tm,),
        in_specs=[row, _resident(wgu.shape), _resident(wd.shape), _resident(g.shape), _resident(b.shape)],
        out_specs=row,
        out_shape=jax.ShapeDtypeStruct((m, D_MODEL), F32),
        scratch_shapes=[pltpu.VMEM((tm, D_MODEL), F32)],
        compiler_params=_cparams(1),
    )(x, wgu, wd, g, b)


def _softplus(x):
    return jnp.maximum(x, 0.0) + jnp.log1p(jnp.exp(-jnp.abs(x)))


def _rope_slice(x, cos_t, sin_up, sin_dn):
    return (x * cos_t + pltpu.roll(x, LANES - ROT_DIM // 2, axis=1) * sin_up
            + pltpu.roll(x, ROT_DIM // 2, axis=1) * sin_dn)


def _proj_ssd_kernel(x_ref, wz_ref, wxbc_ref, wdt_ref, dtb_ref, z_ref, xbc_ref, dt_ref):
    xb = x_ref[...].astype(BF16)
    cw = 512
    for c in range(SSD_INNER // cw):
        z_ref[:, c * cw:(c + 1) * cw] = _dot(xb, wz_ref[:, c * cw:(c + 1) * cw])
    for c in range(SSD_XBC // cw):
        xbc_ref[:, c * cw:(c + 1) * cw] = _dot(xb, wxbc_ref[:, c * cw:(c + 1) * cw])
    dt_ref[...] = _softplus(_dot(xb, wdt_ref[...]) + dtb_ref[...])


def _proj_ssd(x, wz, wxbc, wdt, dtb, *, tm):
    m = x.shape[0]

    def row(n):
        return pl.BlockSpec((tm, n), lambda i: (i, 0))

    return pl.pallas_call(
        _proj_ssd_kernel,
        grid=(m // tm,),
        in_specs=[row(D_MODEL), _resident(wz.shape), _resident(wxbc.shape), _resident(wdt.shape),
                  _resident(dtb.shape)],
        out_specs=[row(SSD_INNER), row(SSD_XBC), row(SSD_GROUPS * LANES)],
        out_shape=[jax.ShapeDtypeStruct((m, SSD_INNER), F32), jax.ShapeDtypeStruct((m, SSD_XBC), F32),
                   jax.ShapeDtypeStruct((m, SSD_GROUPS * LANES), F32)],
        compiler_params=_cparams(1),
    )(x, wz, wxbc, wdt, dtb)


def _proj_ssd_conv_kernel(x_ref, wz_ref, wxbc_ref, wdt_ref, dtb_ref, cw_ref, cb_ref,
                          zs_ref, xs_ref, bc_ref, dt_ref, tail_ref, win_ref, carry_ref, *, tiles_per_seq):
    tm = x_ref.shape[0]
    split = (tm // 2 + 15) // 16 * 16
    first = 8 - (SSD_CONV - 1)
    xb = x_ref[...].astype(BF16)

    @pl.when(pl.program_id(0) % tiles_per_seq == 0)
    def _():
        carry_ref[...] = jnp.zeros_like(carry_ref)

    cw = 512
    n_xbc, n_z = SSD_XBC // cw, SSD_INNER // cw

    def project(c):
        cols = slice(c * cw, (c + 1) * cw)
        win_ref[c % 2, 0:8, :] = carry_ref[:, cols]
        win_ref[c % 2, 8:8 + tm, :] = _dot(xb, wxbc_ref[:, cols])

    def conv(c):
        win = win_ref.at[c % 2]
        for j in range(cw // LANES):
            lanes = slice(j * LANES, (j + 1) * LANES)
            col = c * cw + j * LANES
            for r0, n in ((0, split), (split, tm - split)):
                taps = [win[first + k + r0:first + k + r0 + n, lanes] for k in range(SSD_CONV)]
                act = _conv_silu(taps, cw_ref[:, :, col:col + LANES], cb_ref[:, col:col + LANES])
                if col < SSD_INNER:
                    xs_ref[r0:r0 + n, col:col + LANES] = act
                else:
                    bc_ref[r0:r0 + n, col - SSD_INNER:col - SSD_INNER + LANES] = act.astype(BF16)
        tail = win[tm:tm + 8, :]
        tail_ref[:, c * cw:(c + 1) * cw] = tail
        carry_ref[:, c * cw:(c + 1) * cw] = tail

    project(0)
    for c in range(n_xbc):
        if c + 1 < n_xbc:
            project(c + 1)
        if c < n_z:
            zs_ref[:, c * cw:(c + 1) * cw] = _silu(_dot(xb, wz_ref[:, c * cw:(c + 1) * cw])).astype(BF16)
        conv(c)
    dt_ref[...] = _softplus(_dot(xb, wdt_ref[...]) + dtb_ref[...])


def _proj_ssd_conv(x, wz, wxbc, wdt, dtb, conv_w, conv_b, *, tm, tiles_per_seq):
    m = x.shape[0]
    assert tm % 16 == 0

    def row(n):
        return pl.BlockSpec((tm, n), lambda i: (i, 0))

    consts = (wz, wxbc, wdt, dtb, conv_w, conv_b)
    return pl.pallas_call(
        functools.partial(_proj_ssd_conv_kernel, tiles_per_seq=tiles_per_seq),
        grid=(m // tm,),
        in_specs=[row(D_MODEL)] + [_resident(c.shape) for c in consts],
        out_specs=[row(SSD_INNER), row(SSD_INNER), row(2 * SSD_GROUPS * SSD_STATE), row(SSD_GROUPS * LANES),
                   pl.BlockSpec((8, SSD_XBC), lambda i: (i // tiles_per_seq, 0))],
        out_shape=[jax.ShapeDtypeStruct((m, SSD_INNER), BF16), jax.ShapeDtypeStruct((m, SSD_INNER), F32),
                   jax.ShapeDtypeStruct((m, 2 * SSD_GROUPS * SSD_STATE), BF16),
                   jax.ShapeDtypeStruct((m, SSD_GROUPS * LANES), F32),
                   jax.ShapeDtypeStruct((m // (tm * tiles_per_seq) * 8, SSD_XBC), F32)],
        scratch_shapes=[pltpu.VMEM((2, 8 + tm, 512), F32), pltpu.VMEM((8, SSD_XBC), F32)],
        compiler_params=_cparams(1),
    )(x, *consts)


def _proj_mix_kernel(x_ref, cos_ref, sup_ref, sdn_ref, wq_ref, wk_ref, wv_ref, wga_ref, wgb_ref,
                     q_ref, k_ref, v_ref, u_ref):
    xb = x_ref[...].astype(BF16)
    cos_t, sin_up, sin_dn = cos_ref[...], sup_ref[...], sdn_ref[...]
    cw = 256
    for c in range(DA_WIDTH // cw):
        lo = c * cw
        qc = _dot(xb, wq_ref[:, lo:lo + cw])
        kc = _dot(xb, wk_ref[:, lo:lo + cw])
        for s in range(cw // HEAD_W):
            sl = slice(s * HEAD_W, (s + 1) * HEAD_W)
            dst = slice(lo + s * HEAD_W, lo + (s + 1) * HEAD_W)
            q_ref[:, dst] = (_rope_slice(qc[:, sl], cos_t, sin_up, sin_dn)
                             * (DA_HEAD_DIM ** -0.5)).astype(q_ref.dtype)
            k_ref[:, dst] = _rope_slice(kc[:, sl], cos_t, sin_up, sin_dn)
        v_ref[:, lo:lo + cw] = _dot(xb, wv_ref[:, lo:lo + cw])
        ga = _dot(xb, wga_ref[:, lo:lo + cw])
        gb = _dot(xb, wgb_ref[:, lo:lo + cw])
        u_ref[:, lo:lo + cw] = ga * _sigmoid(gb)


def _proj_mix_cc_kernel(x_ref, cos_ref, sup_ref, sdn_ref, wq_ref, wk_ref, wv_ref, wga_ref, wgb_ref,
                        ccw_ref, ccb_ref, g_ref, beta_ref, q_ref, k_ref, v_ref, c_ref, utail_ref,
                        win_ref, conv_ref, *, tiles_per_seq):
    tm = x_ref.shape[0]
    xb = x_ref[...].astype(BF16)
    cos_t, sin_up, sin_dn = cos_ref[...], sup_ref[...], sdn_ref[...]

    @pl.when(pl.program_id(0) % tiles_per_seq == 0)
    def _():
        win_ref[0:CC_HIST, :] = jnp.zeros((CC_HIST, CC_CH), F32)

    pieces = [(r, min(CC_PIECE, tm - r)) for r in range(0, tm, CC_PIECE)]

    def conv(lo, hi):
        for lb in range(lo // LANES, hi // LANES):
            lanes = slice(lb * LANES, (lb + 1) * LANES)
            for r0, n in pieces:
                conv_ref[r0:r0 + n, lanes] = _cc_taps(
                    lambda start, span: win_ref[r0 + start:r0 + start + span, lanes], ccw_ref, ccb_ref, lanes, n)

    cw = 256
    for c in range(DA_WIDTH // cw):
        lo = c * cw
        qc = _dot(xb, wq_ref[:, lo:lo + cw])
        kc = _dot(xb, wk_ref[:, lo:lo + cw])
        for s in range(cw // HEAD_W):
            sl = slice(s * HEAD_W, (s + 1) * HEAD_W)
            dst = slice(lo + s * HEAD_W, lo + (s + 1) * HEAD_W)
            q_ref[:, dst] = (_rope_slice(qc[:, sl], cos_t, sin_up, sin_dn)
                             * (DA_HEAD_DIM ** -0.5)).astype(q_ref.dtype)
            k_ref[:, dst] = _rope_slice(kc[:, sl], cos_t, sin_up, sin_dn)
        v_ref[:, lo:lo + cw] = _dot(xb, wv_ref[:, lo:lo + cw])
        ga = _dot(xb, wga_ref[:, lo:lo + cw])
        gb = _dot(xb, wgb_ref[:, lo:lo + cw])
        win_ref[CC_HIST:CC_HIST + tm, lo:lo + cw] = ga * _sigmoid(gb)
        if c > 0:
            conv(lo - cw, lo)
    conv(DA_WIDTH - cw, DA_WIDTH)
    c_ref[...] = _silu(_layer_norm(conv_ref[...], g_ref[...], beta_ref[...])).astype(c_ref.dtype)
    tail = win_ref[tm:tm + CC_HIST, :]
    utail_ref[...] = tail
    win_ref[0:CC_HIST, :] = tail


def _proj_mix_cc(x, tables, wq, wk, wv, wga, wgb, ccw, ccb, g, beta, *, tm, tiles_per_seq):
    m = x.shape[0]
    n_tab = tables[0].shape[0] // tm
    row = pl.BlockSpec((tm, D_MODEL), lambda i: (i, 0))
    tab = pl.BlockSpec((tm, LANES), lambda i: (i % n_tab, 0))
    w = _resident(wq.shape)
    consts = (ccw, ccb, g, beta)
    return pl.pallas_call(
        functools.partial(_proj_mix_cc_kernel, tiles_per_seq=tiles_per_seq),
        grid=(m // tm,),
        in_specs=[row, tab, tab, tab, w, w, w, w, w] + [_resident(c.shape) for c in consts],
        out_specs=[row, row, row, row, pl.BlockSpec((CC_HIST, CC_CH), lambda i: (i // tiles_per_seq, 0))],
        out_shape=[jax.ShapeDtypeStruct((m, D_MODEL), BF16), jax.ShapeDtypeStruct((m, D_MODEL), F32),
                   jax.ShapeDtypeStruct((m, D_MODEL), F32), jax.ShapeDtypeStruct((m, CC_CH), BF16),
                   jax.ShapeDtypeStruct((m // (tm * tiles_per_seq) * CC_HIST, CC_CH), F32)],
        scratch_shapes=[pltpu.VMEM((CC_HIST + tm, CC_CH), F32), pltpu.VMEM((tm, CC_CH), F32)],
        compiler_params=_cparams(1),
    )(x, *tables, wq, wk, wv, wga, wgb, *consts)


def _proj_mix(x, tables, wq, wk, wv, wga, wgb, *, tm, q_dtype):
    m = x.shape[0]
    n_tab = tables[0].shape[0] // tm
    row = pl.BlockSpec((tm, D_MODEL), lambda i: (i, 0))
    tab = pl.BlockSpec((tm, LANES), lambda i: (i % n_tab, 0))
    w = _resident(wq.shape)
    return pl.pallas_call(
        _proj_mix_kernel,
        grid=(m // tm,),
        in_specs=[row, tab, tab, tab, w, w, w, w, w],
        out_specs=[row, row, row, row],
        out_shape=[jax.ShapeDtypeStruct((m, D_MODEL), q_dtype)] + [jax.ShapeDtypeStruct((m, D_MODEL), F32)] * 3,
        compiler_params=_cparams(1),
    )(x, *tables, wq, wk, wv, wga, wgb)


def _merge_kernel(h_ref, ys_ref, yo_ref, yc_ref, wgate_ref, bgate_ref, wssd_ref, wda_ref, wcc_ref, wout_ref,
                  g_ref, b_ref, o_ref, *, alpha):
    h = h_ref[...]
    hb = h.astype(BF16)
    merged = None
    for i, (y_ref, w_ref) in enumerate(((ys_ref, wssd_ref), (yo_ref, wda_ref), (yc_ref, wcc_ref))):
        sl = slice(i * D_MODEL, (i + 1) * D_MODEL)
        gate = _sigmoid(_dot(hb, wgate_ref[:, sl]) + bgate_ref[:, sl])
        term = gate * _dot(y_ref[...].astype(BF16), w_ref[...])
        merged = term if merged is None else merged + term
    mix = _dot(merged.astype(BF16), wout_ref[...])
    o_ref[...] = _layer_norm(alpha * h + mix, g_ref[...], b_ref[...])


def _merge(h, ys, yo, yc, wgate, bgate, wssd, wda, wcc, wout, g, b, *, tm, alpha):
    m = h.shape[0]

    def row(n):
        return pl.BlockSpec((tm, n), lambda i: (i, 0))

    consts = (wgate, bgate, wssd, wda, wcc, wout, g, b)
    return pl.pallas_call(
        functools.partial(_merge_kernel, alpha=alpha),
        grid=(m // tm,),
        in_specs=[row(D_MODEL), row(SSD_INNER), row(DA_WIDTH), row(CC_CH)] + [_resident(c.shape) for c in consts],
        out_specs=row(D_MODEL),
        out_shape=jax.ShapeDtypeStruct((m, D_MODEL), F32),
        compiler_params=_cparams(1),
    )(h, ys, yo, yc, *consts)


def _pair_cols(x, h0, h1, lane_lo):
    shape = (x.shape[0], LANES)
    return jnp.where(lane_lo[:x.shape[0]], jnp.broadcast_to(x[:, h0:h0 + 1], shape),
                     jnp.broadcast_to(x[:, h1:h1 + 1], shape))


def _ssd_intra(xs, bc, cc, dt, acol, mask, lane_lo):
    arow = acol.T
    bb = bc.astype(BF16)
    cb = _dot_nt(cc.astype(BF16), bb)
    y_in, xdts, aces = [], [], []
    for j in range(HEADS_PER_GROUP // 2):
        h0, h1 = 2 * j, 2 * j + 1
        gs, wide = [], []
        for h in (h0, h1):
            a_h = jnp.broadcast_to(acol[:, h:h + 1], acol.shape)
            wide.append(a_h)
            gs.append((cb * jnp.exp(jnp.where(mask, a_h - arow[h:h + 1, :], -jnp.inf))).astype(BF16))
        gp = jnp.concatenate(gs, axis=1)
        xdt = xs[:, j * LANES:(j + 1) * LANES] * _pair_cols(dt, h0, h1, lane_lo)
        zero = jnp.zeros_like(xdt)
        xbd = jnp.concatenate([jnp.where(lane_lo, xdt, zero), jnp.where(lane_lo, zero, xdt)], axis=0).astype(BF16)
        y_in.append(_dot(gp, xbd))
        xdts.append(xdt)
        aces.append(jnp.where(lane_lo, wide[0], wide[1]))
    return y_in, xdts, aces, bb


def _ssd_finish(ys, xs, gate, d_row, nw_row):
    gated, ss = [], None
    for j, y in enumerate(ys):
        sl = slice(j * LANES, (j + 1) * LANES)
        yz = (y + d_row[:, sl] * xs[:, sl]) * gate[:, sl]
        gated.append(yz)
        ss = yz * yz if ss is None else ss + yz * yz
    rs = lax.rsqrt(jnp.sum(ss, axis=1, keepdims=True) * (1.0 / GROUP_CH) + LN_EPS)
    return [(yz * rs * nw_row[:, j * LANES:(j + 1) * LANES]).astype(BF16) for j, yz in enumerate(gated)]


def _conv_silu(taps, w_ref, b_ref):
    rows, ch = taps[0].shape
    acc = jnp.broadcast_to(b_ref[...][None], (rows // 8, 8, ch))
    for k in range(SSD_CONV):
        acc = acc + w_ref[k][None] * taps[k].reshape(rows // 8, 8, ch)
    return _silu(acc).reshape(rows, ch)


def _ssd_prompt_kernel(xs_ref, bm_ref, cm_ref, zs_ref, dt_ref, a_ref, d_ref, nw_ref, y_ref, hfin_ref, state_ref):
    L = SSD_CHUNK
    rows = lax.broadcasted_iota(jnp.int32, (L, LANES), 0)
    cols = lax.broadcasted_iota(jnp.int32, (L, LANES), 1)
    causal = rows >= cols
    tri = causal.astype(F32)
    lane_lo = cols < SSD_HEAD_DIM
    row_lo = rows < SSD_HEAD_DIM
    state_ref[...] = jnp.zeros_like(state_ref)

    def local_part(r0, meta):
        xs = xs_ref[pl.ds(r0, L), :]
        bc = bm_ref[pl.ds(r0, L), :]
        cc = cm_ref[pl.ds(r0, L), :]
        dt = dt_ref[pl.ds(r0, L), :]
        if meta:
            dt = jnp.where(rows < N_META, dt, 0.0)
        acol = _dot_exact(tri, dt * a_ref[...])
        alast = acol[L - 1:L, :]
        y_in, xdts, aces, bb = _ssd_intra(xs, bc, cc, dt, acol, causal, lane_lo)
        pairs = []
        for j in range(HEADS_PER_GROUP // 2):
            h0, h1 = 2 * j, 2 * j + 1
            ace = aces[j]
            ale = jnp.where(lane_lo[0:1], jnp.broadcast_to(alast[:, h0:h0 + 1], (1, LANES)),
                            jnp.broadcast_to(alast[:, h1:h1 + 1], (1, LANES)))
            xw = xdts[j] * jnp.exp(ale - ace)
            upd = _dot(xw.T.astype(BF16), bb)
            dec = jnp.where(row_lo, jnp.broadcast_to(jnp.exp(alast[:, h0:h0 + 1]), (L, LANES)),
                            jnp.broadcast_to(jnp.exp(alast[:, h1:h1 + 1]), (L, LANES)))
            pairs.append((y_in[j], jnp.exp(ace), upd, dec))
        return xs, cc.astype(BF16), pairs

    def state_part(r0, xs, cb16, pairs):
        ys = []
        for j, (y_loc, eace, upd, dec) in enumerate(pairs):
            st = state_ref[j * LANES:(j + 1) * LANES, :]
            ys.append(y_loc + _dot_nt(cb16, st.astype(BF16)) * eace)
            state_ref[j * LANES:(j + 1) * LANES, :] = st * dec + upd
        outs = _ssd_finish(ys, xs, zs_ref[pl.ds(r0, L), :].astype(F32), d_ref[...], nw_ref[...])
        for j, o in enumerate(outs):
            y_ref[pl.ds(r0, L), j * LANES:(j + 1) * LANES] = o

    state_part(0, *local_part(0, True))

    def body(c, carry):
        starts = [pl.multiple_of(N_META + (SSD_UNROLL * c + i) * L, 16) for i in range(SSD_UNROLL)]
        local = [local_part(r, False) for r in starts]
        for r, loc in zip(starts, local):
            state_part(r, *loc)
        return carry

    n_chunks = (y_ref.shape[0] - N_META) // L
    assert n_chunks % SSD_UNROLL == 0
    lax.fori_loop(0, n_chunks // SSD_UNROLL, body, 0)
    hfin_ref[...] = state_ref[...]


def _ssd_prompt(xs, bcm, zs, dt, a4, d_e, nw, *, batch, seq):
    def seq_block(width, col0=0):
        return pl.BlockSpec((seq, width), lambda b, g: (b, g + col0))

    def par_block(rows, width):
        return pl.BlockSpec((rows, width), lambda b, g: (0, g))

    return pl.pallas_call(
        _ssd_prompt_kernel,
        grid=(batch, SSD_GROUPS),
        in_specs=[seq_block(GROUP_CH), seq_block(SSD_STATE), seq_block(SSD_STATE, SSD_GROUPS),
                  seq_block(GROUP_CH), seq_block(LANES),
                  par_block(1, LANES), par_block(1, GROUP_CH), par_block(1, GROUP_CH)],
        out_specs=[seq_block(GROUP_CH), pl.BlockSpec((None, GROUP_CH, SSD_STATE), lambda b, g: (b, g, 0))],
        out_shape=[jax.ShapeDtypeStruct((batch * seq, SSD_INNER), BF16),
                   jax.ShapeDtypeStruct((batch, SSD_INNER, SSD_STATE), F32)],
        scratch_shapes=[pltpu.VMEM((GROUP_CH, SSD_STATE), F32)],
        compiler_params=_cparams(2),
    )(xs, bcm, bcm, zs, dt, a4, d_e, nw)


def _ssd_sample_kernel(xs_ref, bm_ref, cm_ref, px_ref, pb_ref, pc_ref, z_ref, dt_ref, h0_ref,
                       wx_ref, wb_ref, wc_ref, bx_ref, bb_ref, bc_ref, a_ref, d_ref, nw_ref, *rest, n_prev):
    y_ref, h1_ref, ext_ref, flat_ref, xwt_ref, yst_ref = rest[-6:]
    if n_prev:
        h1_ref[0:n_prev] = rest[0][...]
    R = SAMPLE_ROWS
    rows = lax.broadcasted_iota(jnp.int32, (R, LANES), 0)
    cols = lax.broadcasted_iota(jnp.int32, (R, LANES), 1)
    same = (rows >> 3) == (cols >> 3)
    causal = same & (rows >= cols)
    lane_lo = cols < SSD_HEAD_DIM
    srcs = ((xs_ref, px_ref, 0, GROUP_CH), (bm_ref, pb_ref, GROUP_CH, GROUP_CH + SSD_STATE),
            (cm_ref, pc_ref, GROUP_CH + SSD_STATE, GROUP_CH + 2 * SSD_STATE))
    for cur, prev, lo, hi in srcs:
        ext_ref[:, 0:8, lo:hi] = prev[...]
        ext_ref[:, 8:16, lo:hi] = cur[...]
    first = 8 - (SSD_CONV - 1)
    for k in range(SSD_CONV):
        flat_ref[k] = ext_ref[:, first + k:first + k + 8, :].reshape(R, GROUP_CH + 2 * SSD_STATE)

    def conv(lo, hi, w_ref, b_ref):
        return _conv_silu([flat_ref[k, :, lo:hi] for k in range(SSD_CONV)], w_ref, b_ref)

    xs = conv(0, GROUP_CH, wx_ref, bx_ref)
    bc = conv(GROUP_CH, GROUP_CH + SSD_STATE, wb_ref, bb_ref)
    cc = conv(GROUP_CH + SSD_STATE, GROUP_CH + 2 * SSD_STATE, wc_ref, bc_ref)
    dt = dt_ref[...]
    dta = dt * a_ref[...]
    acol = _dot_exact(causal.astype(F32), dta)
    total = _dot_exact(same.astype(F32), dta)
    y_in, xdts, aces, _ = _ssd_intra(xs, bc, cc, dt, acol, causal, lane_lo)
    for j in range(HEADS_PER_GROUP // 2):
        xw = xdts[j] * jnp.exp(_pair_cols(total, 2 * j, 2 * j + 1, lane_lo) - aces[j])
        xwt_ref[j * LANES:(j + 1) * LANES, :] = xw.T.astype(BF16)
    xwt = xwt_ref[...]
    for i in range(SAMPLE_BLOCK):
        r = slice(8 * i, 8 * i + 8)
        st = h0_ref[i]
        c16 = jnp.concatenate([cc[r], jnp.zeros((8, SSD_STATE), F32)], axis=0).astype(BF16)
        yst_ref[r, :] = _dot_nt(c16, st.astype(BF16))[0:8]
        b_i = jnp.where((rows >> 3) == i, bc, 0.0).astype(BF16)
        upd = _dot(xwt, b_i)
        tot_i = total[8 * i:8 * i + 1, :]
        for h in range(HEADS_PER_GROUP):
            hs = slice(h * SSD_HEAD_DIM, (h + 1) * SSD_HEAD_DIM)
            h1_ref[n_prev, i, hs, :] = st[hs] * jnp.exp(tot_i[:, h:h + 1]) + upd[hs]
    ys = [y_in[j] + yst_ref[:, j * LANES:(j + 1) * LANES] * jnp.exp(aces[j])
          for j in range(HEADS_PER_GROUP // 2)]
    outs = _ssd_finish(ys, xs, _silu(z_ref[...]), d_ref[...], nw_ref[...])
    for j, o in enumerate(outs):
        y_ref[:, j * LANES:(j + 1) * LANES] = o


def _ssd_sample(xbc3, prev4, z, dt, h0_all, conv_w, conv_b, a4, d_e, nw, *, layer, prev_h1=None):
    n_req = xbc3.shape[0]
    n_prev = 0 if prev_h1 is None else prev_h1.shape[0]
    g_b = SSD_INNER // SSD_STATE
    R = SAMPLE_ROWS
    width = GROUP_CH + 2 * SSD_STATE

    def tok_block(w, col0=0):
        return pl.BlockSpec((SAMPLE_BLOCK, 8, w), lambda i, g: (i, 0, g + col0))

    def prev_block(w, col0=0):
        return pl.BlockSpec((None, SAMPLE_BLOCK, 8, w), lambda i, g: (layer, i, 0, g + col0))

    def row_block(w):
        return pl.BlockSpec((R, w), lambda i, g: (i, g))

    def par_block(rows, w, col0=0):
        return pl.BlockSpec((rows, w), lambda i, g: (0, g + col0))

    def tap_block(w, col0=0):
        return pl.BlockSpec((SSD_CONV, 8, w), lambda i, g: (0, 0, g + col0))

    state = pl.BlockSpec((None, SAMPLE_BLOCK, GROUP_CH, SSD_STATE), lambda i, g: (layer, i, g, 0))
    toks = [tok_block(GROUP_CH), tok_block(SSD_STATE, g_b), tok_block(SSD_STATE, g_b + SSD_GROUPS)]
    prevs = [prev_block(GROUP_CH), prev_block(SSD_STATE, g_b), prev_block(SSD_STATE, g_b + SSD_GROUPS)]
    in_specs = toks + prevs + [row_block(GROUP_CH), row_block(LANES), state,
                               tap_block(GROUP_CH), tap_block(SSD_STATE, g_b),
                               tap_block(SSD_STATE, g_b + SSD_GROUPS),
                               par_block(8, GROUP_CH), par_block(8, SSD_STATE, g_b),
                               par_block(8, SSD_STATE, g_b + SSD_GROUPS),
                               par_block(1, LANES), par_block(1, GROUP_CH), par_block(1, GROUP_CH)]
    args = [xbc3, xbc3, xbc3, prev4, prev4, prev4, z, dt, h0_all, conv_w, conv_w, conv_w, conv_b, conv_b, conv_b,
            a4, d_e, nw]
    def stack_block(n):
        return pl.BlockSpec((n, SAMPLE_BLOCK, GROUP_CH, SSD_STATE), lambda i, g: (0, i, g, 0))

    if n_prev:
        in_specs = in_specs + [stack_block(n_prev)]
        args = args + [prev_h1]
    return pl.pallas_call(
        functools.partial(_ssd_sample_kernel, n_prev=n_prev),
        grid=(n_req // SAMPLE_BLOCK, SSD_GROUPS),
        in_specs=in_specs,
        out_specs=[row_block(GROUP_CH), stack_block(n_prev + 1)],
        out_shape=[jax.ShapeDtypeStruct((n_req * 8, SSD_INNER), BF16),
                   jax.ShapeDtypeStruct((n_prev + 1, n_req, SSD_INNER, SSD_STATE), F32)],
        scratch_shapes=[pltpu.VMEM((SAMPLE_BLOCK, 16, width), F32), pltpu.VMEM((SSD_CONV, R, width), F32),
                        pltpu.VMEM((GROUP_CH, R), BF16), pltpu.VMEM((R, GROUP_CH), F32)],
        compiler_params=_cparams(2),
    )(*args)


def _sub_ln(o, w_row, post_scale):
    return o * lax.rsqrt(jnp.mean(o * o, axis=-1, keepdims=True) + LN_EPS) * w_row * post_scale


def _split_components(q):
    lane_lo = lax.broadcasted_iota(jnp.int32, q.shape, 1) < DA_HEAD_DIM
    zero = jnp.zeros_like(q)
    return jnp.concatenate([jnp.where(lane_lo, q, zero), jnp.where(lane_lo, zero, q)], axis=0)


def _attn_prompt_kernel(lam_ref, q_ref, k_ref, v_ref, w_ref, *rest, n_prev, post_scale):
    kb_ref, vb_ref = rest[-2:]
    if n_prev:
        kp_ref, vp_ref, o_ref, kall_ref, vall_ref = rest[:5]
        kall_ref[0:n_prev] = kp_ref[...]
        vall_ref[0:n_prev] = vp_ref[...]
        kall_ref[n_prev] = k_ref[...]
        vall_ref[n_prev] = v_ref[...]
    else:
        o_ref = rest[0]
    lam = lam_ref[0]
    kb_ref[...] = k_ref[...].astype(BF16)
    vb_ref[:, 0:HEAD_W] = v_ref[...].astype(BF16)
    vb_ref[:, HEAD_W:2 * HEAD_W] = jnp.ones((vb_ref.shape[0], HEAD_W), BF16)
    k_meta = kb_ref[0:LANES]
    v_meta = vb_ref[0:LANES]
    w_row = w_ref[...]

    def attend(r0, n_q, parts):
        finish(r0, n_q, soft(parts))

    def soft(parts):
        m = None
        for s, _ in parts:
            pm = jnp.max(s, axis=1, keepdims=True)
            m = pm if m is None else jnp.maximum(m, pm)
        return [(jnp.exp((s - m).astype(BF16)), vals) for s, vals in parts]

    def finish(r0, n_q, probs):
        acc = None
        for p, vals in probs:
            term = _dot(p, vals)
            acc = term if acc is None else acc + term
        on = acc[:, 0:HEAD_W] * (1.0 / acc[:, HEAD_W:HEAD_W + 1])
        o = on[0:n_q] - lam * on[n_q:2 * n_q]
        o_ref[r0:r0 + n_q, :] = _sub_ln(o, w_row, post_scale).astype(o_ref.dtype)

    q2 = _split_components(q_ref[0:N_META])
    s = _dot_nt(q2, k_meta)
    qpos = lax.broadcasted_iota(jnp.int32, s.shape, 0) & (N_META - 1)
    kpos = lax.broadcasted_iota(jnp.int32, s.shape, 1)
    attend(0, N_META, [(jnp.where(kpos <= qpos, s, -jnp.inf), v_meta)])

    n_blk = (q_ref.shape[0] - N_META) // Q_BLOCK
    blk_shape = (2 * Q_BLOCK, Q_BLOCK)
    kcol = lax.broadcasted_iota(jnp.int32, blk_shape, 1)
    meta_ok = lax.broadcasted_iota(jnp.int32, (2 * Q_BLOCK, LANES), 1) < N_META
    diag_ok = kcol <= (lax.broadcasted_iota(jnp.int32, blk_shape, 0) & (Q_BLOCK - 1))

    def scores(i):
        r0 = N_META + i * Q_BLOCK
        q2 = _split_components(q_ref[r0:r0 + Q_BLOCK])
        parts = [(jnp.where(meta_ok, _dot_nt(q2, k_meta), -jnp.inf), v_meta)]
        if i > 0:
            parts.append((_dot_nt(q2, kb_ref[N_META:r0]), vb_ref[N_META:r0]))
        parts.append((jnp.where(diag_ok, _dot_nt(q2, kb_ref[r0:r0 + Q_BLOCK]), -jnp.inf),
                      vb_ref[r0:r0 + Q_BLOCK]))
        return parts

    nxt = scores(0)
    for i in range(n_blk):
        cur, nxt = nxt, (scores(i + 1) if i + 1 < n_blk else None)
        attend(N_META + i * Q_BLOCK, Q_BLOCK, cur)


def _attn_prompt(lam, q, k, v, w_row, *, batch, seq, post_scale, prev_kv=None):
    blk = pl.BlockSpec((seq, HEAD_W), lambda b, h: (b, h))
    in_specs = [pl.BlockSpec(memory_space=pltpu.SMEM), blk, blk, blk, _resident(w_row.shape)]
    args = [lam, q, k, v, w_row]
    out_specs = [blk]
    out_shape = [jax.ShapeDtypeStruct((batch * seq, DA_WIDTH), BF16)]
    n_prev = 0
    if prev_kv is not None:
        n_prev = prev_kv[0].shape[0]
        in_specs += [pl.BlockSpec((n_prev, seq, HEAD_W), lambda b, h: (0, b, h))] * 2
        args += list(prev_kv)
        out_specs += [pl.BlockSpec((n_prev + 1, seq, HEAD_W), lambda b, h: (0, b, h))] * 2
        out_shape += [jax.ShapeDtypeStruct((n_prev + 1, batch * seq, DA_WIDTH), F32)] * 2
    return pl.pallas_call(
        functools.partial(_attn_prompt_kernel, n_prev=n_prev, post_scale=post_scale),
        grid=(batch, DA_HEADS),
        in_specs=in_specs,
        out_specs=out_specs,
        out_shape=out_shape,
        scratch_shapes=[pltpu.VMEM((seq, HEAD_W), BF16), pltpu.VMEM((seq, 2 * HEAD_W), BF16)],
        compiler_params=_cparams(2),
    )(*args)


def _attn_sample_kernel(pt_ref, lam_ref, q_ref, kn_ref, vn_ref, w_ref, *refs, n_pages, post_scale):
    del pt_ref
    k_pages = refs[:n_pages]
    v_pages = refs[n_pages:2 * n_pages]
    o_ref, s_ref = refs[2 * n_pages], refs[2 * n_pages + 1]
    lam = lam_ref[0]
    q = q_ref[...]
    lane = lax.broadcasted_iota(jnp.int32, (8, HEAD_W), 1)
    kpos = lax.broadcasted_iota(jnp.int32, (16, PAGE_SIZE), 1)
    new_ok = kpos <= (lax.broadcasted_iota(jnp.int32, (16, PAGE_SIZE), 0) & 7)
    pad = jnp.zeros((PAGE_SIZE - 8, HEAD_W), F32)
    w_row = w_ref[...]

    def head_rows(ref, h, n):
        return ref[pl.ds(h, n, stride=DA_HEADS), :]

    def keys(h, p):
        if p < n_pages:
            return head_rows(k_pages[p], h, PAGE_SIZE).astype(BF16)
        return jnp.concatenate([head_rows(kn_ref, h, 8), pad], axis=0).astype(BF16)

    def values(h, p):
        if p < n_pages:
            return head_rows(v_pages[p], h, PAGE_SIZE).astype(BF16)
        return jnp.concatenate([head_rows(vn_ref, h, 8), pad], axis=0).astype(BF16)

    def first_pass(h):
        qh = q[:, h * HEAD_W:(h + 1) * HEAD_W]
        q2 = jnp.concatenate([jnp.where(lane < DA_HEAD_DIM, qh, 0.0), jnp.where(lane >= DA_HEAD_DIM, qh, 0.0)],
                             axis=0).astype(BF16)
        m = None
        for p in range(n_pages + 1):
            s = _dot_nt(q2, keys(h, p))
            if p == n_pages:
                s = jnp.where(new_ok, s, -jnp.inf)
            s_ref[h, :, p * PAGE_SIZE:(p + 1) * PAGE_SIZE] = s
            pm = jnp.max(s, axis=1, keepdims=True)
            m = pm if m is None else jnp.maximum(m, pm)
        return m

    def second_pass(h, m):
        den, acc = None, None
        for p in range(n_pages + 1):
            e = jnp.exp(s_ref[h, :, p * PAGE_SIZE:(p + 1) * PAGE_SIZE] - m)
            pd = jnp.sum(e, axis=1, keepdims=True)
            term = _dot(e.astype(BF16), values(h, p))
            den = pd if den is None else den + pd
            acc = term if acc is None else acc + term
        on = acc * (1.0 / den)
        o_ref[:, h * HEAD_W:(h + 1) * HEAD_W] = _sub_ln(on[0:8] - lam * on[8:16], w_row, post_scale)

    m_next = first_pass(0)
    for h in range(DA_HEADS):
        m_cur, m_next = m_next, (first_pass(h + 1) if h + 1 < DA_HEADS else None)
        second_pass(h, m_cur)


def _attn_sample(page_table_flat, lam, q, k_new, v_new, w_row, cache_k, cache_v, *, layer, post_scale):
    n_req = q.shape[0] // 8
    n_pages = page_table_flat.shape[0] // n_req
    page_rows = PAGE_SIZE * DA_HEADS

    def page_spec(j):
        return pl.BlockSpec((None, None, page_rows, HEAD_W), lambda b, pt: (layer, pt[b * n_pages + j], 0, 0))

    tok = pl.BlockSpec((8, DA_WIDTH), lambda b, pt: (b, 0))
    new = pl.BlockSpec((None, 8 * DA_HEADS, HEAD_W), lambda b, pt: (b, 0, 0))
    pages = [page_spec(j) for j in range(n_pages)]
    grid_spec = pltpu.PrefetchScalarGridSpec(
        num_scalar_prefetch=1,
        grid=(n_req,),
        in_specs=[pl.BlockSpec(memory_space=pltpu.SMEM), tok, new, new,
                  pl.BlockSpec(w_row.shape, lambda b, pt: (0, 0))] + pages + pages,
        out_specs=tok,
        scratch_shapes=[pltpu.VMEM((DA_HEADS, 16, (n_pages + 1) * PAGE_SIZE), F32)],
    )
    return pl.pallas_call(
        functools.partial(_attn_sample_kernel, n_pages=n_pages, post_scale=post_scale),
        grid_spec=grid_spec,
        out_shape=jax.ShapeDtypeStruct((n_req * 8, DA_WIDTH), F32),
        compiler_params=_cparams(1),
    )(page_table_flat, lam, q, k_new, v_new, w_row, *([cache_k] * n_pages), *([cache_v] * n_pages))


CC_HIST = 32
CC_PIECE = 96


def _cc_taps(load_window, w_ref, b_ref, lanes, rows):
    first = CC_HIST - (CC_KERNEL - 1)
    nt = rows // 8
    acc = jnp.broadcast_to(b_ref[:, lanes][None], (nt, 8, LANES))
    for phase in range(8):
        n_tap = (CC_KERNEL - 1 - phase) // 8 + 1
        span = rows + 8 * (n_tap - 1)
        win = load_window(first + phase, span).reshape(span // 8, 8, LANES)
        for a in range(n_tap):
            acc = acc + w_ref[8 * a + phase, :, lanes][None] * win[a:a + nt]
    return acc.reshape(rows, LANES)


def _cc_sample_kernel(u_ref, prev_ref, w_ref, b_ref, g_ref, beta_ref, o_ref, ext_ref):
    ext_ref[:, 0:CC_HIST, :] = prev_ref[...]
    ext_ref[:, CC_HIST:CC_HIST + 8, :] = u_ref[...]
    first = CC_HIST - (CC_KERNEL - 1)
    acc = jnp.broadcast_to(b_ref[...][None], (SAMPLE_BLOCK, 8, CC_CH))
    for k in range(CC_KERNEL):
        acc = acc + w_ref[k][None] * ext_ref[:, first + k:first + k + 8, :]
    acc = acc.reshape(SAMPLE_ROWS, CC_CH)
    o_ref[...] = _silu(_layer_norm(acc, g_ref[...], beta_ref[...])).astype(o_ref.dtype)


def _cc_sample(u3, prev4, w, b, g, beta, *, layer):
    n_req = u3.shape[0]
    return pl.pallas_call(
        _cc_sample_kernel,
        grid=(n_req // SAMPLE_BLOCK,),
        in_specs=[pl.BlockSpec((SAMPLE_BLOCK, 8, CC_CH), lambda i: (i, 0, 0)),
                  pl.BlockSpec((None, SAMPLE_BLOCK, CC_HIST, CC_CH), lambda i: (layer, i, 0, 0))]
        + [_resident(c.shape) for c in (w, b, g, beta)],
        out_specs=pl.BlockSpec((SAMPLE_ROWS, CC_CH), lambda i: (i, 0)),
        out_shape=jax.ShapeDtypeStruct((n_req * 8, CC_CH), BF16),
        scratch_shapes=[pltpu.VMEM((SAMPLE_BLOCK, CC_HIST + 8, CC_CH), F32)],
        compiler_params=_cparams(1),
    )(u3, prev4, w, b, g, beta)


def _rope_tables(pos):
    half = ROT_DIM // 2
    inv_freq = ROPE_THETA ** (-jnp.arange(half, dtype=F32) * 2.0 / ROT_DIM)
    ang = pos.astype(F32)[:, None] * inv_freq[None, :]
    cos, sin = jnp.cos(ang), jnp.sin(ang)
    n = pos.shape[0]
    pad = jnp.zeros((n, DA_HEAD_DIM - ROT_DIM), F32)
    zeros = jnp.zeros((n, half), F32)
    cos_c = jnp.concatenate([cos, cos, pad + 1.0], axis=1)
    up_c = jnp.concatenate([-sin, zeros, pad], axis=1)
    dn_c = jnp.concatenate([zeros, sin, pad], axis=1)
    return tuple(jnp.concatenate([t, t], axis=1) for t in (cos_c, up_c, dn_c))


def _head_lanes(vec):
    v = vec.astype(F32).reshape(SSD_GROUPS, HEADS_PER_GROUP)
    return jnp.pad(v, ((0, 0), (0, LANES - HEADS_PER_GROUP))).reshape(1, SSD_GROUPS * LANES)


def _sublane_repeat(w):
    w = w.astype(F32)
    return jnp.broadcast_to(w[..., None, :], w.shape[:-1] + (8, w.shape[-1]))


def _prepare_layer(l, p):
    pts = [0]
    for s in IN_SIZES:
        pts.append(pts[-1] + s)
    w_in = p['w_in'][l]
    cols = [w_in[:, pts[i]:pts[i + 1]] for i in range(len(IN_SIZES))]
    w_dt = cols[2].reshape(D_MODEL, SSD_GROUPS, HEADS_PER_GROUP)
    w_dt = jnp.pad(w_dt, ((0, 0), (0, 0), (0, LANES - HEADS_PER_GROUP))).reshape(D_MODEL, SSD_GROUPS * LANES)
    row = lambda v: v.astype(F32).reshape(1, -1)
    lam_init = 0.8 - 0.6 * math.exp(-0.3 * l)
    lam = (jnp.exp(jnp.sum(p['da_lambda_q1'][l].astype(F32) * p['da_lambda_k1'][l].astype(F32)))
           - jnp.exp(jnp.sum(p['da_lambda_q2'][l].astype(F32) * p['da_lambda_k2'][l].astype(F32))) + lam_init)
    return dict(
        lam_init=lam_init, lam=lam.reshape(1).astype(F32),
        ln_g=[row(p['ln_g'][l, i]) for i in range(3)], ln_b=[row(p['ln_b'][l, i]) for i in range(3)],
        ff1_gu=p['w_ff1_gu'][l].astype(BF16), ff1_d=p['w_ff1_down'][l].astype(BF16),
        ff2_gu=p['w_ff2_gu'][l].astype(BF16), ff2_d=p['w_ff2_down'][l].astype(BF16),
        w_z=cols[0].astype(BF16), w_xbc=cols[1].astype(BF16), w_dt=w_dt.astype(BF16),
        w_q=cols[3].astype(BF16), w_k=cols[4].astype(BF16), w_v=cols[5].astype(BF16),
        w_ga=cols[6][:, :CC_CH].astype(BF16), w_gb=cols[6][:, CC_CH:].astype(BF16),
        w_gate=cols[7].astype(BF16), b_gate=row(p['b_gate'][l]),
        dt_bias=_head_lanes(p['ssd_dt_bias'][l]), a4=_head_lanes(-jnp.exp(p['ssd_a_log'][l].astype(F32))),
        conv_w=_sublane_repeat(p['ssd_conv_w'][l]), conv_b=_sublane_repeat(p['ssd_conv_b'][l]),
        d_e=jnp.repeat(p['ssd_d'][l].astype(F32), SSD_HEAD_DIM).reshape(1, SSD_INNER),
        norm_w=row(p['ssd_norm_w'][l]), ssd_out=p['ssd_out'][l].astype(BF16),
        subln_w=row(p['da_subln_w'][l]), da_out=p['da_out'][l].astype(BF16),
        cc_w=_sublane_repeat(p['cc_conv_w'][l]), cc_b=_sublane_repeat(p['cc_conv_b'][l]),
        cc_g=row(p['cc_ln_g'][l]), cc_beta=row(p['cc_ln_b'][l]), cc_out=p['cc_out'][l].astype(BF16),
        w_out=p['w_out'][l].astype(BF16),
    )


def _mixer_tail(h1, ys, yo, yc, lp, alpha, tm_merge, tm_ffn, drop_meta=None):
    h2 = _merge(h1, ys, yo, yc, lp['w_gate'], lp['b_gate'], lp['ssd_out'], lp['da_out'], lp['cc_out'], lp['w_out'],
                lp['ln_g'][1], lp['ln_b'][1], tm=tm_merge, alpha=alpha)
    if drop_meta is not None:
        batch, seq, sub = drop_meta
        return _ffn_last(h2, lp['ff2_gu'], lp['ff2_d'], lp['ln_g'][2], lp['ln_b'][2], batch=batch, seq=seq,
                         sub=sub, alpha=alpha)
    return _ffn_ln(h2, lp['ff2_gu'], lp['ff2_d'], lp['ln_g'][2], lp['ln_b'][2], tm=tm_ffn, alpha=alpha)


def kernel(x_prompt, x_sample, cache_k, cache_v, state_ssm, state_ssd_conv, state_conf_conv, page_table,
           meta_tokens, ln_g, ln_b, w_ff1_gu, w_ff1_down, w_ff2_gu, w_ff2_down, w_in, b_gate,
           ssd_conv_w, ssd_conv_b, ssd_dt_bias, ssd_a_log, ssd_d, ssd_norm_w, ssd_out,
           da_lambda_q1, da_lambda_k1, da_lambda_q2, da_lambda_k2, da_subln_w, da_out,
           cc_conv_w, cc_conv_b, cc_ln_g, cc_ln_b, cc_out, w_out):
    params = dict(ln_g=ln_g, ln_b=ln_b, w_ff1_gu=w_ff1_gu, w_ff1_down=w_ff1_down, w_ff2_gu=w_ff2_gu,
                  w_ff2_down=w_ff2_down, w_in=w_in, b_gate=b_gate, ssd_conv_w=ssd_conv_w, ssd_conv_b=ssd_conv_b,
                  ssd_dt_bias=ssd_dt_bias, ssd_a_log=ssd_a_log, ssd_d=ssd_d, ssd_norm_w=ssd_norm_w, ssd_out=ssd_out,
                  da_lambda_q1=da_lambda_q1, da_lambda_k1=da_lambda_k1, da_lambda_q2=da_lambda_q2,
                  da_lambda_k2=da_lambda_k2, da_subln_w=da_subln_w, da_out=da_out, cc_conv_w=cc_conv_w,
                  cc_conv_b=cc_conv_b, cc_ln_g=cc_ln_g, cc_ln_b=cc_ln_b, cc_out=cc_out, w_out=w_out)
    depth = w_in.shape[0]
    alpha = (2.0 * depth) ** 0.25
    bp, seq = x_prompt.shape[0], x_prompt.shape[1]
    lp_len = seq + N_META
    ds, dec_len = x_sample.shape[0], x_sample.shape[1]
    assert dec_len == 8 and lp_len % 48 == 0 and seq % (2 * SSD_CHUNK) == 0 and ds % SAMPLE_BLOCK == 0
    n_pages = page_table.shape[1]
    past = n_pages * cache_k.shape[2]
    n_pool = cache_k.shape[1]

    tm_p = lp_len // 3
    tm_pm = 384
    tm_s = 512
    assert (bp * lp_len) % tm_pm == 0 and (ds * dec_len) % tm_s == 0

    tm_seq = 512
    assert seq % tm_seq == 0
    hp = None
    hs = x_sample.astype(F32).reshape(ds * dec_len, D_MODEL)
    tab_p = _rope_tables(jnp.arange(lp_len))
    tab_s = tuple(jnp.tile(t, (tm_s // dec_len, 1)) for t in _rope_tables(past + jnp.arange(dec_len)))
    ck = cache_k.reshape(depth, n_pool, PAGE_SIZE * DA_HEADS, HEAD_W)
    cv = cache_v.reshape(depth, n_pool, PAGE_SIZE * DA_HEADS, HEAD_W)
    pt_flat = page_table.reshape(-1).astype(jnp.int32)

    h0_all = state_ssm.astype(F32).reshape(depth, ds, SSD_INNER, SSD_STATE)
    conv_prev4 = jnp.pad(state_ssd_conv.astype(F32), ((0, 0), (0, 0), (8 - (SSD_CONV - 1), 0), (0, 0)))
    cc_prev4 = jnp.pad(state_conf_conv.astype(F32), ((0, 0), (0, 0), (CC_HIST - (CC_KERNEL - 1), 0), (0, 0)))
    kv_p, h1_buf = None, None

    outs = {k: [] for k in ('ks', 'vs', 'hp', 'cp', 'cs', 'up', 'us')}
    for l in range(depth):
        lp = _prepare_layer(l, params)
        post = 1.0 - lp['lam_init']

        if l == 0:
            h1 = _ffn_first(x_prompt.astype(F32).reshape(bp * seq, D_MODEL), meta_tokens.astype(F32),
                            lp['ff1_gu'], lp['ff1_d'], lp['ln_g'][0], lp['ln_b'][0],
                            batch=bp, seq=seq, sub=tm_seq, alpha=alpha)
        else:
            h1 = _ffn_ln(hp, lp['ff1_gu'], lp['ff1_d'], lp['ln_g'][0], lp['ln_b'][0], tm=tm_p, alpha=alpha)
        zs, xs, bcm, dt, conv_tail = _proj_ssd_conv(h1, lp['w_z'], lp['w_xbc'], lp['w_dt'], lp['dt_bias'],
                                                     lp['conv_w'], lp['conv_b'], tm=tm_p, tiles_per_seq=lp_len // tm_p)
        q, k, v, yc, u_tail = _proj_mix_cc(h1, tab_p, lp['w_q'], lp['w_k'], lp['w_v'], lp['w_ga'], lp['w_gb'],
                                           lp['cc_w'], lp['cc_b'], lp['cc_g'], lp['cc_beta'],
                                           tm=tm_p, tiles_per_seq=lp_len // tm_p)
        ys, hfin = _ssd_prompt(xs, bcm, zs, dt, lp['a4'], lp['d_e'], lp['norm_w'], batch=bp, seq=lp_len)
        att = _attn_prompt(lp['lam'], q, k, v, lp['subln_w'], batch=bp, seq=lp_len, post_scale=post,
                           prev_kv=kv_p)
        yo = att[0]
        kv_p = (k[None], v[None]) if kv_p is None else (att[1], att[2])
        hp = _mixer_tail(h1, ys, yo, yc, lp, alpha, tm_pm, tm_p,
                         drop_meta=(bp, seq, tm_seq) if l == depth - 1 else None)
        outs['hp'].append(hfin.reshape(bp, SSD_HEADS, SSD_HEAD_DIM, SSD_STATE))
        outs['cp'].append(conv_tail.reshape(bp, 8, SSD_XBC)[:, 8 - (SSD_CONV - 1):])
        outs['up'].append(u_tail.reshape(bp, CC_HIST, CC_CH)[:, CC_HIST - (CC_KERNEL - 1):])

        h1 = _ffn_ln(hs, lp['ff1_gu'], lp['ff1_d'], lp['ln_g'][0], lp['ln_b'][0], tm=tm_s, alpha=alpha)
        z, xbc, dt = _proj_ssd(h1, lp['w_z'], lp['w_xbc'], lp['w_dt'], lp['dt_bias'], tm=tm_s)
        q, k, v, u = _proj_mix(h1, tab_s, lp['w_q'], lp['w_k'], lp['w_v'], lp['w_ga'], lp['w_gb'],
                               tm=tm_s, q_dtype=F32)
        xbc3 = xbc.reshape(ds, dec_len, SSD_XBC)
        ys, h1_buf = _ssd_sample(xbc3, conv_prev4, z, dt, h0_all, lp['conv_w'], lp['conv_b'], lp['a4'], lp['d_e'],
                                 lp['norm_w'], layer=l, prev_h1=h1_buf)
        yo = _attn_sample(pt_flat, lp['lam'], q, k.reshape(ds, dec_len * DA_HEADS, HEAD_W),
                          v.reshape(ds, dec_len * DA_HEADS, HEAD_W), lp['subln_w'], ck, cv,
                          layer=l, post_scale=post)
        u3 = u.reshape(ds, dec_len, CC_CH)
        yc = _cc_sample(u3, cc_prev4, lp['cc_w'], lp['cc_b'], lp['cc_g'], lp['cc_beta'], layer=l)
        hs = _mixer_tail(h1, ys, yo, yc, lp, alpha, tm_s, tm_s)
        outs['ks'].append(k.reshape(ds, dec_len, DA_HEADS, HEAD_W))
        outs['vs'].append(v.reshape(ds, dec_len, DA_HEADS, HEAD_W))
        outs['cs'].append(xbc3)
        outs['us'].append(u3)

    y_prompt = hp.reshape(bp, seq, D_MODEL)
    y_sample = hs.reshape(ds, dec_len, D_MODEL)
    st = {k: jnp.stack(v) for k, v in outs.items()}
    k_prompt = kv_p[0].reshape(depth, bp, lp_len, DA_HEADS, HEAD_W)
    v_prompt = kv_p[1].reshape(depth, bp, lp_len, DA_HEADS, HEAD_W)
    ssm_sample = h1_buf.reshape(depth, ds, SSD_HEADS, SSD_HEAD_DIM, SSD_STATE)
    conv_sample = jnp.concatenate([state_ssd_conv.astype(F32), st['cs']], axis=2)[:, :, -(SSD_CONV - 1):]
    cc_sample = jnp.concatenate([state_conf_conv.astype(F32), st['us']], axis=2)[:, :, -(CC_KERNEL - 1):]
    return (y_prompt, y_sample, k_prompt, v_prompt, st['ks'], st['vs'], st['hp'], ssm_sample,
            st['cp'], conv_sample, st['up'], cc_sample)
```

```python
import functools
import math

import jax
import jax.numpy as jnp
from jax import lax
from jax.experimental import pallas as pl
from jax.experimental.pallas import tpu as pltpu

F32 = jnp.float32
BF16 = jnp.bfloat16

D_MODEL = 1024
N_META = 16
SSD_INNER = 2048
SSD_HEAD_DIM = 64
SSD_HEADS = 32
SSD_STATE = 128
SSD_GROUPS = 4
SSD_CONV = 4
SSD_CHUNK = 128
SSD_UNROLL = 16
SSD_XBC = SSD_INNER + 2 * SSD_GROUPS * SSD_STATE
GROUP_CH = SSD_INNER // SSD_GROUPS
HEADS_PER_GROUP = SSD_HEADS // SSD_GROUPS
DA_HEAD_DIM = 64
DA_HEADS = 8
DA_WIDTH = 1024
HEAD_W = 2 * DA_HEAD_DIM
ROT_DIM = DA_HEAD_DIM // 4
ROPE_THETA = 500000.0
Q_BLOCK = 256
PAGE_SIZE = 128
CC_CH = 1024
CC_KERNEL = 31
FF_HIDDEN = 2816
N_BRANCH = 3
IN_SIZES = (SSD_INNER, SSD_XBC, SSD_HEADS, DA_WIDTH, DA_WIDTH, DA_WIDTH, 2 * CC_CH, N_BRANCH * D_MODEL)
LN_EPS = 1e-5

VMEM_LIMIT_V7X = 52 * 1024 * 1024
LANES = 128
SAMPLE_BLOCK = 16
SAMPLE_ROWS = SAMPLE_BLOCK * 8


def _cparams(n_axes):
    return pltpu.CompilerParams(dimension_semantics=("arbitrary",) * n_axes,
                                vmem_limit_bytes=VMEM_LIMIT_V7X)


def _resident(shape):
    nd = len(shape)
    return pl.BlockSpec(shape, lambda *_: (0,) * nd, pipeline_mode=pl.Buffered(1))


def _dot(a, b):
    return jnp.dot(a, b, preferred_element_type=F32)


def _dot_nt(a, b):
    return lax.dot_general(a, b, (((1,), (1,)), ((), ())), preferred_element_type=F32)


def _dot_exact(a, b):
    return jnp.dot(a, b, preferred_element_type=F32, precision=lax.Precision.HIGHEST)


def _sigmoid(x):
    return 0.5 * jnp.tanh(0.5 * x) + 0.5


def _silu(x):
    return x * _sigmoid(x)


def _layer_norm(x, g, b):
    xc = x - jnp.mean(x, axis=-1, keepdims=True)
    var = jnp.mean(xc * xc, axis=-1, keepdims=True)
    return xc * lax.rsqrt(var + LN_EPS) * g + b


FF_CHUNK = 256


def _ffn_rows(x, wgu_ref, wd_ref, g_ref, b_ref, acc_ref, alpha):
    rows = x.shape[0]
    xb = x.astype(BF16)
    for j in range(FF_HIDDEN // FF_CHUNK):
        lo = j * FF_CHUNK
        gate = _dot(xb, wgu_ref[:, lo:lo + FF_CHUNK])
        up = _dot(xb, wgu_ref[:, FF_HIDDEN + lo:FF_HIDDEN + lo + FF_CHUNK])
        act = (_silu(gate) * up).astype(BF16)
        down = _dot(act, wd_ref[lo:lo + FF_CHUNK, :])
        if j == 0:
            acc_ref[0:rows, :] = down
        else:
            acc_ref[0:rows, :] += down
    return _layer_norm(alpha * x + 0.5 * acc_ref[0:rows, :], g_ref[...], b_ref[...])


def _ffn_kernel(x_ref, wgu_ref, wd_ref, g_ref, b_ref, o_ref, acc_ref, *, alpha):
    o_ref[...] = _ffn_rows(x_ref[...], wgu_ref, wd_ref, g_ref, b_ref, acc_ref, alpha)


def _ffn_first_kernel(x_ref, meta_ref, wgu_ref, wd_ref, g_ref, b_ref, o_ref, acc_ref, *, alpha):
    j = pl.program_id(1)
    sub = x_ref.shape[0]

    @pl.when(j == 0)
    def _():
        o_ref[0:N_META, :] = meta_ref[...]

    o_ref[pl.ds(pl.multiple_of(N_META + j * sub, 16), sub), :] = _ffn_rows(
        x_ref[...], wgu_ref, wd_ref, g_ref, b_ref, acc_ref, alpha)


def _ffn_first(x, meta, wgu, wd, g, b, *, batch, seq, sub, alpha):
    meta = _ffn_ln(meta, wgu, wd, g, b, tm=meta.shape[0], alpha=alpha)
    consts = (meta, wgu, wd, g, b)
    return pl.pallas_call(
        functools.partial(_ffn_first_kernel, alpha=alpha),
        grid=(batch, seq // sub),
        in_specs=[pl.BlockSpec((sub, D_MODEL), lambda i, j: (i * (seq // sub) + j, 0))]
        + [_resident(c.shape) for c in consts],
        out_specs=pl.BlockSpec((N_META + seq, D_MODEL), lambda i, j: (i, 0)),
        out_shape=jax.ShapeDtypeStruct((batch * (N_META + seq), D_MODEL), F32),
        scratch_shapes=[pltpu.VMEM((sub, D_MODEL), F32)],
        compiler_params=_cparams(2),
    )(x, *consts)


def _ffn_last_kernel(x_ref, wgu_ref, wd_ref, g_ref, b_ref, o_ref, acc_ref, *, alpha):
    sub = o_ref.shape[0]
    x = x_ref[pl.ds(pl.multiple_of(N_META + pl.program_id(1) * sub, 16), sub), :]
    o_ref[...] = _ffn_rows(x, wgu_ref, wd_ref, g_ref, b_ref, acc_ref, alpha)


def _ffn_last(x, wgu, wd, g, b, *, batch, seq, sub, alpha):
    consts = (wgu, wd, g, b)
    return pl.pallas_call(
        functools.partial(_ffn_last_kernel, alpha=alpha),
        grid=(batch, seq // sub),
        in_specs=[pl.BlockSpec((N_META + seq, D_MODEL), lambda i, j: (i, 0))] + [_resident(c.shape) for c in consts],
        out_specs=pl.BlockSpec((sub, D_MODEL), lambda i, j: (i * (seq // sub) + j, 0)),
        out_shape=jax.ShapeDtypeStruct((batch * seq, D_MODEL), F32),
        scratch_shapes=[pltpu.VMEM((sub, D_MODEL), F32)],
        compiler_params=_cparams(2),
    )(x, *consts)


def _ffn_ln(x, wgu, wd, g, b, *, tm, alpha):
    m = x.shape[0]
    row = pl.BlockSpec((tm, D_MODEL), lambda i: (i, 0))
    return pl.pallas_call(
        functools.partial(_ffn_kernel, alpha=alpha),
        grid=(m // tm,),
        in_specs=[row, _resident(wgu.shape), _resident(wd.shape), _resident(g.shape), _resident(b.shape)],
        out_specs=row,
        out_shape=jax.ShapeDtypeStruct((m, D_MODEL), F32),
        scratch_shapes=[pltpu.VMEM((tm, D_MODEL), F32)],
        compiler_params=_cparams(1),
    )(x, wgu, wd, g, b)


def _softplus(x):
    return jnp.maximum(x, 0.0) + jnp.log1p(jnp.exp(-jnp.abs(x)))


def _rope_slice(x, cos_t, sin_up, sin_dn):
    return (x * cos_t + pltpu.roll(x, LANES - ROT_DIM // 2, axis=1) * sin_up
            + pltpu.roll(x, ROT_DIM // 2, axis=1) * sin_dn)


def _proj_ssd_kernel(x_ref, wz_ref, wxbc_ref, wdt_ref, dtb_ref, z_ref, xbc_ref, dt_ref):
    xb = x_ref[...].astype(BF16)
    cw = 512
    for c in range(SSD_INNER // cw):
        z_ref[:, c * cw:(c + 1) * cw] = _dot(xb, wz_ref[:, c * cw:(c + 1) * cw])
    for c in range(SSD_XBC // cw):
        xbc_ref[:, c * cw:(c + 1) * cw] = _dot(xb, wxbc_ref[:, c * cw:(c + 1) * cw])
    dt_ref[...] = _softplus(_dot(xb, wdt_ref[...]) + dtb_ref[...])


def _proj_ssd(x, wz, wxbc, wdt, dtb, *, tm):
    m = x.shape[0]

    def row(n):
        return pl.BlockSpec((tm, n), lambda i: (i, 0))

    return pl.pallas_call(
        _proj_ssd_kernel,
        grid=(m // tm,),
        in_specs=[row(D_MODEL), _resident(wz.shape), _resident(wxbc.shape), _resident(wdt.shape),
                  _resident(dtb.shape)],
        out_specs=[row(SSD_INNER), row(SSD_XBC), row(SSD_GROUPS * LANES)],
        out_shape=[jax.ShapeDtypeStruct((m, SSD_INNER), F32), jax.ShapeDtypeStruct((m, SSD_XBC), F32),
                   jax.ShapeDtypeStruct((m, SSD_GROUPS * LANES), F32)],
        compiler_params=_cparams(1),
    )(x, wz, wxbc, wdt, dtb)


def _proj_ssd_conv_kernel(x_ref, wz_ref, wxbc_ref, wdt_ref, dtb_ref, cw_ref, cb_ref,
                          zs_ref, xs_ref, bc_ref, dt_ref, tail_ref, win_ref, carry_ref, *, tiles_per_seq):
    tm = x_ref.shape[0]
    split = (tm // 2 + 15) // 16 * 16
    first = 8 - (SSD_CONV - 1)
    xb = x_ref[...].astype(BF16)

    @pl.when(pl.program_id(0) % tiles_per_seq == 0)
    def _():
        carry_ref[...] = jnp.zeros_like(carry_ref)

    cw = 512
    n_xbc, n_z = SSD_XBC // cw, SSD_INNER // cw

    def project(c):
        cols = slice(c * cw, (c + 1) * cw)
        win_ref[c % 2, 0:8, :] = carry_ref[:, cols]
        win_ref[c % 2, 8:8 + tm, :] = _dot(xb, wxbc_ref[:, cols])

    def conv(c):
        win = win_ref.at[c % 2]
        for j in range(cw // LANES):
            lanes = slice(j * LANES, (j + 1) * LANES)
            col = c * cw + j * LANES
            for r0, n in ((0, split), (split, tm - split)):
                taps = [win[first + k + r0:first + k + r0 + n, lanes] for k in range(SSD_CONV)]
                act = _conv_silu(taps, cw_ref[:, :, col:col + LANES], cb_ref[:, col:col + LANES])
                if col < SSD_INNER:
                    xs_ref[r0:r0 + n, col:col + LANES] = act
                else:
                    bc_ref[r0:r0 + n, col - SSD_INNER:col - SSD_INNER + LANES] = act.astype(BF16)
        tail = win[tm:tm + 8, :]
        tail_ref[:, c * cw:(c + 1) * cw] = tail
        carry_ref[:, c * cw:(c + 1) * cw] = tail

    project(0)
    for c in range(n_xbc):
        if c + 1 < n_xbc:
            project(c + 1)
        if c < n_z:
            zs_ref[:, c * cw:(c + 1) * cw] = _silu(_dot(xb, wz_ref[:, c * cw:(c + 1) * cw])).astype(BF16)
        conv(c)
    dt_ref[...] = _softplus(_dot(xb, wdt_ref[...]) + dtb_ref[...])


def _proj_ssd_conv(x, wz, wxbc, wdt, dtb, conv_w, conv_b, *, tm, tiles_per_seq):
    m = x.shape[0]
    assert tm % 16 == 0

    def row(n):
        return pl.BlockSpec((tm, n), lambda i: (i, 0))

    consts = (wz, wxbc, wdt, dtb, conv_w, conv_b)
    return pl.pallas_call(
        functools.partial(_proj_ssd_conv_kernel, tiles_per_seq=tiles_per_seq),
        grid=(m // tm,),
        in_specs=[row(D_MODEL)] + [_resident(c.shape) for c in consts],
        out_specs=[row(SSD_INNER), row(SSD_INNER), row(2 * SSD_GROUPS * SSD_STATE), row(SSD_GROUPS * LANES),
                   pl.BlockSpec((8, SSD_XBC), lambda i: (i // tiles_per_seq, 0))],
        out_shape=[jax.ShapeDtypeStruct((m, SSD_INNER), BF16), jax.ShapeDtypeStruct((m, SSD_INNER), F32),
                   jax.ShapeDtypeStruct((m, 2 * SSD_GROUPS * SSD_STATE), BF16),
                   jax.ShapeDtypeStruct((m, SSD_GROUPS * LANES), F32),
                   jax.ShapeDtypeStruct((m // (tm * tiles_per_seq) * 8, SSD_XBC), F32)],
        scratch_shapes=[pltpu.VMEM((2, 8 + tm, 512), F32), pltpu.VMEM((8, SSD_XBC), F32)],
        compiler_params=_cparams(1),
    )(x, *consts)


def _proj_mix_kernel(x_ref, cos_ref, sup_ref, sdn_ref, wq_ref, wk_ref, wv_ref, wga_ref, wgb_ref,
                     q_ref, k_ref, v_ref, u_ref):
    xb = x_ref[...].astype(BF16)
    cos_t, sin_up, sin_dn = cos_ref[...], sup_ref[...], sdn_ref[...]
    cw = 256
    for c in range(DA_WIDTH // cw):
        lo = c * cw
        qc = _dot(xb, wq_ref[:, lo:lo + cw])
        kc = _dot(xb, wk_ref[:, lo:lo + cw])
        for s in range(cw // HEAD_W):
            sl = slice(s * HEAD_W, (s + 1) * HEAD_W)
            dst = slice(lo + s * HEAD_W, lo + (s + 1) * HEAD_W)
            q_ref[:, dst] = (_rope_slice(qc[:, sl], cos_t, sin_up, sin_dn)
                             * (DA_HEAD_DIM ** -0.5)).astype(q_ref.dtype)
            k_ref[:, dst] = _rope_slice(kc[:, sl], cos_t, sin_up, sin_dn)
        v_ref[:, lo:lo + cw] = _dot(xb, wv_ref[:, lo:lo + cw])
        ga = _dot(xb, wga_ref[:, lo:lo + cw])
        gb = _dot(xb, wgb_ref[:, lo:lo + cw])
        u_ref[:, lo:lo + cw] = ga * _sigmoid(gb)


def _proj_mix_cc_kernel(x_ref, cos_ref, sup_ref, sdn_ref, wq_ref, wk_ref, wv_ref, wga_ref, wgb_ref,
                        ccw_ref, ccb_ref, g_ref, beta_ref, q_ref, k_ref, v_ref, c_ref, utail_ref,
                        win_ref, conv_ref, *, tiles_per_seq):
    tm = x_ref.shape[0]
    xb = x_ref[...].astype(BF16)
    cos_t, sin_up, sin_dn = cos_ref[...], sup_ref[...], sdn_ref[...]

    @pl.when(pl.program_id(0) % tiles_per_seq == 0)
    def _():
        win_ref[0:CC_HIST, :] = jnp.zeros((CC_HIST, CC_CH), F32)

    pieces = [(r, min(CC_PIECE, tm - r)) for r in range(0, tm, CC_PIECE)]

    def conv(lo, hi):
        for lb in range(lo // LANES, hi // LANES):
            lanes = slice(lb * LANES, (lb + 1) * LANES)
            for r0, n in pieces:
                conv_ref[r0:r0 + n, lanes] = _cc_taps(
                    lambda start, span: win_ref[r0 + start:r0 + start + span, lanes], ccw_ref, ccb_ref, lanes, n)

    cw = 256
    for c in range(DA_WIDTH // cw):
        lo = c * cw
        qc = _dot(xb, wq_ref[:, lo:lo + cw])
        kc = _dot(xb, wk_ref[:, lo:lo + cw])
        for s in range(cw // HEAD_W):
            sl = slice(s * HEAD_W, (s + 1) * HEAD_W)
            dst = slice(lo + s * HEAD_W, lo + (s + 1) * HEAD_W)
            q_ref[:, dst] = (_rope_slice(qc[:, sl], cos_t, sin_up, sin_dn)
                             * (DA_HEAD_DIM ** -0.5)).astype(q_ref.dtype)
            k_ref[:, dst] = _rope_slice(kc[:, sl], cos_t, sin_up, sin_dn)
        v_ref[:, lo:lo + cw] = _dot(xb, wv_ref[:, lo:lo + cw])
        ga = _dot(xb, wga_ref[:, lo:lo + cw])
        gb = _dot(xb, wgb_ref[:, lo:lo + cw])
        win_ref[CC_HIST:CC_HIST + tm, lo:lo + cw] = ga * _sigmoid(gb)
        if c > 0:
            conv(lo - cw, lo)
    conv(DA_WIDTH - cw, DA_WIDTH)
    c_ref[...] = _silu(_layer_norm(conv_ref[...], g_ref[...], beta_ref[...])).astype(c_ref.dtype)
    tail = win_ref[tm:tm + CC_HIST, :]
    utail_ref[...] = tail
    win_ref[0:CC_HIST, :] = tail


def _proj_mix_cc(x, tables, wq, wk, wv, wga, wgb, ccw, ccb, g, beta, *, tm, tiles_per_seq):
    m = x.shape[0]
    n_tab = tables[0].shape[0] // tm
    row = pl.BlockSpec((tm, D_MODEL), lambda i: (i, 0))
    tab = pl.BlockSpec((tm, LANES), lambda i: (i % n_tab, 0))
    w = _resident(wq.shape)
    consts = (ccw, ccb, g, beta)
    return pl.pallas_call(
        functools.partial(_proj_mix_cc_kernel, tiles_per_seq=tiles_per_seq),
        grid=(m // tm,),
        in_specs=[row, tab, tab, tab, w, w, w, w, w] + [_resident(c.shape) for c in consts],
        out_specs=[row, row, row, row, pl.BlockSpec((CC_HIST, CC_CH), lambda i: (i // tiles_per_seq, 0))],
        out_shape=[jax.ShapeDtypeStruct((m, D_MODEL), BF16), jax.ShapeDtypeStruct((m, D_MODEL), F32),
                   jax.ShapeDtypeStruct((m, D_MODEL), F32), jax.ShapeDtypeStruct((m, CC_CH), BF16),
                   jax.ShapeDtypeStruct((m // (tm * tiles_per_seq) * CC_HIST, CC_CH), F32)],
        scratch_shapes=[pltpu.VMEM((CC_HIST + tm, CC_CH), F32), pltpu.VMEM((tm, CC_CH), F32)],
        compiler_params=_cparams(1),
    )(x, *tables, wq, wk, wv, wga, wgb, *consts)


def _proj_mix(x, tables, wq, wk, wv, wga, wgb, *, tm, q_dtype):
    m = x.shape[0]
    n_tab = tables[0].shape[0] // tm
    row = pl.BlockSpec((tm, D_MODEL), lambda i: (i, 0))
    tab = pl.BlockSpec((tm, LANES), lambda i: (i % n_tab, 0))
    w = _resident(wq.shape)
    return pl.pallas_call(
        _proj_mix_kernel,
        grid=(m // tm,),
        in_specs=[row, tab, tab, tab, w, w, w, w, w],
        out_specs=[row, row, row, row],
        out_shape=[jax.ShapeDtypeStruct((m, D_MODEL), q_dtype)] + [jax.ShapeDtypeStruct((m, D_MODEL), F32)] * 3,
        compiler_params=_cparams(1),
    )(x, *tables, wq, wk, wv, wga, wgb)


def _merge_kernel(h_ref, ys_ref, yo_ref, yc_ref, wgate_ref, bgate_ref, wssd_ref, wda_ref, wcc_ref, wout_ref,
                  g_ref, b_ref, o_ref, *, alpha):
    h = h_ref[...]
    hb = h.astype(BF16)
    merged = None
    for i, (y_ref, w_ref) in enumerate(((ys_ref, wssd_ref), (yo_ref, wda_ref), (yc_ref, wcc_ref))):
        sl = slice(i * D_MODEL, (i + 1) * D_MODEL)
        gate = _sigmoid(_dot(hb, wgate_ref[:, sl]) + bgate_ref[:, sl])
        term = gate * _dot(y_ref[...].astype(BF16), w_ref[...])
        merged = term if merged is None else merged + term
    mix = _dot(merged.astype(BF16), wout_ref[...])
    o_ref[...] = _layer_norm(alpha * h + mix, g_ref[...], b_ref[...])


def _merge(h, ys, yo, yc, wgate, bgate, wssd, wda, wcc, wout, g, b, *, tm, alpha):
    m = h.shape[0]

    def row(n):
        return pl.BlockSpec((tm, n), lambda i: (i, 0))

    consts = (wgate, bgate, wssd, wda, wcc, wout, g, b)
    return pl.pallas_call(
        functools.partial(_merge_kernel, alpha=alpha),
        grid=(m // tm,),
        in_specs=[row(D_MODEL), row(SSD_INNER), row(DA_WIDTH), row(CC_CH)] + [_resident(c.shape) for c in consts],
        out_specs=row(D_MODEL),
        out_shape=jax.ShapeDtypeStruct((m, D_MODEL), F32),
        compiler_params=_cparams(1),
    )(h, ys, yo, yc, *consts)


def _pair_cols(x, h0, h1, lane_lo):
    shape = (x.shape[0], LANES)
    return jnp.where(lane_lo[:x.shape[0]], jnp.broadcast_to(x[:, h0:h0 + 1], shape),
                     jnp.broadcast_to(x[:, h1:h1 + 1], shape))


def _ssd_intra(xs, bc, cc, dt, acol, mask, lane_lo):
    arow = acol.T
    bb = bc.astype(BF16)
    cb = _dot_nt(cc.astype(BF16), bb)
    y_in, xdts, aces = [], [], []
    for j in range(HEADS_PER_GROUP // 2):
        h0, h1 = 2 * j, 2 * j + 1
        gs, wide = [], []
        for h in (h0, h1):
            a_h = jnp.broadcast_to(acol[:, h:h + 1], acol.shape)
            wide.append(a_h)
            gs.append((cb * jnp.exp(jnp.where(mask, a_h - arow[h:h + 1, :], -jnp.inf))).astype(BF16))
        gp = jnp.concatenate(gs, axis=1)
        xdt = xs[:, j * LANES:(j + 1) * LANES] * _pair_cols(dt, h0, h1, lane_lo)
        zero = jnp.zeros_like(xdt)
        xbd = jnp.concatenate([jnp.where(lane_lo, xdt, zero), jnp.where(lane_lo, zero, xdt)], axis=0).astype(BF16)
        y_in.append(_dot(gp, xbd))
        xdts.append(xdt)
        aces.append(jnp.where(lane_lo, wide[0], wide[1]))
    return y_in, xdts, aces, bb


def _ssd_finish(ys, xs, gate, d_row, nw_row):
    gated, ss = [], None
    for j, y in enumerate(ys):
        sl = slice(j * LANES, (j + 1) * LANES)
        yz = (y + d_row[:, sl] * xs[:, sl]) * gate[:, sl]
        gated.append(yz)
        ss = yz * yz if ss is None else ss + yz * yz
    rs = lax.rsqrt(jnp.sum(ss, axis=1, keepdims=True) * (1.0 / GROUP_CH) + LN_EPS)
    return [(yz * rs * nw_row[:, j * LANES:(j + 1) * LANES]).astype(BF16) for j, yz in enumerate(gated)]


def _conv_silu(taps, w_ref, b_ref):
    rows, ch = taps[0].shape
    acc = jnp.broadcast_to(b_ref[...][None], (rows // 8, 8, ch))
    for k in range(SSD_CONV):
        acc = acc + w_ref[k][None] * taps[k].reshape(rows // 8, 8, ch)
    return _silu(acc).reshape(rows, ch)


def _ssd_prompt_kernel(xs_ref, bm_ref, cm_ref, zs_ref, dt_ref, a_ref, d_ref, nw_ref, y_ref, hfin_ref, state_ref):
    L = SSD_CHUNK
    rows = lax.broadcasted_iota(jnp.int32, (L, LANES), 0)
    cols = lax.broadcasted_iota(jnp.int32, (L, LANES), 1)
    causal = rows >= cols
    tri = causal.astype(F32)
    lane_lo = cols < SSD_HEAD_DIM
    row_lo = rows < SSD_HEAD_DIM
    state_ref[...] = jnp.zeros_like(state_ref)

    def local_part(r0, meta):
        xs = xs_ref[pl.ds(r0, L), :]
        bc = bm_ref[pl.ds(r0, L), :]
        cc = cm_ref[pl.ds(r0, L), :]
        dt = dt_ref[pl.ds(r0, L), :]
        if meta:
            dt = jnp.where(rows < N_META, dt, 0.0)
        acol = _dot_exact(tri, dt * a_ref[...])
        alast = acol[L - 1:L, :]
        y_in, xdts, aces, bb = _ssd_intra(xs, bc, cc, dt, acol, causal, lane_lo)
        pairs = []
        for j in range(HEADS_PER_GROUP // 2):
            h0, h1 = 2 * j, 2 * j + 1
            ace = aces[j]
            ale = jnp.where(lane_lo[0:1], jnp.broadcast_to(alast[:, h0:h0 + 1], (1, LANES)),
                            jnp.broadcast_to(alast[:, h1:h1 + 1], (1, LANES)))
            xw = xdts[j] * jnp.exp(ale - ace)
            upd = _dot(xw.T.astype(BF16), bb)
            dec = jnp.where(row_lo, jnp.broadcast_to(jnp.exp(alast[:, h0:h0 + 1]), (L, LANES)),
                            jnp.broadcast_to(jnp.exp(alast[:, h1:h1 + 1]), (L, LANES)))
            pairs.append((y_in[j], jnp.exp(ace), upd, dec))
        return xs, cc.astype(BF16), pairs

    def state_part(r0, xs, cb16, pairs):
        ys = []
        for j, (y_loc, eace, upd, dec) in enumerate(pairs):
            st = state_ref[j * LANES:(j + 1) * LANES, :]
            ys.append(y_loc + _dot_nt(cb16, st.astype(BF16)) * eace)
            state_ref[j * LANES:(j + 1) * LANES, :] = st * dec + upd
        outs = _ssd_finish(ys, xs, zs_ref[pl.ds(r0, L), :].astype(F32), d_ref[...], nw_ref[...])
        for j, o in enumerate(outs):
            y_ref[pl.ds(r0, L), j * LANES:(j + 1) * LANES] = o

    state_part(0, *local_part(0, True))

    def body(c, carry):
        starts = [pl.multiple_of(N_META + (SSD_UNROLL * c + i) * L, 16) for i in range(SSD_UNROLL)]
        local = [local_part(r, False) for r in starts]
        for r, loc in zip(starts, local):
            state_part(r, *loc)
        return carry

    n_chunks = (y_ref.shape[0] - N_META) // L
    assert n_chunks % SSD_UNROLL == 0
    lax.fori_loop(0, n_chunks // SSD_UNROLL, body, 0)
    hfin_ref[...] = state_ref[...]


def _ssd_prompt(xs, bcm, zs, dt, a4, d_e, nw, *, batch, seq):
    def seq_block(width, col0=0):
        return pl.BlockSpec((seq, width), lambda b, g: (b, g + col0))

    def par_block(rows, width):
        return pl.BlockSpec((rows, width), lambda b, g: (0, g))

    return pl.pallas_call(
        _ssd_prompt_kernel,
        grid=(batch, SSD_GROUPS),
        in_specs=[seq_block(GROUP_CH), seq_block(SSD_STATE), seq_block(SSD_STATE, SSD_GROUPS),
                  seq_block(GROUP_CH), seq_block(LANES),
                  par_block(1, LANES), par_block(1, GROUP_CH), par_block(1, GROUP_CH)],
        out_specs=[seq_block(GROUP_CH), pl.BlockSpec((None, GROUP_CH, SSD_STATE), lambda b, g: (b, g, 0))],
        out_shape=[jax.ShapeDtypeStruct((batch * seq, SSD_INNER), BF16),
                   jax.ShapeDtypeStruct((batch, SSD_INNER, SSD_STATE), F32)],
        scratch_shapes=[pltpu.VMEM((GROUP_CH, SSD_STATE), F32)],
        compiler_params=_cparams(2),
    )(xs, bcm, bcm, zs, dt, a4, d_e, nw)


def _ssd_sample_kernel(xs_ref, bm_ref, cm_ref, px_ref, pb_ref, pc_ref, z_ref, dt_ref, h0_ref,
                       wx_ref, wb_ref, wc_ref, bx_ref, bb_ref, bc_ref, a_ref, d_ref, nw_ref, *rest, n_prev):
    y_ref, h1_ref, ext_ref, flat_ref, xwt_ref, yst_ref = rest[-6:]
    if n_prev:
        h1_ref[0:n_prev] = rest[0][...]
    R = SAMPLE_ROWS
    rows = lax.broadcasted_iota(jnp.int32, (R, LANES), 0)
    cols = lax.broadcasted_iota(jnp.int32, (R, LANES), 1)
    same = (rows >> 3) == (cols >> 3)
    causal = same & (rows >= cols)
    lane_lo = cols < SSD_HEAD_DIM
    srcs = ((xs_ref, px_ref, 0, GROUP_CH), (bm_ref, pb_ref, GROUP_CH, GROUP_CH + SSD_STATE),
            (cm_ref, pc_ref, GROUP_CH + SSD_STATE, GROUP_CH + 2 * SSD_STATE))
    for cur, prev, lo, hi in srcs:
        ext_ref[:, 0:8, lo:hi] = prev[...]
        ext_ref[:, 8:16, lo:hi] = cur[...]
    first = 8 - (SSD_CONV - 1)
    for k in range(SSD_CONV):
        flat_ref[k] = ext_ref[:, first + k:first + k + 8, :].reshape(R, GROUP_CH + 2 * SSD_STATE)

    def conv(lo, hi, w_ref, b_ref):
        return _conv_silu([flat_ref[k, :, lo:hi] for k in range(SSD_CONV)], w_ref, b_ref)

    xs = conv(0, GROUP_CH, wx_ref, bx_ref)
    bc = conv(GROUP_CH, GROUP_CH + SSD_STATE, wb_ref, bb_ref)
    cc = conv(GROUP_CH + SSD_STATE, GROUP_CH + 2 * SSD_STATE, wc_ref, bc_ref)
    dt = dt_ref[...]
    dta = dt * a_ref[...]
    acol = _dot_exact(causal.astype(F32), dta)
    total = _dot_exact(same.astype(F32), dta)
    y_in, xdts, aces, _ = _ssd_intra(xs, bc, cc, dt, acol, causal, lane_lo)
    for j in range(HEADS_PER_GROUP // 2):
        xw = xdts[j] * jnp.exp(_pair_cols(total, 2 * j, 2 * j + 1, lane_lo) - aces[j])
        xwt_ref[j * LANES:(j + 1) * LANES, :] = xw.T.astype(BF16)
    xwt = xwt_ref[...]
    for i in range(SAMPLE_BLOCK):
        r = slice(8 * i, 8 * i + 8)
        st = h0_ref[i]
        c16 = jnp.concatenate([cc[r], jnp.zeros((8, SSD_STATE), F32)], axis=0).astype(BF16)
        yst_ref[r, :] = _dot_nt(c16, st.astype(BF16))[0:8]
        b_i = jnp.where((rows >> 3) == i, bc, 0.0).astype(BF16)
        upd = _dot(xwt, b_i)
        tot_i = total[8 * i:8 * i + 1, :]
        for h in range(HEADS_PER_GROUP):
            hs = slice(h * SSD_HEAD_DIM, (h + 1) * SSD_HEAD_DIM)
            h1_ref[n_prev, i, hs, :] = st[hs] * jnp.exp(tot_i[:, h:h + 1]) + upd[hs]
    ys = [y_in[j] + yst_ref[:, j * LANES:(j + 1) * LANES] * jnp.exp(aces[j])
          for j in range(HEADS_PER_GROUP // 2)]
    outs = _ssd_finish(ys, xs, _silu(z_ref[...]), d_ref[...], nw_ref[...])
    for j, o in enumerate(outs):
        y_ref[:, j * LANES:(j + 1) * LANES] = o


def _ssd_sample(xbc3, prev4, z, dt, h0_all, conv_w, conv_b, a4, d_e, nw, *, layer, prev_h1=None):
    n_req = xbc3.shape[0]
    n_prev = 0 if prev_h1 is None else prev_h1.shape[0]
    g_b = SSD_INNER // SSD_STATE
    R = SAMPLE_ROWS
    width = GROUP_CH + 2 * SSD_STATE

    def tok_block(w, col0=0):
        return pl.BlockSpec((SAMPLE_BLOCK, 8, w), lambda i, g: (i, 0, g + col0))

    def prev_block(w, col0=0):
        return pl.BlockSpec((None, SAMPLE_BLOCK, 8, w), lambda i, g: (layer, i, 0, g + col0))

    def row_block(w):
        return pl.BlockSpec((R, w), lambda i, g: (i, g))

    def par_block(rows, w, col0=0):
        return pl.BlockSpec((rows, w), lambda i, g: (0, g + col0))

    def tap_block(w, col0=0):
        return pl.BlockSpec((SSD_CONV, 8, w), lambda i, g: (0, 0, g + col0))

    state = pl.BlockSpec((None, SAMPLE_BLOCK, GROUP_CH, SSD_STATE), lambda i, g: (layer, i, g, 0))
    toks = [tok_block(GROUP_CH), tok_block(SSD_STATE, g_b), tok_block(SSD_STATE, g_b + SSD_GROUPS)]
    prevs = [prev_block(GROUP_CH), prev_block(SSD_STATE, g_b), prev_block(SSD_STATE, g_b + SSD_GROUPS)]
    in_specs = toks + prevs + [row_block(GROUP_CH), row_block(LANES), state,
                               tap_block(GROUP_CH), tap_block(SSD_STATE, g_b),
                               tap_block(SSD_STATE, g_b + SSD_GROUPS),
                               par_block(8, GROUP_CH), par_block(8, SSD_STATE, g_b),
                               par_block(8, SSD_STATE, g_b + SSD_GROUPS),
                               par_block(1, LANES), par_block(1, GROUP_CH), par_block(1, GROUP_CH)]
    args = [xbc3, xbc3, xbc3, prev4, prev4, prev4, z, dt, h0_all, conv_w, conv_w, conv_w, conv_b, conv_b, conv_b,
            a4, d_e, nw]
    def stack_block(n):
        return pl.BlockSpec((n, SAMPLE_BLOCK, GROUP_CH, SSD_STATE), lambda i, g: (0, i, g, 0))

    if n_prev:
        in_specs = in_specs + [stack_block(n_prev)]
        args = args + [prev_h1]
    return pl.pallas_call(
        functools.partial(_ssd_sample_kernel, n_prev=n_prev),
        grid=(n_req // SAMPLE_BLOCK, SSD_GROUPS),
        in_specs=in_specs,
        out_specs=[row_block(GROUP_CH), stack_block(n_prev + 1)],
        out_shape=[jax.ShapeDtypeStruct((n_req * 8, SSD_INNER), BF16),
                   jax.ShapeDtypeStruct((n_prev + 1, n_req, SSD_INNER, SSD_STATE), F32)],
        scratch_shapes=[pltpu.VMEM((SAMPLE_BLOCK, 16, width), F32), pltpu.VMEM((SSD_CONV, R, width), F32),
                        pltpu.VMEM((GROUP_CH, R), BF16), pltpu.VMEM((R, GROUP_CH), F32)],
        compiler_params=_cparams(2),
    )(*args)


def _sub_ln(o, w_row, post_scale):
    return o * lax.rsqrt(jnp.mean(o * o, axis=-1, keepdims=True) + LN_EPS) * w_row * post_scale


def _split_components(q):
    lane_lo = lax.broadcasted_iota(jnp.int32, q.shape, 1) < DA_HEAD_DIM
    zero = jnp.zeros_like(q)
    return jnp.concatenate([jnp.where(lane_lo, q, zero), jnp.where(lane_lo, zero, q)], axis=0)


def _attn_prompt_kernel(lam_ref, q_ref, k_ref, v_ref, w_ref, *rest, n_prev, post_scale):
    kb_ref, vb_ref = rest[-2:]
    if n_prev:
        kp_ref, vp_ref, o_ref, kall_ref, vall_ref = rest[:5]
        kall_ref[0:n_prev] = kp_ref[...]
        vall_ref[0:n_prev] = vp_ref[...]
        kall_ref[n_prev] = k_ref[...]
        vall_ref[n_prev] = v_ref[...]
    else:
        o_ref = rest[0]
    lam = lam_ref[0]
    kb_ref[...] = k_ref[...].astype(BF16)
    vb_ref[:, 0:HEAD_W] = v_ref[...].astype(BF16)
    vb_ref[:, HEAD_W:2 * HEAD_W] = jnp.ones((vb_ref.shape[0], HEAD_W), BF16)
    k_meta = kb_ref[0:LANES]
    v_meta = vb_ref[0:LANES]
    w_row = w_ref[...]

    def attend(r0, n_q, parts):
        finish(r0, n_q, soft(parts))

    def soft(parts):
        m = None
        for s, _ in parts:
            pm = jnp.max(s, axis=1, keepdims=True)
            m = pm if m is None else jnp.maximum(m, pm)
        return [(jnp.exp((s - m).astype(BF16)), vals) for s, vals in parts]

    def finish(r0, n_q, probs):
        acc = None
        for p, vals in probs:
            term = _dot(p, vals)
            acc = term if acc is None else acc + term
        on = acc[:, 0:HEAD_W] * (1.0 / acc[:, HEAD_W:HEAD_W + 1])
        o = on[0:n_q] - lam * on[n_q:2 * n_q]
        o_ref[r0:r0 + n_q, :] = _sub_ln(o, w_row, post_scale).astype(o_ref.dtype)

    q2 = _split_components(q_ref[0:N_META])
    s = _dot_nt(q2, k_meta)
    qpos = lax.broadcasted_iota(jnp.int32, s.shape, 0) & (N_META - 1)
    kpos = lax.broadcasted_iota(jnp.int32, s.shape, 1)
    attend(0, N_META, [(jnp.where(kpos <= qpos, s, -jnp.inf), v_meta)])

    n_blk = (q_ref.shape[0] - N_META) // Q_BLOCK
    blk_shape = (2 * Q_BLOCK, Q_BLOCK)
    kcol = lax.broadcasted_iota(jnp.int32, blk_shape, 1)
    meta_ok = lax.broadcasted_iota(jnp.int32, (2 * Q_BLOCK, LANES), 1) < N_META
    diag_ok = kcol <= (lax.broadcasted_iota(jnp.int32, blk_shape, 0) & (Q_BLOCK - 1))

    def scores(i):
        r0 = N_META + i * Q_BLOCK
        q2 = _split_components(q_ref[r0:r0 + Q_BLOCK])
        parts = [(jnp.where(meta_ok, _dot_nt(q2, k_meta), -jnp.inf), v_meta)]
        if i > 0:
            parts.append((_dot_nt(q2, kb_ref[N_META:r0]), vb_ref[N_META:r0]))
        parts.append((jnp.where(diag_ok, _dot_nt(q2, kb_ref[r0:r0 + Q_BLOCK]), -jnp.inf),
                      vb_ref[r0:r0 + Q_BLOCK]))
        return parts

    nxt = scores(0)
    for i in range(n_blk):
        cur, nxt = nxt, (scores(i + 1) if i + 1 < n_blk else None)
        attend(N_META + i * Q_BLOCK, Q_BLOCK, cur)


def _attn_prompt(lam, q, k, v, w_row, *, batch, seq, post_scale, prev_kv=None):
    blk = pl.BlockSpec((seq, HEAD_W), lambda b, h: (b, h))
    in_specs = [pl.BlockSpec(memory_space=pltpu.SMEM), blk, blk, blk, _resident(w_row.shape)]
    args = [lam, q, k, v, w_row]
    out_specs = [blk]
    out_shape = [jax.ShapeDtypeStruct((batch * seq, DA_WIDTH), BF16)]
    n_prev = 0
    if prev_kv is not None:
        n_prev = prev_kv[0].shape[0]
        in_specs += [pl.BlockSpec((n_prev, seq, HEAD_W), lambda b, h: (0, b, h))] * 2
        args += list(prev_kv)
        out_specs += [pl.BlockSpec((n_prev + 1, seq, HEAD_W), lambda b, h: (0, b, h))] * 2
        out_shape += [jax.ShapeDtypeStruct((n_prev + 1, batch * seq, DA_WIDTH), F32)] * 2
    return pl.pallas_call(
        functools.partial(_attn_prompt_kernel, n_prev=n_prev, post_scale=post_scale),
        grid=(batch, DA_HEADS),
        in_specs=in_specs,
        out_specs=out_specs,
        out_shape=out_shape,
        scratch_shapes=[pltpu.VMEM((seq, HEAD_W), BF16), pltpu.VMEM((seq, 2 * HEAD_W), BF16)],
        compiler_params=_cparams(2),
    )(*args)


def _attn_sample_kernel(pt_ref, lam_ref, q_ref, kn_ref, vn_ref, w_ref, *refs, n_pages, post_scale):
    del pt_ref
    k_pages = refs[:n_pages]
    v_pages = refs[n_pages:2 * n_pages]
    o_ref, s_ref = refs[2 * n_pages], refs[2 * n_pages + 1]
    lam = lam_ref[0]
    q = q_ref[...]
    lane = lax.broadcasted_iota(jnp.int32, (8, HEAD_W), 1)
    kpos = lax.broadcasted_iota(jnp.int32, (16, PAGE_SIZE), 1)
    new_ok = kpos <= (lax.broadcasted_iota(jnp.int32, (16, PAGE_SIZE), 0) & 7)
    pad = jnp.zeros((PAGE_SIZE - 8, HEAD_W), F32)
    w_row = w_ref[...]

    def head_rows(ref, h, n):
        return ref[pl.ds(h, n, stride=DA_HEADS), :]

    def keys(h, p):
        if p < n_pages:
            return head_rows(k_pages[p], h, PAGE_SIZE).astype(BF16)
        return jnp.concatenate([head_rows(kn_ref, h, 8), pad], axis=0).astype(BF16)

    def values(h, p):
        if p < n_pages:
            return head_rows(v_pages[p], h, PAGE_SIZE).astype(BF16)
        return jnp.concatenate([head_rows(vn_ref, h, 8), pad], axis=0).astype(BF16)

    def first_pass(h):
        qh = q[:, h * HEAD_W:(h + 1) * HEAD_W]
        q2 = jnp.concatenate([jnp.where(lane < DA_HEAD_DIM, qh, 0.0), jnp.where(lane >= DA_HEAD_DIM, qh, 0.0)],
                             axis=0).astype(BF16)
        m = None
        for p in range(n_pages + 1):
            s = _dot_nt(q2, keys(h, p))
            if p == n_pages:
                s = jnp.where(new_ok, s, -jnp.inf)
            s_ref[h, :, p * PAGE_SIZE:(p + 1) * PAGE_SIZE] = s
            pm = jnp.max(s, axis=1, keepdims=True)
            m = pm if m is None else jnp.maximum(m, pm)
        return m

    def second_pass(h, m):
        den, acc = None, None
        for p in range(n_pages + 1):
            e = jnp.exp(s_ref[h, :, p * PAGE_SIZE:(p + 1) * PAGE_SIZE] - m)
            pd = jnp.sum(e, axis=1, keepdims=True)
            term = _dot(e.astype(BF16), values(h, p))
            den = pd if den is None else den + pd
            acc = term if acc is None else acc + term
        on = acc * (1.0 / den)
        o_ref[:, h * HEAD_W:(h + 1) * HEAD_W] = _sub_ln(on[0:8] - lam * on[8:16], w_row, post_scale)

    m_next = first_pass(0)
    for h in range(DA_HEADS):
        m_cur, m_next = m_next, (first_pass(h + 1) if h + 1 < DA_HEADS else None)
        second_pass(h, m_cur)


def _attn_sample(page_table_flat, lam, q, k_new, v_new, w_row, cache_k, cache_v, *, layer, post_scale):
    n_req = q.shape[0] // 8
    n_pages = page_table_flat.shape[0] // n_req
    page_rows = PAGE_SIZE * DA_HEADS

    def page_spec(j):
        return pl.BlockSpec((None, None, page_rows, HEAD_W), lambda b, pt: (layer, pt[b * n_pages + j], 0, 0))

    tok = pl.BlockSpec((8, DA_WIDTH), lambda b, pt: (b, 0))
    new = pl.BlockSpec((None, 8 * DA_HEADS, HEAD_W), lambda b, pt: (b, 0, 0))
    pages = [page_spec(j) for j in range(n_pages)]
    grid_spec = pltpu.PrefetchScalarGridSpec(
        num_scalar_prefetch=1,
        grid=(n_req,),
        in_specs=[pl.BlockSpec(memory_space=pltpu.SMEM), tok, new, new,
                  pl.BlockSpec(w_row.shape, lambda b, pt: (0, 0))] + pages + pages,
        out_specs=tok,
        scratch_shapes=[pltpu.VMEM((DA_HEADS, 16, (n_pages + 1) * PAGE_SIZE), F32)],
    )
    return pl.pallas_call(
        functools.partial(_attn_sample_kernel, n_pages=n_pages, post_scale=post_scale),
        grid_spec=grid_spec,
        out_shape=jax.ShapeDtypeStruct((n_req * 8, DA_WIDTH), F32),
        compiler_params=_cparams(1),
    )(page_table_flat, lam, q, k_new, v_new, w_row, *([cache_k] * n_pages), *([cache_v] * n_pages))


CC_HIST = 32
CC_PIECE = 96


def _cc_taps(load_window, w_ref, b_ref, lanes, rows):
    first = CC_HIST - (CC_KERNEL - 1)
    nt = rows // 8
    acc = jnp.broadcast_to(b_ref[:, lanes][None], (nt, 8, LANES))
    for phase in range(8):
        n_tap = (CC_KERNEL - 1 - phase) // 8 + 1
        span = rows + 8 * (n_tap - 1)
        win = load_window(first + phase, span).reshape(span // 8, 8, LANES)
        for a in range(n_tap):
            acc = acc + w_ref[8 * a + phase, :, lanes][None] * win[a:a + nt]
    return acc.reshape(rows, LANES)


def _cc_sample_kernel(u_ref, prev_ref, w_ref, b_ref, g_ref, beta_ref, o_ref, ext_ref):
    ext_ref[:, 0:CC_HIST, :] = prev_ref[...]
    ext_ref[:, CC_HIST:CC_HIST + 8, :] = u_ref[...]
    first = CC_HIST - (CC_KERNEL - 1)
    acc = jnp.broadcast_to(b_ref[...][None], (SAMPLE_BLOCK, 8, CC_CH))
    for k in range(CC_KERNEL):
        acc = acc + w_ref[k][None] * ext_ref[:, first + k:first + k + 8, :]
    acc = acc.reshape(SAMPLE_ROWS, CC_CH)
    o_ref[...] = _silu(_layer_norm(acc, g_ref[...], beta_ref[...])).astype(o_ref.dtype)


def _cc_sample(u3, prev4, w, b, g, beta, *, layer):
    n_req = u3.shape[0]
    return pl.pallas_call(
        _cc_sample_kernel,
        grid=(n_req // SAMPLE_BLOCK,),
        in_specs=[pl.BlockSpec((SAMPLE_BLOCK, 8, CC_CH), lambda i: (i, 0, 0)),
                  pl.BlockSpec((None, SAMPLE_BLOCK, CC_HIST, CC_CH), lambda i: (layer, i, 0, 0))]
        + [_resident(c.shape) for c in (w, b, g, beta)],
        out_specs=pl.BlockSpec((SAMPLE_ROWS, CC_CH), lambda i: (i, 0)),
        out_shape=jax.ShapeDtypeStruct((n_req * 8, CC_CH), BF16),
        scratch_shapes=[pltpu.VMEM((SAMPLE_BLOCK, CC_HIST + 8, CC_CH), F32)],
        compiler_params=_cparams(1),
    )(u3, prev4, w, b, g, beta)


def _rope_tables(pos):
    half = ROT_DIM // 2
    inv_freq = ROPE_THETA ** (-jnp.arange(half, dtype=F32) * 2.0 / ROT_DIM)
    ang = pos.astype(F32)[:, None] * inv_freq[None, :]
    cos, sin = jnp.cos(ang), jnp.sin(ang)
    n = pos.shape[0]
    pad = jnp.zeros((n, DA_HEAD_DIM - ROT_DIM), F32)
    zeros = jnp.zeros((n, half), F32)
    cos_c = jnp.concatenate([cos, cos, pad + 1.0], axis=1)
    up_c = jnp.concatenate([-sin, zeros, pad], axis=1)
    dn_c = jnp.concatenate([zeros, sin, pad], axis=1)
    return tuple(jnp.concatenate([t, t], axis=1) for t in (cos_c, up_c, dn_c))


def _head_lanes(vec):
    v = vec.astype(F32).reshape(SSD_GROUPS, HEADS_PER_GROUP)
    return jnp.pad(v, ((0, 0), (0, LANES - HEADS_PER_GROUP))).reshape(1, SSD_GROUPS * LANES)


def _sublane_repeat(w):
    w = w.astype(F32)
    return jnp.broadcast_to(w[..., None, :], w.shape[:-1] + (8, w.shape[-1]))


def _prepare_layer(l, p):
    pts = [0]
    for s in IN_SIZES:
        pts.append(pts[-1] + s)
    w_in = p['w_in'][l]
    cols = [w_in[:, pts[i]:pts[i + 1]] for i in range(len(IN_SIZES))]
    w_dt = cols[2].reshape(D_MODEL, SSD_GROUPS, HEADS_PER_GROUP)
    w_dt = jnp.pad(w_dt, ((0, 0), (0, 0), (0, LANES - HEADS_PER_GROUP))).reshape(D_MODEL, SSD_GROUPS * LANES)
    row = lambda v: v.astype(F32).reshape(1, -1)
    lam_init = 0.8 - 0.6 * math.exp(-0.3 * l)
    lam = (jnp.exp(jnp.sum(p['da_lambda_q1'][l].astype(F32) * p['da_lambda_k1'][l].astype(F32)))
           - jnp.exp(jnp.sum(p['da_lambda_q2'][l].astype(F32) * p['da_lambda_k2'][l].astype(F32))) + lam_init)
    return dict(
        lam_init=lam_init, lam=lam.reshape(1).astype(F32),
        ln_g=[row(p['ln_g'][l, i]) for i in range(3)], ln_b=[row(p['ln_b'][l, i]) for i in range(3)],
        ff1_gu=p['w_ff1_gu'][l].astype(BF16), ff1_d=p['w_ff1_down'][l].astype(BF16),
        ff2_gu=p['w_ff2_gu'][l].astype(BF16), ff2_d=p['w_ff2_down'][l].astype(BF16),
        w_z=cols[0].astype(BF16), w_xbc=cols[1].astype(BF16), w_dt=w_dt.astype(BF16),
        w_q=cols[3].astype(BF16), w_k=cols[4].astype(BF16), w_v=cols[5].astype(BF16),
        w_ga=cols[6][:, :CC_CH].astype(BF16), w_gb=cols[6][:, CC_CH:].astype(BF16),
        w_gate=cols[7].astype(BF16), b_gate=row(p['b_gate'][l]),
        dt_bias=_head_lanes(p['ssd_dt_bias'][l]), a4=_head_lanes(-jnp.exp(p['ssd_a_log'][l].astype(F32))),
        conv_w=_sublane_repeat(p['ssd_conv_w'][l]), conv_b=_sublane_repeat(p['ssd_conv_b'][l]),
        d_e=jnp.repeat(p['ssd_d'][l].astype(F32), SSD_HEAD_DIM).reshape(1, SSD_INNER),
        norm_w=row(p['ssd_norm_w'][l]), ssd_out=p['ssd_out'][l].astype(BF16),
        subln_w=row(p['da_subln_w'][l]), da_out=p['da_out'][l].astype(BF16),
        cc_w=_sublane_repeat(p['cc_conv_w'][l]), cc_b=_sublane_repeat(p['cc_conv_b'][l]),
        cc_g=row(p['cc_ln_g'][l]), cc_beta=row(p['cc_ln_b'][l]), cc_out=p['cc_out'][l].astype(BF16),
        w_out=p['w_out'][l].astype(BF16),
    )


def _mixer_tail(h1, ys, yo, yc, lp, alpha, tm_merge, tm_ffn, drop_meta=None):
    h2 = _merge(h1, ys, yo, yc, lp['w_gate'], lp['b_gate'], lp['ssd_out'], lp['da_out'], lp['cc_out'], lp['w_out'],
                lp['ln_g'][1], lp['ln_b'][1], tm=tm_merge, alpha=alpha)
    if drop_meta is not None:
        batch, seq, sub = drop_meta
        return _ffn_last(h2, lp['ff2_gu'], lp['ff2_d'], lp['ln_g'][2], lp['ln_b'][2], batch=batch, seq=seq,
                         sub=sub, alpha=alpha)
    return _ffn_ln(h2, lp['ff2_gu'], lp['ff2_d'], lp['ln_g'][2], lp['ln_b'][2], tm=tm_ffn, alpha=alpha)


def kernel(x_prompt, x_sample, cache_k, cache_v, state_ssm, state_ssd_conv, state_conf_conv, page_table,
           meta_tokens, ln_g, ln_b, w_ff1_gu, w_ff1_down, w_ff2_gu, w_ff2_down, w_in, b_gate,
           ssd_conv_w, ssd_conv_b, ssd_dt_bias, ssd_a_log, ssd_d, ssd_norm_w, ssd_out,
           da_lambda_q1, da_lambda_k1, da_lambda_q2, da_lambda_k2, da_subln_w, da_out,
           cc_conv_w, cc_conv_b, cc_ln_g, cc_ln_b, cc_out, w_out):
    params = dict(ln_g=ln_g, ln_b=ln_b, w_ff1_gu=w_ff1_gu, w_ff1_down=w_ff1_down, w_ff2_gu=w_ff2_gu,
                  w_ff2_down=w_ff2_down, w_in=w_in, b_gate=b_gate, ssd_conv_w=ssd_conv_w, ssd_conv_b=ssd_conv_b,
                  ssd_dt_bias=ssd_dt_bias, ssd_a_log=ssd_a_log, ssd_d=ssd_d, ssd_norm_w=ssd_norm_w, ssd_out=ssd_out,
                  da_lambda_q1=da_lambda_q1, da_lambda_k1=da_lambda_k1, da_lambda_q2=da_lambda_q2,
                  da_lambda_k2=da_lambda_k2, da_subln_w=da_subln_w, da_out=da_out, cc_conv_w=cc_conv_w,
                  cc_conv_b=cc_conv_b, cc_ln_g=cc_ln_g, cc_ln_b=cc_ln_b, cc_out=cc_out, w_out=w_out)
    depth = w_in.shape[0]
    alpha = (2.0 * depth) ** 0.25
    bp, seq = x_prompt.shape[0], x_prompt.shape[1]
    lp_len = seq + N_META
    ds, dec_len = x_sample.shape[0], x_sample.shape[1]
    assert dec_len == 8 and lp_len % 48 == 0 and seq % (2 * SSD_CHUNK) == 0 and ds % SAMPLE_BLOCK == 0
    n_pages = page_table.shape[1]
    past = n_pages * cache_k.shape[2]
    n_pool = cache_k.shape[1]

    tm_p = lp_len // 3
    tm_pm = 384
    tm_s = 512
    assert (bp * lp_len) % tm_pm == 0 and (ds * dec_len) % tm_s == 0

    tm_seq = 512
    assert seq % tm_seq == 0
    hp = None
    hs = x_sample.astype(F32).reshape(ds * dec_len, D_MODEL)
    tab_p = _rope_tables(jnp.arange(lp_len))
    tab_s = tuple(jnp.tile(t, (tm_s // dec_len, 1)) for t in _rope_tables(past + jnp.arange(dec_len)))
    ck = cache_k.reshape(depth, n_pool, PAGE_SIZE * DA_HEADS, HEAD_W)
    cv = cache_v.reshape(depth, n_pool, PAGE_SIZE * DA_HEADS, HEAD_W)
    pt_flat = page_table.reshape(-1).astype(jnp.int32)

    h0_all = state_ssm.astype(F32).reshape(depth, ds, SSD_INNER, SSD_STATE)
    conv_prev4 = jnp.pad(state_ssd_conv.astype(F32), ((0, 0), (0, 0), (8 - (SSD_CONV - 1), 0), (0, 0)))
    cc_prev4 = jnp.pad(state_conf_conv.astype(F32), ((0, 0), (0, 0), (CC_HIST - (CC_KERNEL - 1), 0), (0, 0)))
    kv_p, h1_buf = None, None

    outs = {k: [] for k in ('ks', 'vs', 'hp', 'cp', 'cs', 'up', 'us')}
    for l in range(depth):
        lp = _prepare_layer(l, params)
        post = 1.0 - lp['lam_init']

        if l == 0:
            h1 = _ffn_first(x_prompt.astype(F32).reshape(bp * seq, D_MODEL), meta_tokens.astype(F32),
                            lp['ff1_gu'], lp['ff1_d'], lp['ln_g'][0], lp['ln_b'][0],
                            batch=bp, seq=seq, sub=tm_seq, alpha=alpha)
        else:
            h1 = _ffn_ln(hp, lp['ff1_gu'], lp['ff1_d'], lp['ln_g'][0], lp['ln_b'][0], tm=tm_p, alpha=alpha)
        zs, xs, bcm, dt, conv_tail = _proj_ssd_conv(h1, lp['w_z'], lp['w_xbc'], lp['w_dt'], lp['dt_bias'],
                                                     lp['conv_w'], lp['conv_b'], tm=tm_p, tiles_per_seq=lp_len // tm_p)
        q, k, v, yc, u_tail = _proj_mix_cc(h1, tab_p, lp['w_q'], lp['w_k'], lp['w_v'], lp['w_ga'], lp['w_gb'],
                                           lp['cc_w'], lp['cc_b'], lp['cc_g'], lp['cc_beta'],
                                           tm=tm_p, tiles_per_seq=lp_len // tm_p)
        ys, hfin = _ssd_prompt(xs, bcm, zs, dt, lp['a4'], lp['d_e'], lp['norm_w'], batch=bp, seq=lp_len)
        att = _attn_prompt(lp['lam'], q, k, v, lp['subln_w'], batch=bp, seq=lp_len, post_scale=post,
                           prev_kv=kv_p)
        yo = att[0]
        kv_p = (k[None], v[None]) if kv_p is None else (att[1], att[2])
        hp = _mixer_tail(h1, ys, yo, yc, lp, alpha, tm_pm, tm_p,
                         drop_meta=(bp, seq, tm_seq) if l == depth - 1 else None)
        outs['hp'].append(hfin.reshape(bp, SSD_HEADS, SSD_HEAD_DIM, SSD_STATE))
        outs['cp'].append(conv_tail.reshape(bp, 8, SSD_XBC)[:, 8 - (SSD_CONV - 1):])
        outs['up'].append(u_tail.reshape(bp, CC_HIST, CC_CH)[:, CC_HIST - (CC_KERNEL - 1):])

        h1 = _ffn_ln(hs, lp['ff1_gu'], lp['ff1_d'], lp['ln_g'][0], lp['ln_b'][0], tm=tm_s, alpha=alpha)
        z, xbc, dt = _proj_ssd(h1, lp['w_z'], lp['w_xbc'], lp['w_dt'], lp['dt_bias'], tm=tm_s)
        q, k, v, u = _proj_mix(h1, tab_s, lp['w_q'], lp['w_k'], lp['w_v'], lp['w_ga'], lp['w_gb'],
                               tm=tm_s, q_dtype=F32)
        xbc3 = xbc.reshape(ds, dec_len, SSD_XBC)
        ys, h1_buf = _ssd_sample(xbc3, conv_prev4, z, dt, h0_all, lp['conv_w'], lp['conv_b'], lp['a4'], lp['d_e'],
                                 lp['norm_w'], layer=l, prev_h1=h1_buf)
        yo = _attn_sample(pt_flat, lp['lam'], q, k.reshape(ds, dec_len * DA_HEADS, HEAD_W),
                          v.reshape(ds, dec_len * DA_HEADS, HEAD_W), lp['subln_w'], ck, cv,
                          layer=l, post_scale=post)
        u3 = u.reshape(ds, dec_len, CC_CH)
        yc = _cc_sample(u3, cc_prev4, lp['cc_w'], lp['cc_b'], lp['cc_g'], lp['cc_beta'], layer=l)
        hs = _mixer_tail(h1, ys, yo, yc, lp, alpha, tm_s, tm_s)
        outs['ks'].append(k.reshape(ds, dec_len, DA_HEADS, HEAD_W))
        outs['vs'].append(v.reshape(ds, dec_len, DA_HEADS, HEAD_W))
        outs['cs'].append(xbc3)
        outs['us'].append(u3)

    y_prompt = hp.reshape(bp, seq, D_MODEL)
    y_sample = hs.reshape(ds, dec_len, D_MODEL)
    st = {k: jnp.stack(v) for k, v in outs.items()}
    k_prompt = kv_p[0].reshape(depth, bp, lp_len, DA_HEADS, HEAD_W)
    v_prompt = kv_p[1].reshape(depth, bp, lp_len, DA_HEADS, HEAD_W)
    ssm_sample = h1_buf.reshape(depth, ds, SSD_HEADS, SSD_HEAD_DIM, SSD_STATE)
    conv_sample = jnp.concatenate([state_ssd_conv.astype(F32), st['cs']], axis=2)[:, :, -(SSD_CONV - 1):]
    cc_sample = jnp.concatenate([state_conf_conv.astype(F32), st['us']], axis=2)[:, :, -(CC_KERNEL - 1):]
    return (y_prompt, y_sample, k_prompt, v_prompt, st['ks'], st['vs'], st['hp'], ssm_sample,
            st['cp'], conv_sample, st['up'], cc_sample)
```
